```python
import numpy as np
import jax
import jax.numpy as jnp
from jax import lax

D_MODEL = 1024
BATCH = 1
SEQ = 16384
DEPTH = 2
DEC_BATCH = 32
DEC_SEQ = 4
PAST_LEN = 16384
PAGE_SIZE = 128

RET_HEADS = 4
RET_HEAD_DIM = 128
RET_W = RET_HEADS * RET_HEAD_DIM
RET_CHUNK = 128
ROPE_BASE = 10000.0
SB_HEADS = 4
SB_HEAD_DIM = 128
SB_W = SB_HEADS * SB_HEAD_DIM
SB_BLOCK = 128
SB_BIAS_INIT = (-3.0, -5.5, -8.0, -10.5)
W_IN_AB = 4 * RET_W + 3 * SB_W
W_MIX_AB = RET_W + SB_W
DIL_HEADS = 16
DIL_HEAD_DIM = 64
DIL_W = DIL_HEADS * DIL_HEAD_DIM
DIL_BRANCHES = ((128, 1), (512, 4), (2048, 16))
DIL_WINDOW_MAX = 2048
DIL_BLOCK = 128
OFFSET_PAD = 1 << 30
D_FF = 4 * D_MODEL
NORM_EPS = 1e-6

kernel_name = 'retnet_stickbreak_dilated_hybrid_step'


def rms_norm(x, gain):
    xf = x.astype(jnp.float32)
    y = xf * lax.rsqrt(jnp.mean(xf * xf, axis=-1, keepdims=True) + NORM_EPS)
    return (y * gain.astype(jnp.float32)).astype(x.dtype)


def head_group_norm(x):
    xf = x.astype(jnp.float32)
    xc = xf - jnp.mean(xf, axis=-1, keepdims=True)
    var = jnp.mean(xc * xc, axis=-1, keepdims=True)
    return (xc * lax.rsqrt(var + NORM_EPS)).astype(x.dtype)


def rotate(x, pos):
    half = x.shape[-1] // 2
    inv_freq = jnp.power(ROPE_BASE, -jnp.linspace(0.0, 1.0, half, dtype=jnp.float32))
    ang = pos.astype(jnp.float32)[:, None] * inv_freq[None, :]
    cos = jnp.cos(ang)[None, :, None, :]
    sin = jnp.sin(ang)[None, :, None, :]
    xf = x.astype(jnp.float32)
    x1, x2 = xf[..., :half], xf[..., half:]
    return jnp.concatenate([x1 * cos - x2 * sin, x2 * cos + x1 * sin], axis=-1).astype(x.dtype)


def retention_decay_logs():
    return jnp.log1p(-jnp.exp2(-5.0 - jnp.arange(RET_HEADS, dtype=jnp.float32)))


def retention(q, k, v, s0):
    b, t, h, _ = q.shape
    dv = v.shape[-1]
    L = RET_CHUNK if t % RET_CHUNK == 0 else t
    n = t // L
    lg = retention_decay_logs()
    i = jnp.arange(L, dtype=jnp.float32)
    rel = i[:, None] - i[None, :]
    intra = jnp.where(rel >= 0, jnp.exp(lg[:, None, None] * jnp.maximum(rel, 0.0)), 0.0)
    q_dec = jnp.exp(lg[None, :] * (i[:, None] + 1.0))
    k_dec = jnp.exp(lg[None, :] * (L - 1.0 - i[:, None]))
    c_dec = jnp.exp(lg * L)

    def chunks(a):
        return a.astype(jnp.float32).reshape(b, n, L, h, a.shape[-1]).transpose(1, 0, 2, 3, 4)

    def step(S, qkv):
        qc, kc, vc = qkv
        att = jnp.einsum('blhd,bmhd->bhlm', qc, kc) * intra[None]
        o = (jnp.einsum('bhlm,bmhe->blhe', att, vc)
             + jnp.einsum('blhd,bhde->blhe', qc * q_dec[None, :, :, None], S))
        S = S * c_dec[None, :, None, None] + jnp.einsum('blhd,blhe->bhde', kc * k_dec[None, :, :, None], vc)
        return S, o

    S, o = lax.scan(step, s0.astype(jnp.float32), (chunks(q), chunks(k), chunks(v)))
    o = o.transpose(1, 0, 2, 3, 4).reshape(b, t, h, dv)
    return o.astype(v.dtype), S.astype(s0.dtype)


def stick_breaking(q, k, v, bias, q_pos, k_pos):
    z = jnp.einsum('bqhd,bkhd->bhqk', q.astype(jnp.float32), k.astype(jnp.float32)) * (q.shape[-1] ** -0.5)
    z = z + bias.astype(jnp.float32)[None, :, None, None]
    mask = k_pos[None, :] < q_pos[:, None]
    log_keep = jnp.where(mask, jax.nn.log_sigmoid(-z), 0.0)
    later = lax.cumsum(log_keep, axis=3, reverse=True) - log_keep
    w = jnp.where(mask, jnp.exp(jax.nn.log_sigmoid(z) + later), 0.0)
    return jnp.einsum('bhqk,bkhd->bqhd', w, v.astype(jnp.float32)).astype(v.dtype)


def dilation_offsets():
    rows = [np.arange(w // d + 1) * d for (w, d) in DIL_BRANCHES]
    n = max(len(r) for r in rows)
    rows = [np.concatenate([r, np.full(n - len(r), OFFSET_PAD)]) for r in rows]
    return np.stack(rows).astype(np.int32)


def dilated_attention(q, k, v, q_idx):
    offs = jnp.asarray(dilation_offsets())
    idx = q_idx[:, None, None] - offs[None]
    valid = idx >= 0
    idx = jnp.maximum(idx, 0)
    kg = jnp.take(k, idx, axis=1)
    vg = jnp.take(v, idx, axis=1)
    z = jnp.einsum('bqhd,bqgjhd->bqhgj', q.astype(jnp.float32), kg.astype(jnp.float32)) * (q.shape[-1] ** -0.5)
    z = jnp.where(valid[None, :, None], z, -jnp.inf)
    m = jnp.max(z, axis=-1, keepdims=True)
    e = jnp.exp(z - m)
    den = jnp.sum(e, axis=-1)
    o_g = jnp.einsum('bqhgj,bqgjhd->bqhgd', e, vg.astype(jnp.float32)) / den[..., None]
    alpha = jax.nn.softmax(m[..., 0] + jnp.log(den), axis=-1)
    return jnp.einsum('bqhg,bqhgd->bqhd', alpha, o_g).astype(v.dtype)


def blocked_over_queries(fn, q, q_pos, block):
    b, t = q.shape[:2]
    if t % block != 0 or t <= block:
        return fn(q, q_pos)
    nb = t // block
    qb = q.reshape(b, nb, block, *q.shape[2:]).swapaxes(0, 1)
    pb = q_pos.reshape(nb, block)
    o = lax.map(lambda a: fn(a[0], a[1]), (qb, pb))
    return o.swapaxes(0, 1).reshape(b, t, *o.shape[3:])


def ab_mixer(a, past_len, ret_state, sb_k_past, sb_v_past, w_in, w_out, sb_bias):
    b, t, _ = a.shape
    pos = past_len + jnp.arange(t, dtype=jnp.int32)
    cuts = [RET_W, 2 * RET_W, 3 * RET_W, 4 * RET_W, 4 * RET_W + SB_W, 4 * RET_W + 2 * SB_W]
    rq, rk, rv, rg, sq, sk, sv = jnp.split(a @ w_in, cuts, axis=-1)
    heads = lambda x, h: x.reshape(b, t, h, -1)
    rq = rotate(heads(rq, RET_HEADS), pos)
    rk = rotate(heads(rk, RET_HEADS), pos) * (RET_HEAD_DIM ** -0.5)
    o_ret, ret_new = retention(rq, rk, heads(rv, RET_HEADS), ret_state)
    y_ret = head_group_norm(o_ret).reshape(b, t, RET_W) * jax.nn.silu(rg)
    sq, sk, sv = heads(sq, SB_HEADS), heads(sk, SB_HEADS), heads(sv, SB_HEADS)
    if sb_k_past is None:
        k_all, v_all = sk, sv
    else:
        k_all = jnp.concatenate([sb_k_past, sk], axis=1)
        v_all = jnp.concatenate([sb_v_past, sv], axis=1)
    k_pos = jnp.arange(past_len + t, dtype=jnp.int32)
    o_sb = blocked_over_queries(lambda qq, pp: stick_breaking(qq, k_all, v_all, sb_bias, pp, k_pos), sq, pos, SB_BLOCK)
    y = jnp.concatenate([y_ret, o_sb.reshape(b, t, SB_W)], axis=-1) @ w_out
    return y, sk, sv, ret_new


def c_mixer(a, k_buf, v_buf, w_in, w_out):
    b, t, _ = a.shape
    q, k, v = jnp.split(a @ w_in, 3, axis=-1)
    q = q.reshape(b, t, DIL_HEADS, DIL_HEAD_DIM)
    k = k.reshape(b, t, DIL_HEADS, DIL_HEAD_DIM)
    v = v.reshape(b, t, DIL_HEADS, DIL_HEAD_DIM)
    if k_buf is None:
        k_src, v_src, offset = k, v, 0
    else:
        k_src = jnp.concatenate([k_buf, k], axis=1)
        v_src = jnp.concatenate([v_buf, v], axis=1)
        offset = k_buf.shape[1]
    q_idx = offset + jnp.arange(t, dtype=jnp.int32)
    o = blocked_over_queries(lambda qq, ii: dilated_attention(qq, k_src, v_src, ii), q, q_idx, DIL_BLOCK)
    keep = min(DIL_WINDOW_MAX, k_src.shape[1])
    return o.reshape(b, t, DIL_W) @ w_out, k_src[:, -keep:], v_src[:, -keep:]


def sq_relu_mlp(a, w1, w2):
    return jnp.square(jax.nn.relu(a @ w1)) @ w2


def setup_inputs(seed: int = 0) -> dict:
    key = jax.random.key(seed)
    ks = jax.random.split(key, 20)
    n_pages = PAST_LEN // PAGE_SIZE
    n_used = DEC_BATCH * n_pages
    n_phys = n_used + max(1, n_used // 4)
    win = min(DIL_WINDOW_MAX, PAST_LEN)
    nrm = lambda k, shape, s: s * jax.random.normal(k, shape, jnp.float32)
    page_table = jax.random.permutation(ks[0], n_phys)[:n_used].reshape(DEC_BATCH, n_pages).astype(jnp.int32)
    return {
        'x_prompt': nrm(ks[1], (BATCH, SEQ, D_MODEL), 1.0),
        'x_sample': nrm(ks[2], (DEC_BATCH, DEC_SEQ, D_MODEL), 1.0),
        'cache_sb_k': nrm(ks[3], (n_phys, PAGE_SIZE, SB_HEADS, SB_HEAD_DIM), 1.0),
        'cache_sb_v': nrm(ks[4], (n_phys, PAGE_SIZE, SB_HEADS, SB_HEAD_DIM), 1.0),
        'state_ret': nrm(ks[5], (DEC_BATCH, RET_HEADS, RET_HEAD_DIM, RET_HEAD_DIM), 0.5),
        'cache_swa_k': nrm(ks[6], (DEC_BATCH, win, DIL_HEADS, DIL_HEAD_DIM), 1.0),
        'cache_swa_v': nrm(ks[7], (DEC_BATCH, win, DIL_HEADS, DIL_HEAD_DIM), 1.0),
        'page_table': page_table,
        'w_in_ab': nrm(ks[8], (D_MODEL, W_IN_AB), D_MODEL ** -0.5),
        'w_out_ab': nrm(ks[9], (W_MIX_AB, D_MODEL), W_MIX_AB ** -0.5),
        'sb_bias': jnp.asarray(SB_BIAS_INIT, jnp.float32) + nrm(ks[18], (SB_HEADS,), 0.1),
        'w_in_c': nrm(ks[10], (D_MODEL, 3 * DIL_W), D_MODEL ** -0.5),
        'w_out_c': nrm(ks[11], (DIL_W, D_MODEL), DIL_W ** -0.5),
        'w_ff1': nrm(ks[12], (DEPTH, D_MODEL, D_FF), D_MODEL ** -0.5),
        'w_ff2': nrm(ks[13], (DEPTH, D_FF, D_MODEL), D_FF ** -0.5),
        'g_mix_pre': 1.0 + nrm(ks[14], (DEPTH, D_MODEL), 0.1),
        'g_mix_post': 1.0 + nrm(ks[15], (DEPTH, D_MODEL), 0.1),
        'g_ffn_pre': 1.0 + nrm(ks[16], (DEPTH, D_MODEL), 0.1),
        'g_ffn_post': 1.0 + nrm(ks[17], (DEPTH, D_MODEL), 0.1),
    }


def reference(x_prompt, x_sample, cache_sb_k, cache_sb_v, state_ret, cache_swa_k, cache_swa_v, page_table,
              w_in_ab, w_out_ab, sb_bias, w_in_c, w_out_c, w_ff1, w_ff2, g_mix_pre, g_mix_post, g_ffn_pre, g_ffn_post):
    n_seq, n_pages = page_table.shape
    past_len = n_pages * cache_sb_k.shape[1]
    sb_k_past = cache_sb_k[page_table].reshape(n_seq, past_len, SB_HEADS, SB_HEAD_DIM)
    sb_v_past = cache_sb_v[page_table].reshape(n_seq, past_len, SB_HEADS, SB_HEAD_DIM)
    ret_zero = jnp.zeros((x_prompt.shape[0], RET_HEADS, RET_HEAD_DIM, RET_HEAD_DIM), state_ret.dtype)
    h_p, h_s = x_prompt, x_sample
    for layer in range(DEPTH):
        a_p = rms_norm(h_p, g_mix_pre[layer])
        a_s = rms_norm(h_s, g_mix_pre[layer])
        if layer % 2 == 0:
            m_p, sb_k_p, sb_v_p, ret_p = ab_mixer(a_p, 0, ret_zero, None, None, w_in_ab, w_out_ab, sb_bias)
            m_s, sb_k_s, sb_v_s, ret_s = ab_mixer(a_s, past_len, state_ret, sb_k_past, sb_v_past, w_in_ab, w_out_ab, sb_bias)
        else:
            m_p, swa_k_p, swa_v_p = c_mixer(a_p, None, None, w_in_c, w_out_c)
            m_s, swa_k_s, swa_v_s = c_mixer(a_s, cache_swa_k, cache_swa_v, w_in_c, w_out_c)
        h_p = h_p + rms_norm(m_p, g_mix_post[layer])
        h_s = h_s + rms_norm(m_s, g_mix_post[layer])
        h_p = h_p + rms_norm(sq_relu_mlp(rms_norm(h_p, g_ffn_pre[layer]), w_ff1[layer], w_ff2[layer]), g_ffn_post[layer])
        h_s = h_s + rms_norm(sq_relu_mlp(rms_norm(h_s, g_ffn_pre[layer]), w_ff1[layer], w_ff2[layer]), g_ffn_post[layer])
    return (h_p, h_s, sb_k_p, sb_v_p, ret_p, swa_k_p, swa_v_p, sb_k_s, sb_v_s, ret_s, swa_k_s, swa_v_s)
```

```python
import functools

import numpy as np
import jax
import jax.numpy as jnp
from jax import lax
from jax.experimental import pallas as pl
from jax.experimental.pallas import tpu as pltpu

F32 = jnp.float32
BF16 = jnp.bfloat16

NORM_EPS = 1e-6
ROPE_BASE = 10000.0
RET_HEADS = 4
SB_HEADS = 4
HEAD_DIM_AB = 128
RET_W = RET_HEADS * HEAD_DIM_AB
SB_W = SB_HEADS * HEAD_DIM_AB
RET_CHUNK = 128
DIL_HEADS = 16
DIL_HEAD_DIM = 64
DIL_W = DIL_HEADS * DIL_HEAD_DIM
DIL_BRANCHES = ((128, 1), (512, 4), (2048, 16))
DIL_WINDOW_MAX = 2048
DIL_SPAN = 128
LANES = 128
DEC_ROWS = 8
VMEM_LIMIT = 56 * 1024 * 1024

_NT = (((1,), (1,)), ((), ()))
_TN = (((0,), (0,)), ((), ()))


def _cparams(sem):
    return pltpu.CompilerParams(dimension_semantics=sem, vmem_limit_bytes=VMEM_LIMIT)


def _bf(x):
    return x.astype(BF16)


def _dot(a, b):
    return jnp.dot(a, b, preferred_element_type=F32)


def _rms(x, gain):
    return x * lax.rsqrt(jnp.mean(x * x, axis=-1, keepdims=True) + NORM_EPS) * gain


def _softplus(z):
    return jnp.maximum(z, 0.0) + jnp.log1p(jnp.exp(-jnp.abs(z)))


def _inproj_ab_kernel(x_ref, g_ref, w_ref, cos_ref, sin_ref,
                      rq_ref, rk_ref, rv_ref, rg_ref, q_hm, k_hm, v_hm, sbk_ref, sbv_ref):
    a = _rms(x_ref[...], g_ref[...])
    p = _dot(_bf(a), w_ref[...])
    cosf = cos_ref[...]
    sinf = sin_ref[...]
    d = HEAD_DIM_AB
    for h in range(RET_HEADS):
        qh = p[:, h * d:(h + 1) * d]
        kh = p[:, RET_W + h * d:RET_W + (h + 1) * d]
        rq_ref[:, h * d:(h + 1) * d] = qh * cosf + pltpu.roll(qh, d // 2, 1) * sinf
        rk_ref[:, h * d:(h + 1) * d] = (kh * cosf + pltpu.roll(kh, d // 2, 1) * sinf) * (d ** -0.5)
    rv_ref[...] = p[:, 2 * RET_W:3 * RET_W]
    rg_ref[...] = p[:, 3 * RET_W:4 * RET_W]
    base = 4 * RET_W
    for h in range(SB_HEADS):
        q_hm[h] = _bf(p[:, base + h * d:base + (h + 1) * d])
        k_hm[h] = _bf(p[:, base + SB_W + h * d:base + SB_W + (h + 1) * d])
        v_hm[h] = _bf(p[:, base + 2 * SB_W + h * d:base + 2 * SB_W + (h + 1) * d])
    sbk_ref[...] = p[:, base + SB_W:base + 2 * SB_W]
    sbv_ref[...] = p[:, base + 2 * SB_W:base + 3 * SB_W]


def _inproj_ab(x, gain, w_bf, cosf, sinf):
    m, dm = x.shape
    n = w_bf.shape[1]
    tm = min(m, 256)
    row = lambda w: pl.BlockSpec((tm, w), lambda i: (i, 0))
    hm = pl.BlockSpec((SB_HEADS, tm, HEAD_DIM_AB), lambda i: (0, i, 0))
    f32_rows = jax.ShapeDtypeStruct((m, RET_W), F32)
    hm_shape = jax.ShapeDtypeStruct((SB_HEADS, m, HEAD_DIM_AB), BF16)
    return pl.pallas_call(
        _inproj_ab_kernel,
        grid=(m // tm,),
        in_specs=[row(dm), pl.BlockSpec((1, dm), lambda i: (0, 0)),
                  pl.BlockSpec((dm, n), lambda i: (0, 0)), row(HEAD_DIM_AB), row(HEAD_DIM_AB)],
        out_specs=[row(RET_W)] * 4 + [hm] * 3 + [row(SB_W)] * 2,
        out_shape=[f32_rows] * 4 + [hm_shape] * 3 + [f32_rows] * 2,
        compiler_params=_cparams(("parallel",)),
        name="inproj_ab",
    )(x, gain, w_bf, cosf, sinf)


def _ret_kernel(rq, rk, rv, rg, s0, intra, qdec, kdec, cdec, y_ref, s_out, s_scr, *, rows, n_chunks):
    step = pl.program_id(1)
    d = HEAD_DIM_AB

    @pl.when(step == 0)
    def _():
        s_scr[...] = s0[0]

    for c in range(n_chunks):
        rs = slice(c * rows, (c + 1) * rows)
        for h in range(RET_HEADS):
            cs = slice(h * d, (h + 1) * d)
            q, k, v, g = rq[0, rs, cs], rk[0, rs, cs], rv[0, rs, cs], rg[0, rs, cs]
            if rows < RET_CHUNK:
                pad = jnp.zeros((RET_CHUNK - rows, d), F32)
                k = jnp.concatenate([k, pad], axis=0)
                v = jnp.concatenate([v, pad], axis=0)
            s_prev = s_scr[h]
            att = lax.dot_general(_bf(q), _bf(k), _NT, preferred_element_type=F32) * intra[h, :rows, :]
            o = _dot(_bf(att), _bf(v)) + _dot(_bf(q * qdec[h, :rows, :]), _bf(s_prev))
            kd = k * kdec[h]
            s_scr[h] = s_prev * cdec[h] + lax.dot_general(_bf(kd), _bf(v), _TN, preferred_element_type=F32)
            xc = o - jnp.mean(o, axis=-1, keepdims=True)
            var = jnp.mean(xc * xc, axis=-1, keepdims=True)
            y_ref[0, rs, cs] = xc * lax.rsqrt(var + NORM_EPS) * (g * jax.nn.sigmoid(g))

    @pl.when(step == pl.num_programs(1) - 1)
    def _():
        s_out[0] = s_scr[...]


def _ret_tables(chunk_len):
    lg = jnp.log1p(-jnp.exp2(-5.0 - jnp.arange(RET_HEADS, dtype=F32)))
    i = jnp.arange(RET_CHUNK, dtype=F32)
    live = i < chunk_len
    rel = i[:, None] - i[None, :]
    intra = jnp.where((rel >= 0) & live[:, None] & live[None, :],
                      jnp.exp(lg[:, None, None] * jnp.maximum(rel, 0.0)), 0.0)
    q_dec = jnp.where(live[None, :], jnp.exp(lg[:, None] * (i[None, :] + 1.0)), 0.0)
    k_dec = jnp.where(live[None, :], jnp.exp(lg[:, None] * (chunk_len - 1.0 - i[None, :])), 0.0)
    c_dec = jnp.exp(lg * chunk_len)
    full = lambda t: jnp.broadcast_to(t[:, :, None], (RET_HEADS, RET_CHUNK, LANES))
    return intra, full(q_dec), full(k_dec), jnp.broadcast_to(c_dec[:, None, None], (RET_HEADS, RET_CHUNK, LANES))


def _retention(rq, rk, rv, rg, s0, chunk_len):
    b, t, _ = rq.shape
    if t % RET_CHUNK == 0:
        rows = RET_CHUNK
        n_chunks = min(8, t // rows)
    else:
        rows, n_chunks = t, 1
    tc = rows * n_chunks
    seq = pl.BlockSpec((1, tc, RET_W), lambda bi, s: (bi, s, 0))
    st = pl.BlockSpec((1, RET_HEADS, HEAD_DIM_AB, HEAD_DIM_AB), lambda bi, s: (bi, 0, 0, 0))
    tab = pl.BlockSpec((RET_HEADS, RET_CHUNK, LANES), lambda bi, s: (0, 0, 0))
    return pl.pallas_call(
        functools.partial(_ret_kernel, rows=rows, n_chunks=n_chunks),
        grid=(b, t // tc),
        in_specs=[seq] * 4 + [st] + [tab] * 4,
        out_specs=[seq, st],
        out_shape=[jax.ShapeDtypeStruct((b, t, RET_W), F32), jax.ShapeDtypeStruct(s0.shape, F32)],
        scratch_shapes=[pltpu.VMEM((RET_HEADS, HEAD_DIM_AB, HEAD_DIM_AB), F32)],
        compiler_params=_cparams(("parallel", "arbitrary")),
        name="retention",
    )(rq, rk, rv, rg, s0, *_ret_tables(chunk_len))


SB_SUB = 128


def _sb_block(q, k_sub, v_sub, bias, tri, carry, acc, mask):
    z = lax.dot_general(q, k_sub, _NT, preferred_element_type=F32) * (HEAD_DIM_AB ** -0.5) + bias
    sp = _softplus(z)
    lk = -sp
    if mask is not None:
        lk = jnp.where(mask, lk, 0.0)
    later = _dot(_bf(lk), tri) + carry
    w = jnp.exp(z - sp + later)
    if mask is not None:
        w = jnp.where(mask, w, 0.0)
    acc = acc + _dot(_bf(w), v_sub)
    carry = carry + jnp.sum(lk, axis=1, keepdims=True)
    return carry, acc


def _sb_prompt_kernel(qi_tab, kj_tab, bias_ref, q_ref, k_ref, v_ref, tri_ref, o_ref, acc_scr, carry_scr, *, tq):
    h = pl.program_id(0)
    p = pl.program_id(1)
    qi = qi_tab[p]
    kj = kj_tab[p]
    bias = bias_ref[h]
    tri = tri_ref[...]
    n_sub = tq // SB_SUB

    @pl.when(kj == qi)
    def _():
        for c in reversed(range(n_sub)):
            lo = c * SB_SUB
            r = tq - lo
            rows = lax.broadcasted_iota(jnp.int32, (r, SB_SUB), 0)
            cols = lax.broadcasted_iota(jnp.int32, (r, SB_SUB), 1)
            mask = cols < rows
            if c == n_sub - 1:
                carry = jnp.zeros((r, LANES), F32)
                acc = jnp.zeros((r, HEAD_DIM_AB), F32)
            else:
                carry = carry_scr[lo:, :]
                acc = acc_scr[lo:, :]
            carry, acc = _sb_block(q_ref[0, lo:, :], k_ref[0, lo:lo + SB_SUB, :], v_ref[0, lo:lo + SB_SUB, :],
                                   bias, tri, carry, acc, mask)
            carry_scr[lo:, :] = carry
            acc_scr[lo:, :] = acc
            if c > 0:
                carry_scr[lo - SB_SUB:lo, :] = jnp.zeros((SB_SUB, LANES), F32)
                acc_scr[lo - SB_SUB:lo, :] = jnp.zeros((SB_SUB, HEAD_DIM_AB), F32)

    @pl.when(kj < qi)
    def _():
        q = q_ref[0]
        for c in reversed(range(n_sub)):
            lo = c * SB_SUB
            carry, acc = _sb_block(q, k_ref[0, lo:lo + SB_SUB, :], v_ref[0, lo:lo + SB_SUB, :],
                                   bias, tri, carry_scr[...], acc_scr[...], None)
            carry_scr[...] = carry
            acc_scr[...] = acc

    @pl.when(kj == 0)
    def _():
        o_ref[...] = acc_scr[...].astype(o_ref.dtype)


def _strict_upper(n):
    i = np.arange(n)
    return jnp.asarray((i[:, None] > i[None, :]).astype(np.float32), BF16)


def _sb_prompt(q_hm, k_hm, v_hm, bias):
    nh, t, d = q_hm.shape
    tq = min(512, t)
    nq = t // tq
    qi_tab = np.concatenate([np.full(i + 1, i) for i in range(nq)]).astype(np.int32)
    kj_tab = np.concatenate([np.arange(i, -1, -1) for i in range(nq)]).astype(np.int32)
    grid_spec = pltpu.PrefetchScalarGridSpec(
        num_scalar_prefetch=2,
        grid=(nh, len(qi_tab)),
        in_specs=[
            pl.BlockSpec(memory_space=pltpu.SMEM),
            pl.BlockSpec((1, tq, d), lambda h, p, qt, kt: (h, qt[p], 0)),
            pl.BlockSpec((1, tq, d), lambda h, p, qt, kt: (h, kt[p], 0)),
            pl.BlockSpec((1, tq, d), lambda h, p, qt, kt: (h, kt[p], 0)),
            pl.BlockSpec((SB_SUB, SB_SUB), lambda h, p, qt, kt: (0, 0)),
        ],
        out_specs=pl.BlockSpec((tq, d), lambda h, p, qt, kt: (qt[p], h)),
        scratch_shapes=[pltpu.VMEM((tq, d), F32), pltpu.VMEM((tq, LANES), F32)],
    )
    return pl.pallas_call(
        functools.partial(_sb_prompt_kernel, tq=tq),
        grid_spec=grid_spec,
        out_shape=jax.ShapeDtypeStruct((t, nh * d), BF16),
        compiler_params=_cparams(("parallel", "arbitrary")),
        name="sb_prompt",
    )(jnp.asarray(qi_tab), jnp.asarray(kj_tab), bias, q_hm, k_hm, v_hm, _strict_upper(SB_SUB))


SB_PAGES_PER_STEP = 8


def _sb_decode_kernel(pt_ref, qbd_ref, bias_ref, knew_ref, vnew_ref, trit_ref, *rest, n_pages_step, page):
    k_pages = rest[:n_pages_step]
    v_pages = rest[n_pages_step:2 * n_pages_step]
    o_ref, acc_scr, carry_scr = rest[2 * n_pages_step:]
    g = pl.program_id(1)
    qbd = qbd_ref[0]
    bias = bias_ref[...]
    trit = trit_ref[...]
    d = HEAD_DIM_AB
    n_cols = SB_HEADS * DEC_ROWS

    def block(k, v, carry, acc, mask):
        zt = _dot(_bf(k), qbd) * (d ** -0.5) + bias
        sp = _softplus(zt)
        lk = -sp
        if mask is not None:
            lk = jnp.where(mask, lk, 0.0)
        later = _dot(trit, _bf(lk)) + carry
        wt = jnp.exp(zt - sp + later)
        if mask is not None:
            wt = jnp.where(mask, wt, 0.0)
        w = jnp.transpose(wt)[:n_cols, :]
        acc = acc + _dot(_bf(w), _bf(v))
        carry = carry + jnp.sum(lk, axis=0, keepdims=True)
        return carry, acc

    @pl.when(g == 0)
    def _():
        pad = jnp.zeros((page - DEC_ROWS, SB_W), F32)
        k = jnp.concatenate([knew_ref[0], pad], axis=0)
        v = jnp.concatenate([vnew_ref[0], pad], axis=0)
        key = lax.broadcasted_iota(jnp.int32, (page, LANES), 0)
        qry = lax.broadcasted_iota(jnp.int32, (page, LANES), 1) & (DEC_ROWS - 1)
        carry, acc = block(k, v, jnp.zeros((1, LANES), F32), jnp.zeros((n_cols, SB_W), F32), key < qry)
        carry_scr[...] = jnp.broadcast_to(carry, carry_scr.shape)
        acc_scr[...] = acc

    carry = carry_scr[0:1, :]
    acc = acc_scr[...]
    for u in reversed(range(n_pages_step)):
        carry, acc = block(k_pages[u][0], v_pages[u][0], carry, acc, None)
    carry_scr[...] = jnp.broadcast_to(carry, carry_scr.shape)
    acc_scr[...] = acc

    @pl.when(g == pl.num_programs(1) - 1)
    def _():
        for h in range(SB_HEADS):
            o_ref[0, :, h * d:(h + 1) * d] = acc_scr[h * DEC_ROWS:(h + 1) * DEC_ROWS, h * d:(h + 1) * d]


def _sb_decode(sq, sk_new, sv_new, cache_k, cache_v, page_table, bias):
    b = sq.shape[0]
    n_pages = page_table.shape[1]
    page = cache_k.shape[1]
    g_pages = min(SB_PAGES_PER_STEP, n_pages)
    n_steps = n_pages // g_pages
    d = HEAD_DIM_AB
    q4 = sq.reshape(b, DEC_ROWS, SB_HEADS, d)
    eye = jnp.eye(SB_HEADS, dtype=F32)
    qbd = jnp.einsum('bihd,hg->bhdgi', q4, eye).reshape(b, SB_W, SB_HEADS * DEC_ROWS)
    qbd = _bf(jnp.pad(qbd, ((0, 0), (0, 0), (0, LANES - SB_HEADS * DEC_ROWS))))
    bias_lane = jnp.pad(jnp.repeat(bias.astype(F32), DEC_ROWS), (0, LANES - SB_HEADS * DEC_ROWS))[None, :]
    i = np.arange(page)
    trit = jnp.asarray((i[None, :] > i[:, None]).astype(np.float32), BF16)

    def page_spec(u):
        return pl.BlockSpec((1, page, SB_W),
                            lambda bi, g, pt: (pt[bi * n_pages + (n_steps - 1 - g) * g_pages + u], 0, 0))

    per_seq = lambda r, w: pl.BlockSpec((1, r, w), lambda bi, g, pt: (bi, 0, 0))
    grid_spec = pltpu.PrefetchScalarGridSpec(
        num_scalar_prefetch=1,
        grid=(b, n_steps),
        in_specs=[per_seq(SB_W, LANES), pl.BlockSpec((1, LANES), lambda bi, g, pt: (0, 0)),
                  per_seq(DEC_ROWS, SB_W), per_seq(DEC_ROWS, SB_W),
                  pl.BlockSpec((page, page), lambda bi, g, pt: (0, 0))]
                 + [page_spec(u) for u in range(g_pages)] * 2,
        out_specs=per_seq(DEC_ROWS, SB_W),
        scratch_shapes=[pltpu.VMEM((SB_HEADS * DEC_ROWS, SB_W), F32), pltpu.VMEM((DEC_ROWS, LANES), F32)],
    )
    return pl.pallas_call(
        functools.partial(_sb_decode_kernel, n_pages_step=g_pages, page=page),
        grid_spec=grid_spec,
        out_shape=jax.ShapeDtypeStruct((b, DEC_ROWS, SB_W), F32),
        compiler_params=_cparams(("parallel", "arbitrary")),
        name="sb_decode",
    )(page_table.reshape(-1), qbd, bias_lane, sk_new, sv_new, trit,
      *([cache_k] * g_pages), *([cache_v] * g_pages))


def _inproj_c_kernel(x_ref, g_ref, w_ref, q_ref, k_ref, v_ref, kf_ref, vf_ref):
    a = _rms(x_ref[...], g_ref[...])
    p = _dot(_bf(a), w_ref[...])
    q_ref[...] = _bf(p[:, :DIL_W])
    k_ref[...] = _bf(p[:, DIL_W:2 * DIL_W])
    v_ref[...] = _bf(p[:, 2 * DIL_W:])
    kf_ref[...] = p[:, DIL_W:2 * DIL_W]
    vf_ref[...] = p[:, 2 * DIL_W:]


def _inproj_c(x, gain, w_bf):
    m, dm = x.shape
    tm = min(m, 256)
    row = pl.BlockSpec((tm, DIL_W), lambda i: (i, 0))
    return pl.pallas_call(
        _inproj_c_kernel,
        grid=(m // tm,),
        in_specs=[pl.BlockSpec((tm, dm), lambda i: (i, 0)), pl.BlockSpec((1, dm), lambda i: (0, 0)),
                  pl.BlockSpec((dm, 3 * DIL_W), lambda i: (0, 0))],
        out_specs=[row] * 5,
        out_shape=[jax.ShapeDtypeStruct((m, DIL_W), BF16)] * 3 + [jax.ShapeDtypeStruct((m, DIL_W), F32)] * 2,
        compiler_params=_cparams(("parallel",)),
        name="inproj_c",
    )(x, gain, w_bf)


def _dil_branch_kernel(q_ref, kp_ref, kc_ref, vp_ref, vc_ref, o_ref, lse_ref):
    i = pl.program_id(1)
    blk = DIL_SPAN
    rows = lax.broadcasted_iota(jnp.int32, (blk, 2 * blk), 0)
    cols = lax.broadcasted_iota(jnp.int32, (blk, 2 * blk), 1)
    dist = blk + rows - cols
    valid = (dist >= 0) & (dist <= DIL_SPAN) & ((cols >= blk) | (i > 0))
    lane = lax.broadcasted_iota(jnp.int32, (blk, LANES), 1)
    first = lane < DIL_HEAD_DIM
    scale = DIL_HEAD_DIM ** -0.5
    for hp in range(DIL_HEADS // 2):
        cs = slice(hp * LANES, (hp + 1) * LANES)
        q = q_ref[:, cs]
        k2 = jnp.concatenate([kp_ref[:, cs], kc_ref[:, cs]], axis=0)
        v2 = jnp.concatenate([vp_ref[:, cs], vc_ref[:, cs]], axis=0)
        zero = jnp.zeros_like(q)
        outs, lses = [], []
        for qh in (jnp.where(first, q, zero), jnp.where(first, zero, q)):
            z = lax.dot_general(qh, k2, _NT, preferred_element_type=F32) * scale
            z = jnp.where(valid, z, -jnp.inf)
            m = jnp.max(z, axis=-1, keepdims=True)
            e = jnp.exp(z - m)
            den = jnp.sum(e, axis=-1, keepdims=True)
            outs.append(_dot(_bf(e), v2) / den)
            lses.append(m + jnp.log(den))
        o_ref[:, cs] = jnp.where(first, outs[0], outs[1])
        lse_ref[:, cs] = jnp.where(first, lses[0], lses[1])


def _dil_branch(q, k, v, dil):
    t = q.shape[0]
    ts = t // dil
    blk = DIL_SPAN
    view = lambda a: a.reshape(ts, dil * DIL_W)
    cur = pl.BlockSpec((blk, DIL_W), lambda r, i: (i, r))
    prev = pl.BlockSpec((blk, DIL_W), lambda r, i: (jnp.maximum(i - 1, 0), r))
    o, lse = pl.pallas_call(
        _dil_branch_kernel,
        grid=(dil, ts // blk),
        in_specs=[cur, prev, cur, prev, cur],
        out_specs=[cur, cur],
        out_shape=[jax.ShapeDtypeStruct((ts, dil * DIL_W), F32)] * 2,
        compiler_params=_cparams(("parallel", "arbitrary")),
        name=f"dilated_d{dil}",
    )(view(q), view(k), view(k), view(v), view(v))
    return o.reshape(t, DIL_W), lse.reshape(t, DIL_W)


DIL_DEC_CHUNK = 512


def _dil_multiplicity(dist):
    cnt = jnp.zeros(dist.shape, F32)
    for window, dil in DIL_BRANCHES:
        hit = (dist >= 0) & (dist <= window) & ((dist & (dil - 1)) == 0)
        cnt = cnt + jnp.where(hit, 1.0, 0.0)
    return cnt


def _dil_decode_kernel(qbd_ref, k_ref, v_ref, kn_ref, vn_ref, o_ref, z_scr, *, n_past, n_new):
    qbd = qbd_ref[0]
    scale = DIL_HEAD_DIM ** -0.5
    ck = DIL_DEC_CHUNK
    n_chunks = n_past // ck
    n_cols = n_new * DIL_HEADS

    def q_of(shape):
        return lax.broadcasted_iota(jnp.int32, shape, 1) >> 4

    def past_count(c):
        key = c * ck + lax.broadcasted_iota(jnp.int32, (ck, LANES), 0)
        return _dil_multiplicity(n_past + q_of((ck, LANES)) - key)

    key_new = lax.broadcasted_iota(jnp.int32, (DEC_ROWS, LANES), 0)
    cnt_new = jnp.where(key_new < n_new, _dil_multiplicity(q_of((DEC_ROWS, LANES)) - key_new), 0.0)

    z_new = _dot(_bf(kn_ref[0]), qbd) * scale
    m = jnp.max(jnp.where(cnt_new > 0, z_new, -jnp.inf), axis=0, keepdims=True)
    for c in range(n_chunks):
        z = _dot(_bf(k_ref[0, c * ck:(c + 1) * ck, :]), qbd) * scale
        z_scr[c * ck:(c + 1) * ck, :] = z
        m = jnp.maximum(m, jnp.max(jnp.where(past_count(c) > 0, z, -jnp.inf), axis=0, keepdims=True))

    p_new = jnp.where(cnt_new > 0, cnt_new * jnp.exp(z_new - m), 0.0)
    den = jnp.sum(p_new, axis=0, keepdims=True)
    ps = []
    for c in range(n_chunks):
        cnt = past_count(c)
        p = jnp.where(cnt > 0, cnt * jnp.exp(z_scr[c * ck:(c + 1) * ck, :] - m), 0.0)
        den = den + jnp.sum(p, axis=0, keepdims=True)
        ps.append(p)
    inv = 1.0 / den
    pad_p = jnp.zeros((LANES - DEC_ROWS, LANES), F32)
    pad_v = jnp.zeros((LANES - DEC_ROWS, DIL_W), F32)
    pn_t = jnp.transpose(jnp.concatenate([p_new * inv, pad_p], axis=0))[:n_cols, :]
    acc = _dot(_bf(pn_t), _bf(jnp.concatenate([vn_ref[0], pad_v], axis=0)))
    for c in range(n_chunks):
        pt = jnp.concatenate([jnp.transpose((ps[c] * inv)[s * LANES:(s + 1) * LANES, :])[:n_cols, :]
                              for s in range(ck // LANES)], axis=1)
        acc = acc + _dot(_bf(pt), _bf(v_ref[0, c * ck:(c + 1) * ck, :]))
    row_head = lax.broadcasted_iota(jnp.int32, (DIL_HEADS, DIL_W), 0)
    lane_head = lax.broadcasted_iota(jnp.int32, (DIL_HEADS, DIL_W), 1) // DIL_HEAD_DIM
    out_row = lax.broadcasted_iota(jnp.int32, (DEC_ROWS, DIL_W), 0)
    out = jnp.zeros((DEC_ROWS, DIL_W), F32)
    for i in range(n_new):
        picked = jnp.sum(jnp.where(row_head == lane_head, acc[i * DIL_HEADS:(i + 1) * DIL_HEADS, :], 0.0),
                         axis=0, keepdims=True)
        out = out + jnp.where(out_row == i, picked, 0.0)
    o_ref[0] = out


def _dil_decode(q, k_new, v_new, cache_k, cache_v, n_new):
    b, n_past, _ = cache_k.shape
    q4 = q[:, :n_new].reshape(b, n_new, DIL_HEADS, DIL_HEAD_DIM)
    eye = jnp.eye(DIL_HEADS, dtype=F32)
    qbd = jnp.einsum('bihd,hg->bhdig', q4, eye).reshape(b, DIL_W, n_new * DIL_HEADS)
    qbd = _bf(jnp.pad(qbd, ((0, 0), (0, 0), (0, LANES - n_new * DIL_HEADS))))
    per_seq = lambda r, w: pl.BlockSpec((1, r, w), lambda bi: (bi, 0, 0))
    return pl.pallas_call(
        functools.partial(_dil_decode_kernel, n_past=n_past, n_new=n_new),
        grid=(b,),
        in_specs=[per_seq(DIL_W, LANES), per_seq(n_past, DIL_W), per_seq(n_past, DIL_W),
                  per_seq(DEC_ROWS, DIL_W), per_seq(DEC_ROWS, DIL_W)],
        out_specs=per_seq(DEC_ROWS, DIL_W),
        out_shape=jax.ShapeDtypeStruct((b, DEC_ROWS, DIL_W), F32),
        scratch_shapes=[pltpu.VMEM((n_past, LANES), F32)],
        compiler_params=_cparams(("parallel",)),
        name="dilated_decode",
    )(qbd, cache_k, cache_v, k_new, v_new)


def _mix_concat(*refs):
    return jnp.concatenate([_bf(r[...]) for r in refs], axis=-1)


def _mix_merge_branches(*refs):
    n = len(refs) // 2
    lses = [r[...] for r in refs[n:]]
    top = functools.reduce(jnp.maximum, lses)
    ws = [jnp.exp(l - top) for l in lses]
    num = functools.reduce(lambda a, c: a + c, [w * r[...] for w, r in zip(ws, refs[:n])])
    return _bf(num / functools.reduce(lambda a, c: a + c, ws))


def _tail_kernel(*refs, n_mix, mix_fn):
    mix_refs = refs[:n_mix]
    (h_ref, wout_ref, gpost_ref, gpre_ref, w1_ref, w2_ref, gffn_ref,
     out_ref, h1_scr, a_scr, acc_scr) = refs[n_mix:]
    f = pl.program_id(1)

    @pl.when(f == 0)
    def _():
        m = _dot(mix_fn(*mix_refs), wout_ref[...])
        h1 = h_ref[...] + _rms(m, gpost_ref[...])
        h1_scr[...] = h1
        a_scr[...] = _bf(_rms(h1, gpre_ref[...]))
        acc_scr[...] = jnp.zeros_like(acc_scr)

    hid = jnp.square(jnp.maximum(_dot(a_scr[...], w1_ref[...]), 0.0))
    acc_scr[...] += _dot(_bf(hid), w2_ref[...])

    @pl.when(f == pl.num_programs(1) - 1)
    def _():
        out_ref[...] = h1_scr[...] + _rms(acc_scr[...], gffn_ref[...])


def _layer_tail(mix, mix_fn, h, w_out, g_post, g_ffn_pre, w1, w2, g_ffn_post, tm):
    m, dm = h.shape
    dff = w1.shape[1]
    tm = min(m, tm)
    tf = min(dff, 1024)
    row = lambda w: pl.BlockSpec((tm, w), lambda i, f: (i, 0))
    gain = pl.BlockSpec((1, dm), lambda i, f: (0, 0))
    return pl.pallas_call(
        functools.partial(_tail_kernel, n_mix=len(mix), mix_fn=mix_fn),
        grid=(m // tm, dff // tf),
        in_specs=[row(a.shape[1]) for a in mix]
                 + [row(dm), pl.BlockSpec(w_out.shape, lambda i, f: (0, 0)), gain, gain,
                    pl.BlockSpec((dm, tf), lambda i, f: (0, f)), pl.BlockSpec((tf, dm), lambda i, f: (f, 0)), gain],
        out_specs=row(dm),
        out_shape=jax.ShapeDtypeStruct((m, dm), F32),
        scratch_shapes=[pltpu.VMEM((tm, dm), F32), pltpu.VMEM((tm, dm), BF16), pltpu.VMEM((tm, dm), F32)],
        compiler_params=_cparams(("parallel", "arbitrary")),
        name="layer_tail",
    )(*mix, h, w_out, g_post, g_ffn_pre, w1, w2, g_ffn_post)


def _rope_tables(pos):
    half = HEAD_DIM_AB // 2
    inv_freq = jnp.power(ROPE_BASE, -jnp.linspace(0.0, 1.0, half, dtype=F32))
    ang = pos.astype(F32)[:, None] * inv_freq[None, :]
    cos, sin = jnp.cos(ang), jnp.sin(ang)
    return jnp.concatenate([cos, cos], axis=-1), jnp.concatenate([-sin, sin], axis=-1)


def _pad_rows(a, rows):
    return jnp.pad(a, ((0, 0), (0, rows - a.shape[1]), (0, 0)))


def kernel(x_prompt, x_sample, cache_sb_k, cache_sb_v, state_ret, cache_swa_k, cache_swa_v, page_table,
           w_in_ab, w_out_ab, sb_bias, w_in_c, w_out_c, w_ff1, w_ff2, g_mix_pre, g_mix_post, g_ffn_pre, g_ffn_post):
    bp, t, dm = x_prompt.shape
    bs, ts, _ = x_sample.shape
    assert bp == 1 and ts <= DEC_ROWS and t % RET_CHUNK == 0
    n_pages = page_table.shape[1]
    page = cache_sb_k.shape[1]
    past_len = n_pages * page
    d = HEAD_DIM_AB
    gain = lambda g, layer: g[layer][None, :].astype(F32)
    w_in_ab_bf, w_out_ab_bf, w_in_c_bf, w_out_c_bf = _bf(w_in_ab), _bf(w_out_ab), _bf(w_in_c), _bf(w_out_c)
    w_ff1_bf, w_ff2_bf = _bf(w_ff1), _bf(w_ff2)

    h_p = x_prompt.reshape(t, dm)
    h_s = x_sample.reshape(bs * ts, dm)

    cos_p, sin_p = _rope_tables(jnp.arange(t, dtype=jnp.int32))
    rq, rk, rv, rg, q_hm, k_hm, v_hm, sb_k_p, sb_v_p = _inproj_ab(h_p, gain(g_mix_pre, 0), w_in_ab_bf, cos_p, sin_p)
    seq = lambda a: a.reshape(1, t, RET_W)
    ret_zero = jnp.zeros((1, RET_HEADS, d, d), F32)
    y_ret_p, ret_p = _retention(seq(rq), seq(rk), seq(rv), seq(rg), ret_zero, float(RET_CHUNK))
    o_sb_p = _sb_prompt(q_hm, k_hm, v_hm, sb_bias.astype(F32))
    h_p = _layer_tail([y_ret_p.reshape(t, RET_W), o_sb_p], _mix_concat, h_p, w_out_ab_bf, gain(g_mix_post, 0),
                      gain(g_ffn_pre, 0), w_ff1_bf[0], w_ff2_bf[0], gain(g_ffn_post, 0), tm=512)

    cos_s, sin_s = _rope_tables(past_len + jnp.arange(ts, dtype=jnp.int32))
    tile_s = lambda a: jnp.tile(a, (bs, 1))
    rq, rk, rv, rg, q_hm, _, _, sb_k_s, sb_v_s = _inproj_ab(h_s, gain(g_mix_pre, 0), w_in_ab_bf,
                                                            tile_s(cos_s), tile_s(sin_s))
    dec = lambda a: _pad_rows(a.reshape(bs, ts, a.shape[-1]), DEC_ROWS)
    y_ret_s, ret_s = _retention(dec(rq), dec(rk), dec(rv), dec(rg), state_ret.astype(F32), float(ts))
    sq_s = jnp.transpose(q_hm.astype(F32), (1, 0, 2)).reshape(bs * ts, SB_W)
    o_sb_s = _sb_decode(dec(sq_s), dec(sb_k_s), dec(sb_v_s),
                        cache_sb_k.reshape(-1, page, SB_W), cache_sb_v.reshape(-1, page, SB_W),
                        page_table, sb_bias.astype(F32))
    undec = lambda a: a[:, :ts].reshape(bs * ts, a.shape[-1])
    h_s = _layer_tail([undec(y_ret_s), undec(o_sb_s)], _mix_concat, h_s, w_out_ab_bf, gain(g_mix_post, 0),
                      gain(g_ffn_pre, 0), w_ff1_bf[0], w_ff2_bf[0], gain(g_ffn_post, 0), tm=512)

    q, k, v, k_f32, v_f32 = _inproj_c(h_p, gain(g_mix_pre, 1), w_in_c_bf)
    branches = [_dil_branch(q, k, v, dil) for _, dil in DIL_BRANCHES]
    mix = [o for o, _ in branches] + [lse for _, lse in branches]
    h_p = _layer_tail(mix, _mix_merge_branches, h_p, w_out_c_bf, gain(g_mix_post, 1),
                      gain(g_ffn_pre, 1), w_ff1_bf[1], w_ff2_bf[1], gain(g_ffn_post, 1), tm=256)
    keep_p = min(DIL_WINDOW_MAX, t)
    swa_k_p = k_f32[t - keep_p:].reshape(1, keep_p, DIL_HEADS, DIL_HEAD_DIM)
    swa_v_p = v_f32[t - keep_p:].reshape(1, keep_p, DIL_HEADS, DIL_HEAD_DIM)

    q, _, _, k_f32, v_f32 = _inproj_c(h_s, gain(g_mix_pre, 1), w_in_c_bf)
    n_past = cache_swa_k.shape[1]
    ck = cache_swa_k.reshape(bs, n_past, DIL_W)
    cv = cache_swa_v.reshape(bs, n_past, DIL_W)
    o_dil_s = _dil_decode(dec(q.astype(F32)), dec(k_f32), dec(v_f32), ck, cv, ts)
    h_s = _layer_tail([undec(o_dil_s)], _mix_concat, h_s, w_out_c_bf, gain(g_mix_post, 1),
                      gain(g_ffn_pre, 1), w_ff1_bf[1], w_ff2_bf[1], gain(g_ffn_post, 1), tm=512)
    keep_s = min(DIL_WINDOW_MAX, n_past + ts)
    swa_k_s = jnp.concatenate([ck, k_f32.reshape(bs, ts, DIL_W)], axis=1)[:, n_past + ts - keep_s:]
    swa_v_s = jnp.concatenate([cv, v_f32.reshape(bs, ts, DIL_W)], axis=1)[:, n_past + ts - keep_s:]

    heads_ab = lambda a, b: a.reshape(b, -1, SB_HEADS, d)
    return (h_p.reshape(1, t, dm), h_s.reshape(bs, ts, dm),
            heads_ab(sb_k_p, 1), heads_ab(sb_v_p, 1), ret_p,
            swa_k_p, swa_v_p,
            heads_ab(sb_k_s, bs), heads_ab(sb_v_s, bs), ret_s,
            swa_k_s.reshape(bs, keep_s, DIL_HEADS, DIL_HEAD_DIM),
            swa_v_s.reshape(bs, keep_s, DIL_HEADS, DIL_HEAD_DIM))
```

```python
import functools

import numpy as np
import jax
import jax.numpy as jnp
from jax import lax
from jax.experimental import pallas as pl
from jax.experimental.pallas import tpu as pltpu

F32 = jnp.float32
BF16 = jnp.bfloat16

NORM_EPS = 1e-6
ROPE_BASE = 10000.0
RET_HEADS = 4
SB_HEADS = 4
HEAD_DIM_AB = 128
RET_W = RET_HEADS * HEAD_DIM_AB
SB_W = SB_HEADS * HEAD_DIM_AB
RET_CHUNK = 128
DIL_HEADS = 16
DIL_HEAD_DIM = 64
DIL_W = DIL_HEADS * DIL_HEAD_DIM
DIL_BRANCHES = ((128, 1), (512, 4), (2048, 16))
DIL_WINDOW_MAX = 2048
DIL_SPAN = 128
LANES = 128
DEC_ROWS = 8
VMEM_LIMIT = 56 * 1024 * 1024

_NT = (((1,), (1,)), ((), ()))
_TN = (((0,), (0,)), ((), ()))


def _cparams(sem):
    return pltpu.CompilerParams(dimension_semantics=sem, vmem_limit_bytes=VMEM_LIMIT)


def _bf(x):
    return x.astype(BF16)


def _dot(a, b):
    return jnp.dot(a, b, preferred_element_type=F32)


def _rms(x, gain):
    return x * lax.rsqrt(jnp.mean(x * x, axis=-1, keepdims=True) + NORM_EPS) * gain


def _softplus(z):
    return jnp.maximum(z, 0.0) + jnp.log1p(jnp.exp(-jnp.abs(z)))


def _inproj_ab_kernel(x_ref, g_ref, w_ref, cos_ref, sin_ref,
                      rq_ref, rk_ref, rv_ref, rg_ref, q_hm, k_hm, v_hm, sbk_ref, sbv_ref):
    a = _rms(x_ref[...], g_ref[...])
    p = _dot(_bf(a), w_ref[...])
    cosf = cos_ref[...]
    sinf = sin_ref[...]
    d = HEAD_DIM_AB
    for h in range(RET_HEADS):
        qh = p[:, h * d:(h + 1) * d]
        kh = p[:, RET_W + h * d:RET_W + (h + 1) * d]
        rq_ref[:, h * d:(h + 1) * d] = qh * cosf + pltpu.roll(qh, d // 2, 1) * sinf
        rk_ref[:, h * d:(h + 1) * d] = (kh * cosf + pltpu.roll(kh, d // 2, 1) * sinf) * (d ** -0.5)
    rv_ref[...] = p[:, 2 * RET_W:3 * RET_W]
    rg_ref[...] = p[:, 3 * RET_W:4 * RET_W]
    base = 4 * RET_W
    for h in range(SB_HEADS):
        q_hm[h] = _bf(p[:, base + h * d:base + (h + 1) * d])
        k_hm[h] = _bf(p[:, base + SB_W + h * d:base + SB_W + (h + 1) * d])
        v_hm[h] = _bf(p[:, base + 2 * SB_W + h * d:base + 2 * SB_W + (h + 1) * d])
    sbk_ref[...] = p[:, base + SB_W:base + 2 * SB_W]
    sbv_ref[...] = p[:, base + 2 * SB_W:base + 3 * SB_W]


def _inproj_ab(x, gain, w_bf, cosf, sinf):
    m, dm = x.shape
    n = w_bf.shape[1]
    tm = min(m, 256)
    row = lambda w: pl.BlockSpec((tm, w), lambda i: (i, 0))
    hm = pl.BlockSpec((SB_HEADS, tm, HEAD_DIM_AB), lambda i: (0, i, 0))
    f32_rows = jax.ShapeDtypeStruct((m, RET_W), F32)
    hm_shape = jax.ShapeDtypeStruct((SB_HEADS, m, HEAD_DIM_AB), BF16)
    return pl.pallas_call(
        _inproj_ab_kernel,
        grid=(m // tm,),
        in_specs=[row(dm), pl.BlockSpec((1, dm), lambda i: (0, 0)),
                  pl.BlockSpec((dm, n), lambda i: (0, 0)), row(HEAD_DIM_AB), row(HEAD_DIM_AB)],
        out_specs=[row(RET_W)] * 4 + [hm] * 3 + [row(SB_W)] * 2,
        out_shape=[f32_rows] * 4 + [hm_shape] * 3 + [f32_rows] * 2,
        compiler_params=_cparams(("parallel",)),
        name="inproj_ab",
    )(x, gain, w_bf, cosf, sinf)


def _ret_kernel(rq, rk, rv, rg, s0, intra, qdec, kdec, cdec, y_ref, s_out, s_scr, *, rows, n_chunks):
    step = pl.program_id(1)
    d = HEAD_DIM_AB

    @pl.when(step == 0)
    def _():
        s_scr[...] = s0[0]

    for c in range(n_chunks):
        rs = slice(c * rows, (c + 1) * rows)
        for h in range(RET_HEADS):
            cs = slice(h * d, (h + 1) * d)
            q, k, v, g = rq[0, rs, cs], rk[0, rs, cs], rv[0, rs, cs], rg[0, rs, cs]
            if rows < RET_CHUNK:
                pad = jnp.zeros((RET_CHUNK - rows, d), F32)
                k = jnp.concatenate([k, pad], axis=0)
                v = jnp.concatenate([v, pad], axis=0)
            s_prev = s_scr[h]
            att = lax.dot_general(_bf(q), _bf(k), _NT, preferred_element_type=F32) * intra[h, :rows, :]
            o = _dot(_bf(att), _bf(v)) + _dot(_bf(q * qdec[h, :rows, :]), _bf(s_prev))
            kd = k * kdec[h]
            s_scr[h] = s_prev * cdec[h] + lax.dot_general(_bf(kd), _bf(v), _TN, preferred_element_type=F32)
            xc = o - jnp.mean(o, axis=-1, keepdims=True)
            var = jnp.mean(xc * xc, axis=-1, keepdims=True)
            y_ref[0, rs, cs] = xc * lax.rsqrt(var + NORM_EPS) * (g * jax.nn.sigmoid(g))

    @pl.when(step == pl.num_programs(1) - 1)
    def _():
        s_out[0] = s_scr[...]


def _ret_tables(chunk_len):
    lg = jnp.log1p(-jnp.exp2(-5.0 - jnp.arange(RET_HEADS, dtype=F32)))
    i = jnp.arange(RET_CHUNK, dtype=F32)
    live = i < chunk_len
    rel = i[:, None] - i[None, :]
    intra = jnp.where((rel >= 0) & live[:, None] & live[None, :],
                      jnp.exp(lg[:, None, None] * jnp.maximum(rel, 0.0)), 0.0)
    q_dec = jnp.where(live[None, :], jnp.exp(lg[:, None] * (i[None, :] + 1.0)), 0.0)
    k_dec = jnp.where(live[None, :], jnp.exp(lg[:, None] * (chunk_len - 1.0 - i[None, :])), 0.0)
    c_dec = jnp.exp(lg * chunk_len)
    full = lambda t: jnp.broadcast_to(t[:, :, None], (RET_HEADS, RET_CHUNK, LANES))
    return intra, full(q_dec), full(k_dec), jnp.broadcast_to(c_dec[:, None, None], (RET_HEADS, RET_CHUNK, LANES))


def _retention(rq, rk, rv, rg, s0, chunk_len):
    b, t, _ = rq.shape
    if t % RET_CHUNK == 0:
        rows = RET_CHUNK
        n_chunks = min(8, t // rows)
    else:
        rows, n_chunks = t, 1
    tc = rows * n_chunks
    seq = pl.BlockSpec((1, tc, RET_W), lambda bi, s: (bi, s, 0))
    st = pl.BlockSpec((1, RET_HEADS, HEAD_DIM_AB, HEAD_DIM_AB), lambda bi, s: (bi, 0, 0, 0))
    tab = pl.BlockSpec((RET_HEADS, RET_CHUNK, LANES), lambda bi, s: (0, 0, 0))
    return pl.pallas_call(
        functools.partial(_ret_kernel, rows=rows, n_chunks=n_chunks),
        grid=(b, t // tc),
        in_specs=[seq] * 4 + [st] + [tab] * 4,
        out_specs=[seq, st],
        out_shape=[jax.ShapeDtypeStruct((b, t, RET_W), F32), jax.ShapeDtypeStruct(s0.shape, F32)],
        scratch_shapes=[pltpu.VMEM((RET_HEADS, HEAD_DIM_AB, HEAD_DIM_AB), F32)],
        compiler_params=_cparams(("parallel", "arbitrary")),
        name="retention",
    )(rq, rk, rv, rg, s0, *_ret_tables(chunk_len))


SB_SUB = 256
SB_ROWS = 128
LOG2E = 1.4426950408889634


def _sb_tile(q_ref, k_ref, v_ref, tri_ref, zbias, u_scr, sp_scr, w_scr, carry_scr, acc_scr, units, diagonal):
    d = HEAD_DIM_AB
    n_chunks = SB_SUB // SB_ROWS
    keys_of = lambda c: pl.ds(c * SB_SUB, SB_SUB)
    rows_of = lambda g: pl.ds(g * SB_SUB, SB_SUB)
    chunk_of = lambda g, r: pl.ds(g * SB_SUB + r * SB_ROWS, SB_ROWS)
    in_slot = lambda r: pl.ds(r * SB_ROWS, SB_ROWS)
    key = lax.broadcasted_iota(jnp.int32, (SB_ROWS, SB_SUB), 1)
    row = lax.broadcasted_iota(jnp.int32, (SB_ROWS, SB_SUB), 0)

    def mask_of(unit, r):
        c, g = unit
        return key < row + r * SB_ROWS if diagonal and g == c else None

    def scores(unit):
        c, g = unit
        return lax.dot_general(q_ref[0, rows_of(g), :], k_ref[0, keys_of(c), :], _NT, preferred_element_type=F32)

    def keep_logs(n, s):
        c, g = units[n]
        slot = n % 2
        for r in range(n_chunks):
            zs = s[r * SB_ROWS:(r + 1) * SB_ROWS, :] * (d ** -0.5 * LOG2E) + zbias
            pos, neg = jnp.maximum(zs, 0.0), jnp.minimum(zs, 0.0)
            l2 = jnp.log2(1.0 + jnp.exp2(neg - pos))
            sp = pos + l2
            mask = mask_of(units[n], r)
            if mask is not None:
                sp = jnp.where(mask, sp, 0.0)
            sp_scr[slot, in_slot(r), :] = _bf(sp)
            carry = carry_scr[chunk_of(g, r), :]
            u_scr[slot, in_slot(r), :] = neg - l2 - jnp.tile(carry, (1, SB_SUB // LANES))
            carry_scr[chunk_of(g, r), :] = carry + jnp.sum(sp, axis=1, keepdims=True)
        return _dot(sp_scr[slot], tri_ref[...])

    def weights(n, sums):
        c, g = units[n]
        slot = n % 2
        for r in range(n_chunks):
            w = jnp.exp2(u_scr[slot, in_slot(r), :] - sums[r * SB_ROWS:(r + 1) * SB_ROWS, :])
            mask = mask_of(units[n], r)
            if mask is not None:
                w = jnp.where(mask, w, 0.0)
            w_scr[slot, in_slot(r), :] = _bf(w)
        acc_scr[rows_of(g), :] += _dot(w_scr[slot], v_ref[0, keys_of(c), :])

    s, sums = {}, {}
    for n in range(len(units) + 2):
        if n < len(units):
            s[n] = scores(units[n])
        if 1 <= n <= len(units):
            sums[n - 1] = keep_logs(n - 1, s.pop(n - 1))
        if n >= 2:
            weights(n - 2, sums.pop(n - 2))


def _sb_prompt_kernel(qi_tab, kj_tab, bias_ref, q_ref, k_ref, v_ref, tri_ref, o_ref,
                      acc_scr, carry_scr, u_scr, sp_scr, w_scr, *, tq):
    h = pl.program_id(0)
    p = pl.program_id(1)
    qi = qi_tab[p]
    kj = kj_tab[p]
    zbias = bias_ref[h] * LOG2E
    n_sub = tq // SB_SUB
    tile = functools.partial(_sb_tile, q_ref, k_ref, v_ref, tri_ref, zbias, u_scr, sp_scr, w_scr, carry_scr, acc_scr)

    @pl.when(kj == qi)
    def _():
        acc_scr[...] = jnp.zeros_like(acc_scr)
        carry_scr[...] = jnp.zeros_like(carry_scr)
        tile([(c, g) for c in reversed(range(n_sub)) for g in range(c, n_sub)], True)

    @pl.when(kj < qi)
    def _():
        tile([(c, g) for c in reversed(range(n_sub)) for g in range(n_sub)], False)

    @pl.when(kj == 0)
    def _():
        o_ref[...] = acc_scr[...].astype(o_ref.dtype)


def _tri_ones(n):
    i = np.arange(n)
    return jnp.asarray((i[:, None] > i[None, :]).astype(np.float32), BF16)


def _sb_prompt(q_hm, k_hm, v_hm, bias):
    nh, t, d = q_hm.shape
    tq = min(1024, t)
    nq = t // tq
    qi_tab = np.concatenate([np.full(i + 1, i) for i in range(nq)]).astype(np.int32)
    kj_tab = np.concatenate([np.arange(i, -1, -1) for i in range(nq)]).astype(np.int32)
    grid_spec = pltpu.PrefetchScalarGridSpec(
        num_scalar_prefetch=2,
        grid=(nh, len(qi_tab)),
        in_specs=[
            pl.BlockSpec(memory_space=pltpu.SMEM),
            pl.BlockSpec((1, tq, d), lambda h, p, qt, kt: (h, qt[p], 0)),
            pl.BlockSpec((1, tq, d), lambda h, p, qt, kt: (h, kt[p], 0)),
            pl.BlockSpec((1, tq, d), lambda h, p, qt, kt: (h, kt[p], 0)),
            pl.BlockSpec((SB_SUB, SB_SUB), lambda h, p, qt, kt: (0, 0)),
        ],
        out_specs=pl.BlockSpec((tq, d), lambda h, p, qt, kt: (qt[p], h)),
        scratch_shapes=[pltpu.VMEM((tq, d), F32), pltpu.VMEM((tq, LANES), F32), pltpu.VMEM((2, SB_SUB, SB_SUB), F32),
                        pltpu.VMEM((2, SB_SUB, SB_SUB), BF16), pltpu.VMEM((2, SB_SUB, SB_SUB), BF16)],
    )
    return pl.pallas_call(
        functools.partial(_sb_prompt_kernel, tq=tq),
        grid_spec=grid_spec,
        out_shape=jax.ShapeDtypeStruct((t, nh * d), BF16),
        compiler_params=_cparams(("parallel", "arbitrary")),
        name="sb_prompt",
    )(jnp.asarray(qi_tab), jnp.asarray(kj_tab), bias, q_hm, k_hm, v_hm, _tri_ones(SB_SUB))


SB_PAGES_PER_STEP = 8


def _sb_decode_kernel(pt_ref, qbd_ref, bias_ref, knew_ref, vnew_ref, trit_ref, *rest, n_pages_step, page):
    k_pages = rest[:n_pages_step]
    v_pages = rest[n_pages_step:2 * n_pages_step]
    o_ref, acc_scr, carry_scr = rest[2 * n_pages_step:]
    g = pl.program_id(1)
    qbd = qbd_ref[0]
    zbias = bias_ref[...] * LOG2E
    trit = trit_ref[...]
    d = HEAD_DIM_AB
    heads = range(SB_HEADS)

    def block(k_heads, v_heads, carry, accs, mask):
        s = functools.reduce(lambda a, c: a + c,
                             [_dot(_bf(k_heads[h]), qbd[h * d:(h + 1) * d, :]) for h in heads])
        zs = s * (d ** -0.5 * LOG2E) + zbias
        l2 = jnp.log2(1.0 + jnp.exp2(-jnp.abs(zs)))
        sp = jnp.maximum(zs, 0.0) + l2
        if mask is not None:
            sp = jnp.where(mask, sp, 0.0)
        later = _dot(trit[:page, :page], _bf(sp))
        wt = jnp.exp2(jnp.minimum(zs, 0.0) - l2 - later - carry)
        if mask is not None:
            wt = jnp.where(mask, wt, 0.0)
        w = jnp.transpose(wt)
        accs = [accs[h] + _dot(_bf(w[h * DEC_ROWS:(h + 1) * DEC_ROWS, :]), _bf(v_heads[h])) for h in heads]
        return carry + jnp.sum(sp, axis=0, keepdims=True), accs

    def store(carry, accs):
        carry_scr[...] = jnp.broadcast_to(carry, carry_scr.shape)
        for h in heads:
            acc_scr[h * DEC_ROWS:(h + 1) * DEC_ROWS, :] = accs[h]

    @pl.when(g == 0)
    def _():
        pad = jnp.zeros((page - DEC_ROWS, d), F32)
        k_heads = [jnp.concatenate([knew_ref[0, :, h * d:(h + 1) * d], pad], axis=0) for h in heads]
        v_heads = [jnp.concatenate([vnew_ref[0, :, h * d:(h + 1) * d], pad], axis=0) for h in heads]
        key = lax.broadcasted_iota(jnp.int32, (page, LANES), 0)
        qry = lax.broadcasted_iota(jnp.int32, (page, LANES), 1) & (DEC_ROWS - 1)
        store(*block(k_heads, v_heads, jnp.zeros((1, LANES), F32),
                     [jnp.zeros((DEC_ROWS, d), F32) for _ in heads], key < qry))

    unit_pages = trit_ref.shape[0] // page
    n_units = n_pages_step // unit_pages
    pages_of = lambda n: range(n_pages_step - (n + 1) * unit_pages, n_pages_step - n * unit_pages)
    head_rows = lambda ref, h: ref[0, pl.ds(h, page, stride=SB_HEADS), :]

    def scores(n):
        k_unit = jnp.concatenate([jnp.concatenate([head_rows(k_pages[u], h) for h in heads], axis=1)
                                  for u in pages_of(n)], axis=0)
        return _dot(_bf(k_unit), qbd) * (d ** -0.5 * LOG2E) + zbias

    def keep_logs(zs):
        pos, neg = jnp.maximum(zs, 0.0), jnp.minimum(zs, 0.0)
        l2 = jnp.log2(1.0 + jnp.exp2(neg - pos))
        sp = pos + l2
        return neg - l2, _dot(trit, _bf(sp)), jnp.sum(sp, axis=0, keepdims=True)

    def weights(n, logs, carry, accs):
        log_beta, later, total = logs
        wt = jnp.exp2(log_beta - later - carry)
        w = jnp.concatenate([jnp.transpose(wt[j * LANES:(j + 1) * LANES, :])
                             for j in range(wt.shape[0] // LANES)], axis=1)
        for h in heads:
            v_unit = jnp.concatenate([head_rows(v_pages[u], h) for u in pages_of(n)], axis=0)
            accs[h] = accs[h] + _dot(_bf(w[h * DEC_ROWS:(h + 1) * DEC_ROWS, :]), _bf(v_unit))
        return carry + total, accs

    carry = carry_scr[0:1, :]
    accs = [acc_scr[h * DEC_ROWS:(h + 1) * DEC_ROWS, :] for h in heads]
    zs, logs = {}, {}
    for m in range(n_units + 2):
        if m < n_units:
            zs[m] = scores(m)
        if 1 <= m <= n_units:
            logs[m - 1] = keep_logs(zs.pop(m - 1))
        if m >= 2:
            carry, accs = weights(m - 2, logs.pop(m - 2), carry, accs)
    store(carry, accs)

    @pl.when(g == pl.num_programs(1) - 1)
    def _():
        for h in heads:
            o_ref[0, :, h * d:(h + 1) * d] = acc_scr[h * DEC_ROWS:(h + 1) * DEC_ROWS, :]


def _sb_decode(sq, sk_new, sv_new, cache_k, cache_v, page_table, bias):
    b = sq.shape[0]
    n_pages = page_table.shape[1]
    page = cache_k.shape[1]
    g_pages = min(SB_PAGES_PER_STEP, n_pages)
    n_steps = n_pages // g_pages
    d = HEAD_DIM_AB
    q4 = sq.reshape(b, DEC_ROWS, SB_HEADS, d)
    eye = jnp.eye(SB_HEADS, dtype=F32)
    qbd = jnp.einsum('bihd,hg->bhdgi', q4, eye).reshape(b, SB_W, SB_HEADS * DEC_ROWS)
    qbd = _bf(jnp.pad(qbd, ((0, 0), (0, 0), (0, LANES - SB_HEADS * DEC_ROWS))))
    bias_lane = jnp.pad(jnp.repeat(bias.astype(F32), DEC_ROWS), (0, LANES - SB_HEADS * DEC_ROWS))[None, :]
    unit_keys = page * (2 if g_pages % 2 == 0 else 1)
    i = np.arange(unit_keys)
    trit = jnp.asarray((i[None, :] > i[:, None]).astype(np.float32), BF16)

    def page_spec(u):
        return pl.BlockSpec((1, page * SB_HEADS, d),
                            lambda bi, g, pt: (pt[bi * n_pages + (n_steps - 1 - g) * g_pages + u], 0, 0))

    per_seq = lambda r, w: pl.BlockSpec((1, r, w), lambda bi, g, pt: (bi, 0, 0))
    grid_spec = pltpu.PrefetchScalarGridSpec(
        num_scalar_prefetch=1,
        grid=(b, n_steps),
        in_specs=[per_seq(SB_W, LANES), pl.BlockSpec((1, LANES), lambda bi, g, pt: (0, 0)),
                  per_seq(DEC_ROWS, SB_W), per_seq(DEC_ROWS, SB_W),
                  pl.BlockSpec((unit_keys, unit_keys), lambda bi, g, pt: (0, 0))]
                 + [page_spec(u) for u in range(g_pages)] * 2,
        out_specs=per_seq(DEC_ROWS, SB_W),
        scratch_shapes=[pltpu.VMEM((SB_HEADS * DEC_ROWS, d), F32), pltpu.VMEM((DEC_ROWS, LANES), F32)],
    )
    return pl.pallas_call(
        functools.partial(_sb_decode_kernel, n_pages_step=g_pages, page=page),
        grid_spec=grid_spec,
        out_shape=jax.ShapeDtypeStruct((b, DEC_ROWS, SB_W), F32),
        compiler_params=_cparams(("parallel", "arbitrary")),
        name="sb_decode",
    )(page_table.reshape(-1), qbd, bias_lane, sk_new, sv_new, trit,
      *([cache_k.reshape(-1, page * SB_HEADS, d)] * g_pages), *([cache_v.reshape(-1, page * SB_HEADS, d)] * g_pages))


def _inproj_c_kernel(x_ref, g_ref, w_ref, q_ref, k_ref, v_ref, kf_ref, vf_ref):
    a = _rms(x_ref[...], g_ref[...])
    p = _dot(_bf(a), w_ref[...])
    q_ref[...] = _bf(p[:, :DIL_W])
    k_ref[...] = _bf(p[:, DIL_W:2 * DIL_W])
    v_ref[...] = _bf(p[:, 2 * DIL_W:])
    kf_ref[...] = p[:, DIL_W:2 * DIL_W]
    vf_ref[...] = p[:, 2 * DIL_W:]


def _inproj_c(x, gain, w_bf):
    m, dm = x.shape
    tm = min(m, 256)
    row = pl.BlockSpec((tm, DIL_W), lambda i: (i, 0))
    return pl.pallas_call(
        _inproj_c_kernel,
        grid=(m // tm,),
        in_specs=[pl.BlockSpec((tm, dm), lambda i: (i, 0)), pl.BlockSpec((1, dm), lambda i: (0, 0)),
                  pl.BlockSpec((dm, 3 * DIL_W), lambda i: (0, 0))],
        out_specs=[row] * 5,
        out_shape=[jax.ShapeDtypeStruct((m, DIL_W), BF16)] * 3 + [jax.ShapeDtypeStruct((m, DIL_W), F32)] * 2,
        compiler_params=_cparams(("parallel",)),
        name="inproj_c",
    )(x, gain, w_bf)


def _dil_branch_kernel(q_ref, kp_ref, kc_ref, vp_ref, vc_ref, o_ref, lse_ref):
    i = pl.program_id(1)
    blk = DIL_SPAN
    rows = lax.broadcasted_iota(jnp.int32, (blk, 2 * blk), 0)
    cols = lax.broadcasted_iota(jnp.int32, (blk, 2 * blk), 1)
    dist = blk + rows - cols
    valid = (dist >= 0) & (dist <= DIL_SPAN) & ((cols >= blk) | (i > 0))
    lane = lax.broadcasted_iota(jnp.int32, (blk, LANES), 1)
    first = lane < DIL_HEAD_DIM
    scale = DIL_HEAD_DIM ** -0.5
    for hp in range(DIL_HEADS // 2):
        cs = slice(hp * LANES, (hp + 1) * LANES)
        q = q_ref[:, cs]
        k2 = jnp.concatenate([kp_ref[:, cs], kc_ref[:, cs]], axis=0)
        v2 = jnp.concatenate([vp_ref[:, cs], vc_ref[:, cs]], axis=0)
        zero = jnp.zeros_like(q)
        outs, lses = [], []
        for qh in (jnp.where(first, q, zero), jnp.where(first, zero, q)):
            z = lax.dot_general(qh, k2, _NT, preferred_element_type=F32) * scale
            z = jnp.where(valid, z, -jnp.inf)
            m = jnp.max(z, axis=-1, keepdims=True)
            e = jnp.exp(z - m)
            den = jnp.sum(e, axis=-1, keepdims=True)
            outs.append(_dot(_bf(e), v2) / den)
            lses.append(m + jnp.log(den))
        o_ref[:, cs] = jnp.where(first, outs[0], outs[1])
        lse_ref[:, cs] = jnp.where(first, lses[0], lses[1])


def _dil_branch(q, k, v, dil):
    t = q.shape[0]
    ts = t // dil
    blk = DIL_SPAN
    view = lambda a: a.reshape(ts, dil * DIL_W)
    cur = pl.BlockSpec((blk, DIL_W), lambda r, i: (i, r))
    prev = pl.BlockSpec((blk, DIL_W), lambda r, i: (jnp.maximum(i - 1, 0), r))
    o, lse = pl.pallas_call(
        _dil_branch_kernel,
        grid=(dil, ts // blk),
        in_specs=[cur, prev, cur, prev, cur],
        out_specs=[cur, cur],
        out_shape=[jax.ShapeDtypeStruct((ts, dil * DIL_W), F32)] * 2,
        compiler_params=_cparams(("parallel", "arbitrary")),
        name=f"dilated_d{dil}",
    )(view(q), view(k), view(k), view(v), view(v))
    return o.reshape(t, DIL_W), lse.reshape(t, DIL_W)


DIL_DEC_CHUNK = 512


def _dil_multiplicity(dist):
    cnt = jnp.zeros(dist.shape, F32)
    for window, dil in DIL_BRANCHES:
        hit = (dist >= 0) & (dist <= window) & ((dist & (dil - 1)) == 0)
        cnt = cnt + jnp.where(hit, 1.0, 0.0)
    return cnt


def _dil_decode_kernel(qbd_ref, k_ref, v_ref, kn_ref, vn_ref, o_ref, z_scr, *, n_past, n_new):
    qbd = qbd_ref[0]
    scale = DIL_HEAD_DIM ** -0.5
    ck = DIL_DEC_CHUNK
    n_chunks = n_past // ck
    n_cols = n_new * DIL_HEADS

    def q_of(shape):
        return lax.broadcasted_iota(jnp.int32, shape, 1) >> 4

    def past_count(c):
        key = c * ck + lax.broadcasted_iota(jnp.int32, (ck, LANES), 0)
        return _dil_multiplicity(n_past + q_of((ck, LANES)) - key)

    key_new = lax.broadcasted_iota(jnp.int32, (DEC_ROWS, LANES), 0)
    cnt_new = jnp.where(key_new < n_new, _dil_multiplicity(q_of((DEC_ROWS, LANES)) - key_new), 0.0)

    z_new = _dot(_bf(kn_ref[0]), qbd) * scale
    m = jnp.max(jnp.where(cnt_new > 0, z_new, -jnp.inf), axis=0, keepdims=True)
    for c in range(n_chunks):
        z = _dot(_bf(k_ref[0, c * ck:(c + 1) * ck, :]), qbd) * scale
        z_scr[c * ck:(c + 1) * ck, :] = z
        m = jnp.maximum(m, jnp.max(jnp.where(past_count(c) > 0, z, -jnp.inf), axis=0, keepdims=True))

    p_new = jnp.where(cnt_new > 0, cnt_new * jnp.exp(z_new - m), 0.0)
    den = jnp.sum(p_new, axis=0, keepdims=True)
    ps = []
    for c in range(n_chunks):
        cnt = past_count(c)
        p = jnp.where(cnt > 0, cnt * jnp.exp(z_scr[c * ck:(c + 1) * ck, :] - m), 0.0)
        den = den + jnp.sum(p, axis=0, keepdims=True)
        ps.append(p)
    inv = 1.0 / den
    pad_p = jnp.zeros((LANES - DEC_ROWS, LANES), F32)
    pad_v = jnp.zeros((LANES - DEC_ROWS, DIL_W), F32)
    pn_t = jnp.transpose(jnp.concatenate([p_new * inv, pad_p], axis=0))[:n_cols, :]
    acc = _dot(_bf(pn_t), _bf(jnp.concatenate([vn_ref[0], pad_v], axis=0)))
    for c in range(n_chunks):
        pt = jnp.concatenate([jnp.transpose((ps[c] * inv)[s * LANES:(s + 1) * LANES, :])[:n_cols, :]
                              for s in range(ck // LANES)], axis=1)
        acc = acc + _dot(_bf(pt), _bf(v_ref[0, c * ck:(c + 1) * ck, :]))
    row_head = lax.broadcasted_iota(jnp.int32, (DIL_HEADS, DIL_W), 0)
    lane_head = lax.broadcasted_iota(jnp.int32, (DIL_HEADS, DIL_W), 1) // DIL_HEAD_DIM
    out_row = lax.broadcasted_iota(jnp.int32, (DEC_ROWS, DIL_W), 0)
    out = jnp.zeros((DEC_ROWS, DIL_W), F32)
    for i in range(n_new):
        picked = jnp.sum(jnp.where(row_head == lane_head, acc[i * DIL_HEADS:(i + 1) * DIL_HEADS, :], 0.0),
                         axis=0, keepdims=True)
        out = out + jnp.where(out_row == i, picked, 0.0)
    o_ref[0] = out


def _dil_decode(q, k_new, v_new, cache_k, cache_v, n_new):
    b, n_past, _ = cache_k.shape
    q4 = q[:, :n_new].reshape(b, n_new, DIL_HEADS, DIL_HEAD_DIM)
    eye = jnp.eye(DIL_HEADS, dtype=F32)
    qbd = jnp.einsum('bihd,hg->bhdig', q4, eye).reshape(b, DIL_W, n_new * DIL_HEADS)
    qbd = _bf(jnp.pad(qbd, ((0, 0), (0, 0), (0, LANES - n_new * DIL_HEADS))))
    per_seq = lambda r, w: pl.BlockSpec((1, r, w), lambda bi: (bi, 0, 0))
    return pl.pallas_call(
        functools.partial(_dil_decode_kernel, n_past=n_past, n_new=n_new),
        grid=(b,),
        in_specs=[per_seq(DIL_W, LANES), per_seq(n_past, DIL_W), per_seq(n_past, DIL_W),
                  per_seq(DEC_ROWS, DIL_W), per_seq(DEC_ROWS, DIL_W)],
        out_specs=per_seq(DEC_ROWS, DIL_W),
        out_shape=jax.ShapeDtypeStruct((b, DEC_ROWS, DIL_W), F32),
        scratch_shapes=[pltpu.VMEM((n_past, LANES), F32)],
        compiler_params=_cparams(("parallel",)),
        name="dilated_decode",
    )(qbd, cache_k, cache_v, k_new, v_new)


def _mix_concat(*refs):
    return jnp.concatenate([_bf(r[...]) for r in refs], axis=-1)


def _mix_merge_branches(*refs):
    n = len(refs) // 2
    lses = [r[...] for r in refs[n:]]
    top = functools.reduce(jnp.maximum, lses)
    ws = [jnp.exp(l - top) for l in lses]
    num = functools.reduce(lambda a, c: a + c, [w * r[...] for w, r in zip(ws, refs[:n])])
    return _bf(num / functools.reduce(lambda a, c: a + c, ws))


def _tail_kernel(*refs, n_mix, mix_fn):
    mix_refs = refs[:n_mix]
    (h_ref, wout_ref, gpost_ref, gpre_ref, w1_ref, w2_ref, gffn_ref,
     out_ref, h1_scr, a_scr, acc_scr) = refs[n_mix:]
    f = pl.program_id(1)

    @pl.when(f == 0)
    def _():
        m = _dot(mix_fn(*mix_refs), wout_ref[...])
        h1 = h_ref[...] + _rms(m, gpost_ref[...])
        h1_scr[...] = h1
        a_scr[...] = _bf(_rms(h1, gpre_ref[...]))
        acc_scr[...] = jnp.zeros_like(acc_scr)

    hid = jnp.square(jnp.maximum(_dot(a_scr[...], w1_ref[...]), 0.0))
    acc_scr[...] += _dot(_bf(hid), w2_ref[...])

    @pl.when(f == pl.num_programs(1) - 1)
    def _():
        out_ref[...] = h1_scr[...] + _rms(acc_scr[...], gffn_ref[...])


def _layer_tail(mix, mix_fn, h, w_out, g_post, g_ffn_pre, w1, w2, g_ffn_post, tm):
    m, dm = h.shape
    dff = w1.shape[1]
    tm = min(m, tm)
    tf = min(dff, 1024)
    row = lambda w: pl.BlockSpec((tm, w), lambda i, f: (i, 0))
    gain = pl.BlockSpec((1, dm), lambda i, f: (0, 0))
    return pl.pallas_call(
        functools.partial(_tail_kernel, n_mix=len(mix), mix_fn=mix_fn),
        grid=(m // tm, dff // tf),
        in_specs=[row(a.shape[1]) for a in mix]
                 + [row(dm), pl.BlockSpec(w_out.shape, lambda i, f: (0, 0)), gain, gain,
                    pl.BlockSpec((dm, tf), lambda i, f: (0, f)), pl.BlockSpec((tf, dm), lambda i, f: (f, 0)), gain],
        out_specs=row(dm),
        out_shape=jax.ShapeDtypeStruct((m, dm), F32),
        scratch_shapes=[pltpu.VMEM((tm, dm), F32), pltpu.VMEM((tm, dm), BF16), pltpu.VMEM((tm, dm), F32)],
        compiler_params=_cparams(("parallel", "arbitrary")),
        name="layer_tail",
    )(*mix, h, w_out, g_post, g_ffn_pre, w1, w2, g_ffn_post)


def _rope_tables(pos):
    half = HEAD_DIM_AB // 2
    inv_freq = jnp.power(ROPE_BASE, -jnp.linspace(0.0, 1.0, half, dtype=F32))
    ang = pos.astype(F32)[:, None] * inv_freq[None, :]
    cos, sin = jnp.cos(ang), jnp.sin(ang)
    return jnp.concatenate([cos, cos], axis=-1), jnp.concatenate([-sin, sin], axis=-1)


def _pad_rows(a, rows):
    return jnp.pad(a, ((0, 0), (0, rows - a.shape[1]), (0, 0)))


def kernel(x_prompt, x_sample, cache_sb_k, cache_sb_v, state_ret, cache_swa_k, cache_swa_v, page_table,
           w_in_ab, w_out_ab, sb_bias, w_in_c, w_out_c, w_ff1, w_ff2, g_mix_pre, g_mix_post, g_ffn_pre, g_ffn_post):
    bp, t, dm = x_prompt.shape
    bs, ts, _ = x_sample.shape
    assert bp == 1 and ts <= DEC_ROWS and t % RET_CHUNK == 0
    n_pages = page_table.shape[1]
    page = cache_sb_k.shape[1]
    past_len = n_pages * page
    d = HEAD_DIM_AB
    gain = lambda g, layer: g[layer][None, :].astype(F32)
    w_in_ab_bf, w_out_ab_bf, w_in_c_bf, w_out_c_bf = _bf(w_in_ab), _bf(w_out_ab), _bf(w_in_c), _bf(w_out_c)
    w_ff1_bf, w_ff2_bf = _bf(w_ff1), _bf(w_ff2)

    h_p = x_prompt.reshape(t, dm)
    h_s = x_sample.reshape(bs * ts, dm)

    cos_p, sin_p = _rope_tables(jnp.arange(t, dtype=jnp.int32))
    rq, rk, rv, rg, q_hm, k_hm, v_hm, sb_k_p, sb_v_p = _inproj_ab(h_p, gain(g_mix_pre, 0), w_in_ab_bf, cos_p, sin_p)
    seq = lambda a: a.reshape(1, t, RET_W)
    ret_zero = jnp.zeros((1, RET_HEADS, d, d), F32)
    y_ret_p, ret_p = _retention(seq(rq), seq(rk), seq(rv), seq(rg), ret_zero, float(RET_CHUNK))
    o_sb_p = _sb_prompt(q_hm, k_hm, v_hm, sb_bias.astype(F32))
    h_p = _layer_tail([y_ret_p.reshape(t, RET_W), o_sb_p], _mix_concat, h_p, w_out_ab_bf, gain(g_mix_post, 0),
                      gain(g_ffn_pre, 0), w_ff1_bf[0], w_ff2_bf[0], gain(g_ffn_post, 0), tm=512)

    cos_s, sin_s = _rope_tables(past_len + jnp.arange(ts, dtype=jnp.int32))
    tile_s = lambda a: jnp.tile(a, (bs, 1))
    rq, rk, rv, rg, q_hm, _, _, sb_k_s, sb_v_s = _inproj_ab(h_s, gain(g_mix_pre, 0), w_in_ab_bf,
                                                            tile_s(cos_s), tile_s(sin_s))
    dec = lambda a: _pad_rows(a.reshape(bs, ts, a.shape[-1]), DEC_ROWS)
    y_ret_s, ret_s = _retention(dec(rq), dec(rk), dec(rv), dec(rg), state_ret.astype(F32), float(ts))
    sq_s = jnp.transpose(q_hm.astype(F32), (1, 0, 2)).reshape(bs * ts, SB_W)
    o_sb_s = _sb_decode(dec(sq_s), dec(sb_k_s), dec(sb_v_s),
                        cache_sb_k, cache_sb_v, page_table, sb_bias.astype(F32))
    undec = lambda a: a[:, :ts].reshape(bs * ts, a.shape[-1])
    h_s = _layer_tail([undec(y_ret_s), undec(o_sb_s)], _mix_concat, h_s, w_out_ab_bf, gain(g_mix_post, 0),
                      gain(g_ffn_pre, 0), w_ff1_bf[0], w_ff2_bf[0], gain(g_ffn_post, 0), tm=512)

    q, k, v, k_f32, v_f32 = _inproj_c(h_p, gain(g_mix_pre, 1), w_in_c_bf)
    branches = [_dil_branch(q, k, v, dil) for _, dil in DIL_BRANCHES]
    mix = [o for o, _ in branches] + [lse for _, lse in branches]
    h_p = _layer_tail(mix, _mix_merge_branches, h_p, w_out_c_bf, gain(g_mix_post, 1),
                      gain(g_ffn_pre, 1), w_ff1_bf[1], w_ff2_bf[1], gain(g_ffn_post, 1), tm=256)
    keep_p = min(DIL_WINDOW_MAX, t)
    swa_k_p = k_f32[t - keep_p:].reshape(1, keep_p, DIL_HEADS, DIL_HEAD_DIM)
    swa_v_p = v_f32[t - keep_p:].reshape(1, keep_p, DIL_HEADS, DIL_HEAD_DIM)

    q, _, _, k_f32, v_f32 = _inproj_c(h_s, gain(g_mix_pre, 1), w_in_c_bf)
    n_past = cache_swa_k.shape[1]
    ck = cache_swa_k.reshape(bs, n_past, DIL_W)
    cv = cache_swa_v.reshape(bs, n_past, DIL_W)
    o_dil_s = _dil_decode(dec(q.astype(F32)), dec(k_f32), dec(v_f32), ck, cv, ts)
    h_s = _layer_tail([undec(o_dil_s)], _mix_concat, h_s, w_out_c_bf, gain(g_mix_post, 1),
                      gain(g_ffn_pre, 1), w_ff1_bf[1], w_ff2_bf[1], gain(g_ffn_post, 1), tm=512)
    keep_s = min(DIL_WINDOW_MAX, n_past + ts)
    swa_k_s = jnp.concatenate([ck, k_f32.reshape(bs, ts, DIL_W)], axis=1)[:, n_past + ts - keep_s:]
    swa_v_s = jnp.concatenate([cv, v_f32.reshape(bs, ts, DIL_W)], axis=1)[:, n_past + ts - keep_s:]

    heads_ab = lambda a, b: a.reshape(b, -1, SB_HEADS, d)
    return (h_p.reshape(1, t, dm), h_s.reshape(bs, ts, dm),
            heads_ab(sb_k_p, 1), heads_ab(sb_v_p, 1), ret_p,
            swa_k_p, swa_v_p,
            heads_ab(sb_k_s, bs), heads_ab(sb_v_s, bs), ret_s,
            swa_k_s.reshape(bs, keep_s, DIL_HEADS, DIL_HEAD_DIM),
            swa_v_s.reshape(bs, keep_s, DIL_HEADS, DIL_HEAD_DIM))
```

```python
import functools

import numpy as np
import jax
import jax.numpy as jnp
from jax import lax
from jax.experimental import pallas as pl
from jax.experimental.pallas import tpu as pltpu

F32 = jnp.float32
BF16 = jnp.bfloat16

NORM_EPS = 1e-6
ROPE_BASE = 10000.0
RET_HEADS = 4
SB_HEADS = 4
HEAD_DIM_AB = 128
RET_W = RET_HEADS * HEAD_DIM_AB
SB_W = SB_HEADS * HEAD_DIM_AB
RET_CHUNK = 128
DIL_HEADS = 16
DIL_HEAD_DIM = 64
DIL_W = DIL_HEADS * DIL_HEAD_DIM
DIL_BRANCHES = ((128, 1), (512, 4), (2048, 16))
DIL_WINDOW_MAX = 2048
DIL_SPAN = 128
LANES = 128
DEC_ROWS = 8
VMEM_LIMIT = 56 * 1024 * 1024

_NT = (((1,), (1,)), ((), ()))
_TN = (((0,), (0,)), ((), ()))


def _cparams(sem):
    return pltpu.CompilerParams(dimension_semantics=sem, vmem_limit_bytes=VMEM_LIMIT)


def _bf(x):
    return x.astype(BF16)


def _dot(a, b):
    return jnp.dot(a, b, preferred_element_type=F32)


def _rms(x, gain):
    return x * lax.rsqrt(jnp.mean(x * x, axis=-1, keepdims=True) + NORM_EPS) * gain


def _softplus(z):
    return jnp.maximum(z, 0.0) + jnp.log1p(jnp.exp(-jnp.abs(z)))


def _inproj_ab_kernel(x_ref, g_ref, w_ref, cos_ref, sin_ref,
                      rq_ref, rk_ref, rv_ref, rg_ref, q_hm, k_hm, v_hm, sbk_ref, sbv_ref):
    a = _rms(x_ref[...], g_ref[...])
    p = _dot(_bf(a), w_ref[...])
    cosf = cos_ref[...]
    sinf = sin_ref[...]
    d = HEAD_DIM_AB
    for h in range(RET_HEADS):
        qh = p[:, h * d:(h + 1) * d]
        kh = p[:, RET_W + h * d:RET_W + (h + 1) * d]
        rq_ref[:, h * d:(h + 1) * d] = qh * cosf + pltpu.roll(qh, d // 2, 1) * sinf
        rk_ref[:, h * d:(h + 1) * d] = (kh * cosf + pltpu.roll(kh, d // 2, 1) * sinf) * (d ** -0.5)
    rv_ref[...] = p[:, 2 * RET_W:3 * RET_W]
    rg_ref[...] = p[:, 3 * RET_W:4 * RET_W]
    base = 4 * RET_W
    for h in range(SB_HEADS):
        q_hm[h] = _bf(p[:, base + h * d:base + (h + 1) * d])
        k_hm[h] = _bf(p[:, base + SB_W + h * d:base + SB_W + (h + 1) * d])
        v_hm[h] = _bf(p[:, base + 2 * SB_W + h * d:base + 2 * SB_W + (h + 1) * d])
    tm = p.shape[0]
    for h in range(SB_HEADS):
        sbk_ref[pl.ds(h, tm, stride=SB_HEADS), :] = p[:, base + SB_W + h * d:base + SB_W + (h + 1) * d]
        sbv_ref[pl.ds(h, tm, stride=SB_HEADS), :] = p[:, base + 2 * SB_W + h * d:base + 2 * SB_W + (h + 1) * d]


def _inproj_ab(x, gain, w_bf, cosf, sinf):
    m, dm = x.shape
    n = w_bf.shape[1]
    tm = min(m, 256)
    row = lambda w: pl.BlockSpec((tm, w), lambda i: (i, 0))
    hm = pl.BlockSpec((SB_HEADS, tm, HEAD_DIM_AB), lambda i: (0, i, 0))
    f32_rows = jax.ShapeDtypeStruct((m, RET_W), F32)
    hm_shape = jax.ShapeDtypeStruct((SB_HEADS, m, HEAD_DIM_AB), BF16)
    return pl.pallas_call(
        _inproj_ab_kernel,
        grid=(m // tm,),
        in_specs=[row(dm), pl.BlockSpec((1, dm), lambda i: (0, 0)),
                  pl.BlockSpec((dm, n), lambda i: (0, 0)), row(HEAD_DIM_AB), row(HEAD_DIM_AB)],
        out_specs=[row(RET_W)] * 4 + [hm] * 3 + [pl.BlockSpec((tm * SB_HEADS, HEAD_DIM_AB), lambda i: (i, 0))] * 2,
        out_shape=[f32_rows] * 4 + [hm_shape] * 3 + [jax.ShapeDtypeStruct((m * SB_HEADS, HEAD_DIM_AB), F32)] * 2,
        compiler_params=_cparams(("parallel",)),
        name="inproj_ab",
    )(x, gain, w_bf, cosf, sinf)


def _ret_kernel(rq, rk, rv, rg, s0, intra, qdec, kdec, cdec, y_ref, s_out, s_scr, *, rows, n_chunks):
    step = pl.program_id(1)
    d = HEAD_DIM_AB

    @pl.when(step == 0)
    def _():
        s_scr[...] = s0[0]

    for c in range(n_chunks):
        rs = slice(c * rows, (c + 1) * rows)
        for h in range(RET_HEADS):
            cs = slice(h * d, (h + 1) * d)
            q, k, v, g = rq[0, rs, cs], rk[0, rs, cs], rv[0, rs, cs], rg[0, rs, cs]
            if rows < RET_CHUNK:
                pad = jnp.zeros((RET_CHUNK - rows, d), F32)
                k = jnp.concatenate([k, pad], axis=0)
                v = jnp.concatenate([v, pad], axis=0)
            s_prev = s_scr[h]
            att = lax.dot_general(_bf(q), _bf(k), _NT, preferred_element_type=F32) * intra[h, :rows, :]
            o = _dot(_bf(att), _bf(v)) + _dot(_bf(q * qdec[h, :rows, :]), _bf(s_prev))
            kd = k * kdec[h]
            s_scr[h] = s_prev * cdec[h] + lax.dot_general(_bf(kd), _bf(v), _TN, preferred_element_type=F32)
            xc = o - jnp.mean(o, axis=-1, keepdims=True)
            var = jnp.mean(xc * xc, axis=-1, keepdims=True)
            y_ref[0, rs, cs] = xc * lax.rsqrt(var + NORM_EPS) * (g * jax.nn.sigmoid(g))

    @pl.when(step == pl.num_programs(1) - 1)
    def _():
        s_out[0] = s_scr[...]


def _ret_tables(chunk_len):
    lg = jnp.log1p(-jnp.exp2(-5.0 - jnp.arange(RET_HEADS, dtype=F32)))
    i = jnp.arange(RET_CHUNK, dtype=F32)
    live = i < chunk_len
    rel = i[:, None] - i[None, :]
    intra = jnp.where((rel >= 0) & live[:, None] & live[None, :],
                      jnp.exp(lg[:, None, None] * jnp.maximum(rel, 0.0)), 0.0)
    q_dec = jnp.where(live[None, :], jnp.exp(lg[:, None] * (i[None, :] + 1.0)), 0.0)
    k_dec = jnp.where(live[None, :], jnp.exp(lg[:, None] * (chunk_len - 1.0 - i[None, :])), 0.0)
    c_dec = jnp.exp(lg * chunk_len)
    full = lambda t: jnp.broadcast_to(t[:, :, None], (RET_HEADS, RET_CHUNK, LANES))
    return intra, full(q_dec), full(k_dec), jnp.broadcast_to(c_dec[:, None, None], (RET_HEADS, RET_CHUNK, LANES))


def _retention(rq, rk, rv, rg, s0, chunk_len):
    b, t, _ = rq.shape
    if t % RET_CHUNK == 0:
        rows = RET_CHUNK
        n_chunks = min(8, t // rows)
    else:
        rows, n_chunks = t, 1
    tc = rows * n_chunks
    seq = pl.BlockSpec((1, tc, RET_W), lambda bi, s: (bi, s, 0))
    st = pl.BlockSpec((1, RET_HEADS, HEAD_DIM_AB, HEAD_DIM_AB), lambda bi, s: (bi, 0, 0, 0))
    tab = pl.BlockSpec((RET_HEADS, RET_CHUNK, LANES), lambda bi, s: (0, 0, 0))
    return pl.pallas_call(
        functools.partial(_ret_kernel, rows=rows, n_chunks=n_chunks),
        grid=(b, t // tc),
        in_specs=[seq] * 4 + [st] + [tab] * 4,
        out_specs=[seq, st],
        out_shape=[jax.ShapeDtypeStruct((b, t, RET_W), F32), jax.ShapeDtypeStruct(s0.shape, F32)],
        scratch_shapes=[pltpu.VMEM((RET_HEADS, HEAD_DIM_AB, HEAD_DIM_AB), F32)],
        compiler_params=_cparams(("parallel", "arbitrary")),
        name="retention",
    )(rq, rk, rv, rg, s0, *_ret_tables(chunk_len))


SB_SUB = 256
SB_ROWS = 128
LOG2E = 1.4426950408889634


def _sb_tile(q_ref, k_ref, v_ref, tri_ref, zbias, u_scr, sp_scr, w_scr, carry_scr, acc_scr, units, diagonal):
    d = HEAD_DIM_AB
    n_chunks = SB_SUB // SB_ROWS
    keys_of = lambda c: pl.ds(c * SB_SUB, SB_SUB)
    rows_of = lambda g: pl.ds(g * SB_SUB, SB_SUB)
    chunk_of = lambda g, r: pl.ds(g * SB_SUB + r * SB_ROWS, SB_ROWS)
    in_slot = lambda r: pl.ds(r * SB_ROWS, SB_ROWS)
    key = lax.broadcasted_iota(jnp.int32, (SB_ROWS, SB_SUB), 1)
    row = lax.broadcasted_iota(jnp.int32, (SB_ROWS, SB_SUB), 0)

    def mask_of(unit, r):
        c, g = unit
        return key < row + r * SB_ROWS if diagonal and g == c else None

    def scores(unit):
        c, g = unit
        return lax.dot_general(q_ref[0, rows_of(g), :], k_ref[0, keys_of(c), :], _NT, preferred_element_type=F32)

    def keep_logs(n, s):
        c, g = units[n]
        slot = n % 2
        for r in range(n_chunks):
            zs = s[r * SB_ROWS:(r + 1) * SB_ROWS, :] * (d ** -0.5 * LOG2E) + zbias
            sp = jnp.maximum(zs, 0.0) + jnp.log2(1.0 + jnp.exp2(-jnp.abs(zs)))
            carry = carry_scr[chunk_of(g, r), :]
            u_scr[slot, in_slot(r), :] = zs - sp - jnp.tile(carry, (1, SB_SUB // LANES))
            mask = mask_of(units[n], r)
            if mask is not None:
                sp = jnp.where(mask, sp, 0.0)
            sp_scr[slot, in_slot(r), :] = _bf(sp)
            carry_scr[chunk_of(g, r), :] = carry + jnp.sum(sp, axis=1, keepdims=True)
        return _dot(sp_scr[slot], tri_ref[...])

    def weights(n, sums):
        c, g = units[n]
        slot = n % 2
        for r in range(n_chunks):
            w = jnp.exp2(u_scr[slot, in_slot(r), :] - sums[r * SB_ROWS:(r + 1) * SB_ROWS, :])
            mask = mask_of(units[n], r)
            if mask is not None:
                w = jnp.where(mask, w, 0.0)
            w_scr[slot, in_slot(r), :] = _bf(w)
        acc_scr[rows_of(g), :] += _dot(w_scr[slot], v_ref[0, keys_of(c), :])

    s, sums = {}, {}
    for n in range(len(units) + 2):
        if n < len(units):
            s[n] = scores(units[n])
        if 1 <= n <= len(units):
            sums[n - 1] = keep_logs(n - 1, s.pop(n - 1))
        if n >= 2:
            weights(n - 2, sums.pop(n - 2))


def _sb_prompt_kernel(qi_tab, kj_tab, bias_ref, q_ref, k_ref, v_ref, tri_ref, o_ref,
                      acc_scr, carry_scr, u_scr, sp_scr, w_scr, *, tq):
    h = pl.program_id(0)
    p = pl.program_id(1)
    qi = qi_tab[p]
    kj = kj_tab[p]
    zbias = bias_ref[h] * LOG2E
    n_sub = tq // SB_SUB
    tile = functools.partial(_sb_tile, q_ref, k_ref, v_ref, tri_ref, zbias, u_scr, sp_scr, w_scr, carry_scr, acc_scr)

    @pl.when(kj == qi)
    def _():
        acc_scr[...] = jnp.zeros_like(acc_scr)
        carry_scr[...] = jnp.zeros_like(carry_scr)
        tile([(c, g) for c in reversed(range(n_sub)) for g in range(c, n_sub)], True)

    @pl.when(kj < qi)
    def _():
        tile([(c, g) for c in reversed(range(n_sub)) for g in range(n_sub)], False)

    @pl.when(kj == 0)
    def _():
        o_ref[...] = acc_scr[...].astype(o_ref.dtype)


def _tri_ones(n):
    i = np.arange(n)
    return jnp.asarray((i[:, None] > i[None, :]).astype(np.float32), BF16)


def _sb_prompt(q_hm, k_hm, v_hm, bias):
    nh, t, d = q_hm.shape
    tq = min(1024, t)
    nq = t // tq
    qi_tab = np.concatenate([np.full(i + 1, i) for i in range(nq)]).astype(np.int32)
    kj_tab = np.concatenate([np.arange(i, -1, -1) for i in range(nq)]).astype(np.int32)
    grid_spec = pltpu.PrefetchScalarGridSpec(
        num_scalar_prefetch=2,
        grid=(nh, len(qi_tab)),
        in_specs=[
            pl.BlockSpec(memory_space=pltpu.SMEM),
            pl.BlockSpec((1, tq, d), lambda h, p, qt, kt: (h, qt[p], 0)),
            pl.BlockSpec((1, tq, d), lambda h, p, qt, kt: (h, kt[p], 0)),
            pl.BlockSpec((1, tq, d), lambda h, p, qt, kt: (h, kt[p], 0)),
            pl.BlockSpec((SB_SUB, SB_SUB), lambda h, p, qt, kt: (0, 0)),
        ],
        out_specs=pl.BlockSpec((tq, d), lambda h, p, qt, kt: (qt[p], h)),
        scratch_shapes=[pltpu.VMEM((tq, d), F32), pltpu.VMEM((tq, LANES), F32), pltpu.VMEM((2, SB_SUB, SB_SUB), F32),
                        pltpu.VMEM((2, SB_SUB, SB_SUB), BF16), pltpu.VMEM((2, SB_SUB, SB_SUB), BF16)],
    )
    return pl.pallas_call(
        functools.partial(_sb_prompt_kernel, tq=tq),
        grid_spec=grid_spec,
        out_shape=jax.ShapeDtypeStruct((t, nh * d), BF16),
        compiler_params=_cparams(("parallel", "arbitrary")),
        name="sb_prompt",
    )(jnp.asarray(qi_tab), jnp.asarray(kj_tab), bias, q_hm, k_hm, v_hm, _tri_ones(SB_SUB))


SB_PAGES_PER_STEP = 8


def _sb_decode_kernel(pt_ref, qbd_ref, bias_ref, knew_ref, vnew_ref, trit_ref, *rest, n_pages_step, page):
    k_pages = rest[:n_pages_step]
    v_pages = rest[n_pages_step:2 * n_pages_step]
    o_ref, acc_scr, carry_scr = rest[2 * n_pages_step:]
    g = pl.program_id(1)
    qbd = qbd_ref[0]
    zbias = bias_ref[...] * LOG2E
    trit = trit_ref[...]
    d = HEAD_DIM_AB
    heads = range(SB_HEADS)

    def block(k_heads, v_heads, carry, accs, mask):
        s = functools.reduce(lambda a, c: a + c,
                             [_dot(_bf(k_heads[h]), qbd[h * d:(h + 1) * d, :]) for h in heads])
        zs = s * (d ** -0.5 * LOG2E) + zbias
        l2 = jnp.log2(1.0 + jnp.exp2(-jnp.abs(zs)))
        sp = jnp.maximum(zs, 0.0) + l2
        if mask is not None:
            sp = jnp.where(mask, sp, 0.0)
        later = _dot(trit[:page, :page], _bf(sp))
        wt = jnp.exp2(jnp.minimum(zs, 0.0) - l2 - later - carry)
        if mask is not None:
            wt = jnp.where(mask, wt, 0.0)
        w = jnp.transpose(wt)
        accs = [accs[h] + _dot(_bf(w[h * DEC_ROWS:(h + 1) * DEC_ROWS, :]), _bf(v_heads[h])) for h in heads]
        return carry + jnp.sum(sp, axis=0, keepdims=True), accs

    def store(carry, accs):
        carry_scr[...] = jnp.broadcast_to(carry, carry_scr.shape)
        for h in heads:
            acc_scr[h * DEC_ROWS:(h + 1) * DEC_ROWS, :] = accs[h]

    @pl.when(g == 0)
    def _():
        pad = jnp.zeros((page - DEC_ROWS, d), F32)
        k_heads = [jnp.concatenate([knew_ref[0, :, h * d:(h + 1) * d], pad], axis=0) for h in heads]
        v_heads = [jnp.concatenate([vnew_ref[0, :, h * d:(h + 1) * d], pad], axis=0) for h in heads]
        key = lax.broadcasted_iota(jnp.int32, (page, LANES), 0)
        qry = lax.broadcasted_iota(jnp.int32, (page, LANES), 1) & (DEC_ROWS - 1)
        store(*block(k_heads, v_heads, jnp.zeros((1, LANES), F32),
                     [jnp.zeros((DEC_ROWS, d), F32) for _ in heads], key < qry))

    unit_pages = trit_ref.shape[0] // page
    n_units = n_pages_step // unit_pages
    pages_of = lambda n: range(n_pages_step - (n + 1) * unit_pages, n_pages_step - n * unit_pages)
    head_rows = lambda ref, h: ref[0, pl.ds(h, page, stride=SB_HEADS), :]

    def scores(n):
        k_unit = jnp.concatenate([jnp.concatenate([head_rows(k_pages[u], h) for h in heads], axis=1)
                                  for u in pages_of(n)], axis=0)
        return _dot(_bf(k_unit), qbd) * (d ** -0.5 * LOG2E) + zbias

    def keep_logs(zs):
        sp = jnp.maximum(zs, 0.0) + jnp.log2(1.0 + jnp.exp2(-jnp.abs(zs)))
        return zs - sp, _dot(trit, _bf(sp)), jnp.sum(sp, axis=0, keepdims=True)

    def weights(n, logs, carry, accs):
        log_beta, later, total = logs
        wt = jnp.exp2(log_beta - later - carry)
        w = jnp.concatenate([jnp.transpose(wt[j * LANES:(j + 1) * LANES, :])
                             for j in range(wt.shape[0] // LANES)], axis=1)
        for h in heads:
            v_unit = jnp.concatenate([head_rows(v_pages[u], h) for u in pages_of(n)], axis=0)
            accs[h] = accs[h] + _dot(_bf(w[h * DEC_ROWS:(h + 1) * DEC_ROWS, :]), _bf(v_unit))
        return carry + total, accs

    carry = carry_scr[0:1, :]
    accs = [acc_scr[h * DEC_ROWS:(h + 1) * DEC_ROWS, :] for h in heads]
    zs, logs = {}, {}
    for m in range(n_units + 2):
        if m < n_units:
            zs[m] = scores(m)
        if 1 <= m <= n_units:
            logs[m - 1] = keep_logs(zs.pop(m - 1))
        if m >= 2:
            carry, accs = weights(m - 2, logs.pop(m - 2), carry, accs)
    store(carry, accs)

    @pl.when(g == pl.num_programs(1) - 1)
    def _():
        for h in heads:
            o_ref[0, :, h * d:(h + 1) * d] = acc_scr[h * DEC_ROWS:(h + 1) * DEC_ROWS, :]


def _sb_decode(sq, sk_new, sv_new, cache_k, cache_v, page_table, bias):
    b = sq.shape[0]
    n_pages = page_table.shape[1]
    page = cache_k.shape[1]
    g_pages = min(SB_PAGES_PER_STEP, n_pages)
    n_steps = n_pages // g_pages
    d = HEAD_DIM_AB
    q4 = sq.reshape(b, DEC_ROWS, SB_HEADS, d)
    eye = jnp.eye(SB_HEADS, dtype=F32)
    qbd = jnp.einsum('bihd,hg->bhdgi', q4, eye).reshape(b, SB_W, SB_HEADS * DEC_ROWS)
    qbd = _bf(jnp.pad(qbd, ((0, 0), (0, 0), (0, LANES - SB_HEADS * DEC_ROWS))))
    bias_lane = jnp.pad(jnp.repeat(bias.astype(F32), DEC_ROWS), (0, LANES - SB_HEADS * DEC_ROWS))[None, :]
    unit_keys = page * (2 if g_pages % 2 == 0 else 1)
    i = np.arange(unit_keys)
    trit = jnp.asarray((i[None, :] > i[:, None]).astype(np.float32), BF16)

    def page_spec(u):
        return pl.BlockSpec((1, page * SB_HEADS, d),
                            lambda bi, g, pt: (pt[bi * n_pages + (n_steps - 1 - g) * g_pages + u], 0, 0))

    per_seq = lambda r, w: pl.BlockSpec((1, r, w), lambda bi, g, pt: (bi, 0, 0))
    grid_spec = pltpu.PrefetchScalarGridSpec(
        num_scalar_prefetch=1,
        grid=(b, n_steps),
        in_specs=[per_seq(SB_W, LANES), pl.BlockSpec((1, LANES), lambda bi, g, pt: (0, 0)),
                  per_seq(DEC_ROWS, SB_W), per_seq(DEC_ROWS, SB_W),
                  pl.BlockSpec((unit_keys, unit_keys), lambda bi, g, pt: (0, 0))]
                 + [page_spec(u) for u in range(g_pages)] * 2,
        out_specs=per_seq(DEC_ROWS, SB_W),
        scratch_shapes=[pltpu.VMEM((SB_HEADS * DEC_ROWS, d), F32), pltpu.VMEM((DEC_ROWS, LANES), F32)],
    )
    return pl.pallas_call(
        functools.partial(_sb_decode_kernel, n_pages_step=g_pages, page=page),
        grid_spec=grid_spec,
        out_shape=jax.ShapeDtypeStruct((b, DEC_ROWS, SB_W), F32),
        compiler_params=_cparams(("parallel", "arbitrary")),
        name="sb_decode",
    )(page_table.reshape(-1), qbd, bias_lane, sk_new, sv_new, trit,
      *([cache_k.reshape(-1, page * SB_HEADS, d)] * g_pages), *([cache_v.reshape(-1, page * SB_HEADS, d)] * g_pages))


def _inproj_c_kernel(x_ref, g_ref, w_ref, q_ref, k_ref, v_ref):
    a = _rms(x_ref[...], g_ref[...])
    p = _dot(_bf(a), w_ref[...])
    q_ref[...] = p[:, :DIL_W]
    k_ref[...] = p[:, DIL_W:2 * DIL_W]
    v_ref[...] = p[:, 2 * DIL_W:]


def _inproj_c(x, gain, w_bf):
    m, dm = x.shape
    tm = min(m, 256)
    row = pl.BlockSpec((tm, DIL_W), lambda i: (i, 0))
    return pl.pallas_call(
        _inproj_c_kernel,
        grid=(m // tm,),
        in_specs=[pl.BlockSpec((tm, dm), lambda i: (i, 0)), pl.BlockSpec((1, dm), lambda i: (0, 0)),
                  pl.BlockSpec((dm, 3 * DIL_W), lambda i: (0, 0))],
        out_specs=[row] * 3,
        out_shape=[jax.ShapeDtypeStruct((m, DIL_W), F32)] * 3,
        compiler_params=_cparams(("parallel",)),
        name="inproj_c",
    )(x, gain, w_bf)


DIL_SUPER = DIL_WINDOW_MAX
DIL_SKEW = 2


def _dil_prompt_kernel(q_ref, kp_ref, kc_ref, vp_ref, vc_ref, o_ref, kwin, vwin, o_scr, lse_scr):
    sb = pl.program_id(0)
    blk = DIL_SPAN
    kwin[:DIL_SUPER, :] = kp_ref[...]
    kwin[DIL_SUPER:, :] = kc_ref[...]
    vwin[:DIL_SUPER, :] = vp_ref[...]
    vwin[DIL_SUPER:, :] = vc_ref[...]
    rows = lax.broadcasted_iota(jnp.int32, (blk, 2 * blk), 0)
    cols = lax.broadcasted_iota(jnp.int32, (blk, 2 * blk), 1)
    dist = blk + rows - cols
    in_span = (dist >= 0) & (dist <= DIL_SPAN)
    in_span_first = in_span & ((cols >= blk) | (sb > 0))
    lane = lax.broadcasted_iota(jnp.int32, (blk, LANES), 1)
    first = lane < DIL_HEAD_DIM
    scale = DIL_HEAD_DIM ** -0.5
    units = [(g, dil, r, j) for g, (_, dil) in enumerate(DIL_BRANCHES)
             for j in range(DIL_SUPER // (dil * blk)) for r in range(dil)]

    def scores(unit):
        g, dil, r, j = unit
        q = q_ref[pl.ds(r + dil * blk * j, blk, stride=dil), :]
        k2 = _bf(kwin[pl.ds(DIL_SUPER + r + dil * blk * (j - 1), 2 * blk, stride=dil), :])
        zero = jnp.zeros_like(q)
        return [lax.dot_general(_bf(qh), k2, _NT, preferred_element_type=F32) * scale
                for qh in (jnp.where(first, q, zero), jnp.where(first, zero, q))]

    def attend(unit, zs):
        g, dil, r, j = unit
        v2 = _bf(vwin[pl.ds(DIL_SUPER + r + dil * blk * (j - 1), 2 * blk, stride=dil), :])
        valid = in_span_first if j == 0 else in_span
        outs, lses = [], []
        for z in zs:
            z = jnp.where(valid, z, -jnp.inf)
            m = jnp.max(z, axis=-1, keepdims=True)
            e = jnp.exp(z - m)
            den = jnp.sum(e, axis=-1, keepdims=True)
            outs.append(_dot(_bf(e), v2) / den)
            lses.append(m + jnp.log(den))
        dst = pl.ds(r + dil * blk * j, blk, stride=dil)
        o_scr[g, dst, :] = jnp.where(first, outs[0], outs[1])
        lse_scr[g, dst, :] = jnp.where(first, lses[0], lses[1])

    pending = {}
    for n in range(len(units) + DIL_SKEW):
        if n < len(units):
            pending[n] = scores(units[n])
        if n >= DIL_SKEW:
            attend(units[n - DIL_SKEW], pending.pop(n - DIL_SKEW))

    n_br = len(DIL_BRANCHES)
    for c in range(DIL_SUPER // blk):
        rs = slice(c * blk, (c + 1) * blk)
        lses = [lse_scr[g, rs, :] for g in range(n_br)]
        top = functools.reduce(jnp.maximum, lses)
        ws = [jnp.exp(l - top) for l in lses]
        num = functools.reduce(lambda a, b: a + b, [w * o_scr[g, rs, :] for g, w in enumerate(ws)])
        o_ref[rs, :] = (num / functools.reduce(lambda a, b: a + b, ws)).astype(o_ref.dtype)


def _dil_prompt(q, k, v):
    t = q.shape[0]
    assert t % DIL_SUPER == 0
    cur = pl.BlockSpec((DIL_SUPER, LANES), lambda s, hp: (s, hp))
    prev = pl.BlockSpec((DIL_SUPER, LANES), lambda s, hp: (jnp.maximum(s - 1, 0), hp))
    n_br = len(DIL_BRANCHES)
    return pl.pallas_call(
        _dil_prompt_kernel,
        grid=(t // DIL_SUPER, DIL_W // LANES),
        in_specs=[cur, prev, cur, prev, cur],
        out_specs=cur,
        out_shape=jax.ShapeDtypeStruct((t, DIL_W), BF16),
        scratch_shapes=[pltpu.VMEM((2 * DIL_SUPER, LANES), F32), pltpu.VMEM((2 * DIL_SUPER, LANES), F32),
                        pltpu.VMEM((n_br, DIL_SUPER, LANES), F32), pltpu.VMEM((n_br, DIL_SUPER, LANES), F32)],
        compiler_params=_cparams(("parallel", "parallel")),
        name="dilated_prompt",
    )(q, k, k, v, v)


def _dil_multiplicity(dist):
    cnt = jnp.zeros(dist.shape, F32)
    for window, dil in DIL_BRANCHES:
        hit = (dist >= 0) & (dist <= window) & ((dist & (dil - 1)) == 0)
        cnt = cnt + jnp.where(hit, 1.0, 0.0)
    return cnt


def _dil_decode_kernel(q_ref, kt_ref, vt_ref, knew_ref, vnew_ref, o_ref, kout_ref, vout_ref, *, n_past, n_new):
    scale = DIL_HEAD_DIM ** -0.5
    heads = range(kt_ref.shape[1])
    qi = lax.broadcasted_iota(jnp.int32, (DEC_ROWS, n_past), 0)
    key = lax.broadcasted_iota(jnp.int32, (DEC_ROWS, n_past), 1)
    cnt_past = _dil_multiplicity(n_past + qi - key)
    qi_new = lax.broadcasted_iota(jnp.int32, (DEC_ROWS, LANES), 0)
    lane = lax.broadcasted_iota(jnp.int32, (DEC_ROWS, LANES), 1)
    j_new = lane - (LANES - n_new)
    cnt_new = jnp.where(j_new >= 0, _dil_multiplicity(qi_new - j_new), 0.0)

    qs = [_bf(q_ref[0, h]) for h in heads]
    z_past = [_dot(qs[h], _bf(kt_ref[0, h])) * scale for h in heads]
    z_new = [_dot(qs[h], _bf(knew_ref[0, h])) * scale for h in heads]
    ps = []
    for h in heads:
        m = jnp.maximum(jnp.max(jnp.where(cnt_past > 0, z_past[h], -jnp.inf), axis=1, keepdims=True),
                        jnp.max(jnp.where(cnt_new > 0, z_new[h], -jnp.inf), axis=1, keepdims=True))
        p_past = jnp.where(cnt_past > 0, cnt_past * jnp.exp(z_past[h] - m), 0.0)
        p_new = jnp.where(cnt_new > 0, cnt_new * jnp.exp(z_new[h] - m), 0.0)
        den = jnp.sum(p_past, axis=1, keepdims=True) + jnp.sum(p_new, axis=1, keepdims=True)
        ps.append((p_past, p_new, den))
    for h in heads:
        p_past, p_new, den = ps[h]
        num = (lax.dot_general(_bf(p_past), _bf(vt_ref[0, h]), _NT, preferred_element_type=F32)
               + lax.dot_general(_bf(p_new), _bf(vnew_ref[0, h]), _NT, preferred_element_type=F32))
        o_ref[0, h] = num / den

    tail = lax.broadcasted_iota(jnp.int32, (DIL_HEAD_DIM, LANES), 1) >= LANES - n_new
    for src, new, dst in ((kt_ref, knew_ref, kout_ref), (vt_ref, vnew_ref, vout_ref)):
        for h in heads:
            shifted = pltpu.roll(src[0, h], n_past - n_new, 1)
            dst[0, h, :, :n_past - LANES] = shifted[:, :n_past - LANES]
            dst[0, h, :, n_past - LANES:] = jnp.where(tail, new[0, h], shifted[:, n_past - LANES:])


DIL_DEC_HEADS_PER_STEP = 4


def _dil_decode(q, k_new, v_new, cache_k, cache_v):
    b, n_new, _ = q.shape
    n_past = cache_k.shape[1]
    assert n_past == DIL_WINDOW_MAX and n_new <= DEC_ROWS
    hd = (DIL_HEADS, DIL_HEAD_DIM)
    to_t = lambda a: jnp.transpose(a, (0, 2, 3, 1))
    new_t = lambda a: jnp.pad(to_t(a.reshape(b, n_new, *hd)), ((0, 0), (0, 0), (0, 0), (LANES - n_new, 0)))
    q_rows = jnp.pad(jnp.transpose(q.reshape(b, n_new, *hd), (0, 2, 1, 3)),
                     ((0, 0), (0, 0), (0, DEC_ROWS - n_new), (0, 0)))
    hg = DIL_DEC_HEADS_PER_STEP
    spec = lambda r, w: pl.BlockSpec((1, hg, r, w), lambda bi, g: (bi, g, 0, 0))
    cache_shape = jax.ShapeDtypeStruct((b, DIL_HEADS, DIL_HEAD_DIM, n_past), F32)
    o, k_out, v_out = pl.pallas_call(
        functools.partial(_dil_decode_kernel, n_past=n_past, n_new=n_new),
        grid=(b, DIL_HEADS // hg),
        in_specs=[spec(DEC_ROWS, DIL_HEAD_DIM), spec(DIL_HEAD_DIM, n_past), spec(DIL_HEAD_DIM, n_past),
                  spec(DIL_HEAD_DIM, LANES), spec(DIL_HEAD_DIM, LANES)],
        out_specs=[spec(DEC_ROWS, DIL_HEAD_DIM), spec(DIL_HEAD_DIM, n_past), spec(DIL_HEAD_DIM, n_past)],
        out_shape=[jax.ShapeDtypeStruct((b, DIL_HEADS, DEC_ROWS, DIL_HEAD_DIM), F32), cache_shape, cache_shape],
        compiler_params=_cparams(("parallel", "parallel")),
        name="dilated_decode",
    )(q_rows, to_t(cache_k), to_t(cache_v), new_t(k_new), new_t(v_new))
    from_t = lambda a: jnp.transpose(a, (0, 3, 1, 2))
    o = jnp.transpose(o[:, :, :n_new, :], (0, 2, 1, 3)).reshape(b, n_new, DIL_W)
    return o, from_t(k_out), from_t(v_out)


def _mix_concat(*refs):
    return jnp.concatenate([_bf(r[...]) for r in refs], axis=-1)


def _tail_kernel(*refs, n_mix, mix_fn):
    mix_refs = refs[:n_mix]
    (h_ref, wout_ref, gpost_ref, gpre_ref, w1_ref, w2_ref, gffn_ref,
     out_ref, h1_scr, a_scr, acc_scr) = refs[n_mix:]
    f = pl.program_id(1)

    @pl.when(f == 0)
    def _():
        m = _dot(mix_fn(*mix_refs), wout_ref[...])
        h1 = h_ref[...] + _rms(m, gpost_ref[...])
        h1_scr[...] = h1
        a_scr[...] = _bf(_rms(h1, gpre_ref[...]))
        acc_scr[...] = jnp.zeros_like(acc_scr)

    hid = jnp.square(jnp.maximum(_dot(a_scr[...], w1_ref[...]), 0.0))
    acc_scr[...] += _dot(_bf(hid), w2_ref[...])

    @pl.when(f == pl.num_programs(1) - 1)
    def _():
        out_ref[...] = h1_scr[...] + _rms(acc_scr[...], gffn_ref[...])


def _layer_tail(mix, mix_fn, h, w_out, g_post, g_ffn_pre, w1, w2, g_ffn_post, tm):
    m, dm = h.shape
    dff = w1.shape[1]
    tm = min(m, tm)
    tf = min(dff, 1024)
    row = lambda w: pl.BlockSpec((tm, w), lambda i, f: (i, 0))
    gain = pl.BlockSpec((1, dm), lambda i, f: (0, 0))
    return pl.pallas_call(
        functools.partial(_tail_kernel, n_mix=len(mix), mix_fn=mix_fn),
        grid=(m // tm, dff // tf),
        in_specs=[row(a.shape[1]) for a in mix]
                 + [row(dm), pl.BlockSpec(w_out.shape, lambda i, f: (0, 0)), gain, gain,
                    pl.BlockSpec((dm, tf), lambda i, f: (0, f)), pl.BlockSpec((tf, dm), lambda i, f: (f, 0)), gain],
        out_specs=row(dm),
        out_shape=jax.ShapeDtypeStruct((m, dm), F32),
        scratch_shapes=[pltpu.VMEM((tm, dm), F32), pltpu.VMEM((tm, dm), BF16), pltpu.VMEM((tm, dm), F32)],
        compiler_params=_cparams(("parallel", "arbitrary")),
        name="layer_tail",
    )(*mix, h, w_out, g_post, g_ffn_pre, w1, w2, g_ffn_post)


def _rope_tables(pos):
    half = HEAD_DIM_AB // 2
    inv_freq = jnp.power(ROPE_BASE, -jnp.linspace(0.0, 1.0, half, dtype=F32))
    ang = pos.astype(F32)[:, None] * inv_freq[None, :]
    cos, sin = jnp.cos(ang), jnp.sin(ang)
    return jnp.concatenate([cos, cos], axis=-1), jnp.concatenate([-sin, sin], axis=-1)


def _pad_rows(a, rows):
    return jnp.pad(a, ((0, 0), (0, rows - a.shape[1]), (0, 0)))


def kernel(x_prompt, x_sample, cache_sb_k, cache_sb_v, state_ret, cache_swa_k, cache_swa_v, page_table,
           w_in_ab, w_out_ab, sb_bias, w_in_c, w_out_c, w_ff1, w_ff2, g_mix_pre, g_mix_post, g_ffn_pre, g_ffn_post):
    bp, t, dm = x_prompt.shape
    bs, ts, _ = x_sample.shape
    assert bp == 1 and ts <= DEC_ROWS and t % RET_CHUNK == 0
    n_pages = page_table.shape[1]
    page = cache_sb_k.shape[1]
    past_len = n_pages * page
    d = HEAD_DIM_AB
    gain = lambda g, layer: g[layer][None, :].astype(F32)
    w_in_ab_bf, w_out_ab_bf, w_in_c_bf, w_out_c_bf = _bf(w_in_ab), _bf(w_out_ab), _bf(w_in_c), _bf(w_out_c)
    w_ff1_bf, w_ff2_bf = _bf(w_ff1), _bf(w_ff2)

    h_p = x_prompt.reshape(t, dm)
    h_s = x_sample.reshape(bs * ts, dm)

    cos_p, sin_p = _rope_tables(jnp.arange(t, dtype=jnp.int32))
    rq, rk, rv, rg, q_hm, k_hm, v_hm, sb_k_p, sb_v_p = _inproj_ab(h_p, gain(g_mix_pre, 0), w_in_ab_bf, cos_p, sin_p)
    seq = lambda a: a.reshape(1, t, RET_W)
    ret_zero = jnp.zeros((1, RET_HEADS, d, d), F32)
    y_ret_p, ret_p = _retention(seq(rq), seq(rk), seq(rv), seq(rg), ret_zero, float(RET_CHUNK))
    o_sb_p = _sb_prompt(q_hm, k_hm, v_hm, sb_bias.astype(F32))
    h_p = _layer_tail([y_ret_p.reshape(t, RET_W), o_sb_p], _mix_concat, h_p, w_out_ab_bf, gain(g_mix_post, 0),
                      gain(g_ffn_pre, 0), w_ff1_bf[0], w_ff2_bf[0], gain(g_ffn_post, 0), tm=512)

    cos_s, sin_s = _rope_tables(past_len + jnp.arange(ts, dtype=jnp.int32))
    tile_s = lambda a: jnp.tile(a, (bs, 1))
    rq, rk, rv, rg, q_hm, _, _, sb_k_s, sb_v_s = _inproj_ab(h_s, gain(g_mix_pre, 0), w_in_ab_bf,
                                                            tile_s(cos_s), tile_s(sin_s))
    dec = lambda a: _pad_rows(a.reshape(bs, ts, a.shape[-1]), DEC_ROWS)
    y_ret_s, ret_s = _retention(dec(rq), dec(rk), dec(rv), dec(rg), state_ret.astype(F32), float(ts))
    sq_s = jnp.transpose(q_hm.astype(F32), (1, 0, 2)).reshape(bs * ts, SB_W)
    o_sb_s = _sb_decode(dec(sq_s), dec(sb_k_s.reshape(bs * ts, SB_W)), dec(sb_v_s.reshape(bs * ts, SB_W)),
                        cache_sb_k, cache_sb_v, page_table, sb_bias.astype(F32))
    undec = lambda a: a[:, :ts].reshape(bs * ts, a.shape[-1])
    h_s = _layer_tail([undec(y_ret_s), undec(o_sb_s)], _mix_concat, h_s, w_out_ab_bf, gain(g_mix_post, 0),
                      gain(g_ffn_pre, 0), w_ff1_bf[0], w_ff2_bf[0], gain(g_ffn_post, 0), tm=512)

    q, k_f32, v_f32 = _inproj_c(h_p, gain(g_mix_pre, 1), w_in_c_bf)
    h_p = _layer_tail([_dil_prompt(q, k_f32, v_f32)], _mix_concat, h_p, w_out_c_bf, gain(g_mix_post, 1),
                      gain(g_ffn_pre, 1), w_ff1_bf[1], w_ff2_bf[1], gain(g_ffn_post, 1), tm=512)
    keep_p = min(DIL_WINDOW_MAX, t)
    swa_k_p = k_f32[t - keep_p:].reshape(1, keep_p, DIL_HEADS, DIL_HEAD_DIM)
    swa_v_p = v_f32[t - keep_p:].reshape(1, keep_p, DIL_HEADS, DIL_HEAD_DIM)

    q, k_f32, v_f32 = _inproj_c(h_s, gain(g_mix_pre, 1), w_in_c_bf)
    per_seq = lambda a: a.reshape(bs, ts, DIL_W)
    o_dil_s, swa_k_s, swa_v_s = _dil_decode(per_seq(q.astype(F32)), per_seq(k_f32), per_seq(v_f32),
                                            cache_swa_k.astype(F32), cache_swa_v.astype(F32))
    h_s = _layer_tail([o_dil_s.reshape(bs * ts, DIL_W)], _mix_concat, h_s, w_out_c_bf, gain(g_mix_post, 1),
                      gain(g_ffn_pre, 1), w_ff1_bf[1], w_ff2_bf[1], gain(g_ffn_post, 1), tm=512)

    heads_ab = lambda a, b: a.reshape(b, -1, SB_HEADS, d)
    return (h_p.reshape(1, t, dm), h_s.reshape(bs, ts, dm),
            heads_ab(sb_k_p, 1), heads_ab(sb_v_p, 1), ret_p,
            swa_k_p, swa_v_p,
            heads_ab(sb_k_s, bs), heads_ab(sb_v_s, bs), ret_s,
            swa_k_s, swa_v_s)
```

```python
import functools

import numpy as np
import jax
import jax.numpy as jnp
from jax import lax
from jax.experimental import pallas as pl
from jax.experimental.pallas import tpu as pltpu

F32 = jnp.float32
BF16 = jnp.bfloat16

NORM_EPS = 1e-6
ROPE_BASE = 10000.0
RET_HEADS = 4
SB_HEADS = 4
HEAD_DIM_AB = 128
RET_W = RET_HEADS * HEAD_DIM_AB
SB_W = SB_HEADS * HEAD_DIM_AB
RET_CHUNK = 128
DIL_HEADS = 16
DIL_HEAD_DIM = 64
DIL_W = DIL_HEADS * DIL_HEAD_DIM
DIL_BRANCHES = ((128, 1), (512, 4), (2048, 16))
DIL_WINDOW_MAX = 2048
DIL_SPAN = 128
LANES = 128
DEC_ROWS = 8
VMEM_LIMIT = 56 * 1024 * 1024

_NT = (((1,), (1,)), ((), ()))
_TN = (((0,), (0,)), ((), ()))


def _cparams(sem):
    return pltpu.CompilerParams(dimension_semantics=sem, vmem_limit_bytes=VMEM_LIMIT)


def _bf(x):
    return x.astype(BF16)


def _dot(a, b):
    return jnp.dot(a, b, preferred_element_type=F32)


def _rms(x, gain):
    return x * lax.rsqrt(jnp.mean(x * x, axis=-1, keepdims=True) + NORM_EPS) * gain


def _softplus(z):
    return jnp.maximum(z, 0.0) + jnp.log1p(jnp.exp(-jnp.abs(z)))


def _inproj_ab_kernel(x_ref, g_ref, w_ref, cos_ref, sin_ref,
                      rq_ref, rk_ref, rv_ref, rg_ref, q_hm, k_hm, v_hm, sbk_ref, sbv_ref):
    a = _rms(x_ref[...], g_ref[...])
    p = _dot(_bf(a), w_ref[...])
    cosf = cos_ref[...]
    sinf = sin_ref[...]
    d = HEAD_DIM_AB
    for h in range(RET_HEADS):
        qh = p[:, h * d:(h + 1) * d]
        kh = p[:, RET_W + h * d:RET_W + (h + 1) * d]
        rq_ref[:, h * d:(h + 1) * d] = qh * cosf + pltpu.roll(qh, d // 2, 1) * sinf
        rk_ref[:, h * d:(h + 1) * d] = (kh * cosf + pltpu.roll(kh, d // 2, 1) * sinf) * (d ** -0.5)
    rv_ref[...] = p[:, 2 * RET_W:3 * RET_W]
    rg_ref[...] = p[:, 3 * RET_W:4 * RET_W]
    base = 4 * RET_W
    for h in range(SB_HEADS):
        q_hm[h] = _bf(p[:, base + h * d:base + (h + 1) * d])
        k_hm[h] = _bf(p[:, base + SB_W + h * d:base + SB_W + (h + 1) * d])
        v_hm[h] = _bf(p[:, base + 2 * SB_W + h * d:base + 2 * SB_W + (h + 1) * d])
    tm = p.shape[0]
    for h in range(SB_HEADS):
        sbk_ref[pl.ds(h, tm, stride=SB_HEADS), :] = p[:, base + SB_W + h * d:base + SB_W + (h + 1) * d]
        sbv_ref[pl.ds(h, tm, stride=SB_HEADS), :] = p[:, base + 2 * SB_W + h * d:base + 2 * SB_W + (h + 1) * d]


def _inproj_ab(x, gain, w_bf, cosf, sinf):
    m, dm = x.shape
    n = w_bf.shape[1]
    tm = min(m, 256)
    row = lambda w: pl.BlockSpec((tm, w), lambda i: (i, 0))
    hm = pl.BlockSpec((SB_HEADS, tm, HEAD_DIM_AB), lambda i: (0, i, 0))
    f32_rows = jax.ShapeDtypeStruct((m, RET_W), F32)
    hm_shape = jax.ShapeDtypeStruct((SB_HEADS, m, HEAD_DIM_AB), BF16)
    return pl.pallas_call(
        _inproj_ab_kernel,
        grid=(m // tm,),
        in_specs=[row(dm), pl.BlockSpec((1, dm), lambda i: (0, 0)),
                  pl.BlockSpec((dm, n), lambda i: (0, 0)), row(HEAD_DIM_AB), row(HEAD_DIM_AB)],
        out_specs=[row(RET_W)] * 4 + [hm] * 3 + [pl.BlockSpec((tm * SB_HEADS, HEAD_DIM_AB), lambda i: (i, 0))] * 2,
        out_shape=[f32_rows] * 4 + [hm_shape] * 3 + [jax.ShapeDtypeStruct((m * SB_HEADS, HEAD_DIM_AB), F32)] * 2,
        compiler_params=_cparams(("parallel",)),
        name="inproj_ab",
    )(x, gain, w_bf, cosf, sinf)


def _ret_kernel(rq, rk, rv, rg, s0, intra, qdec, kdec, cdec, y_ref, s_out, s_scr, *, rows, n_chunks):
    step = pl.program_id(1)
    d = HEAD_DIM_AB

    @pl.when(step == 0)
    def _():
        s_scr[...] = s0[0]

    for c in range(n_chunks):
        rs = slice(c * rows, (c + 1) * rows)
        for h in range(RET_HEADS):
            cs = slice(h * d, (h + 1) * d)
            q, k, v, g = rq[0, rs, cs], rk[0, rs, cs], rv[0, rs, cs], rg[0, rs, cs]
            if rows < RET_CHUNK:
                pad = jnp.zeros((RET_CHUNK - rows, d), F32)
                k = jnp.concatenate([k, pad], axis=0)
                v = jnp.concatenate([v, pad], axis=0)
            s_prev = s_scr[h]
            att = lax.dot_general(_bf(q), _bf(k), _NT, preferred_element_type=F32) * intra[h, :rows, :]
            o = _dot(_bf(att), _bf(v)) + _dot(_bf(q * qdec[h, :rows, :]), _bf(s_prev))
            kd = k * kdec[h]
            s_scr[h] = s_prev * cdec[h] + lax.dot_general(_bf(kd), _bf(v), _TN, preferred_element_type=F32)
            xc = o - jnp.mean(o, axis=-1, keepdims=True)
            var = jnp.mean(xc * xc, axis=-1, keepdims=True)
            y_ref[0, rs, cs] = xc * lax.rsqrt(var + NORM_EPS) * (g * jax.nn.sigmoid(g))

    @pl.when(step == pl.num_programs(1) - 1)
    def _():
        s_out[0] = s_scr[...]


def _ret_tables(chunk_len):
    lg = jnp.log1p(-jnp.exp2(-5.0 - jnp.arange(RET_HEADS, dtype=F32)))
    i = jnp.arange(RET_CHUNK, dtype=F32)
    live = i < chunk_len
    rel = i[:, None] - i[None, :]
    intra = jnp.where((rel >= 0) & live[:, None] & live[None, :],
                      jnp.exp(lg[:, None, None] * jnp.maximum(rel, 0.0)), 0.0)
    q_dec = jnp.where(live[None, :], jnp.exp(lg[:, None] * (i[None, :] + 1.0)), 0.0)
    k_dec = jnp.where(live[None, :], jnp.exp(lg[:, None] * (chunk_len - 1.0 - i[None, :])), 0.0)
    c_dec = jnp.exp(lg * chunk_len)
    full = lambda t: jnp.broadcast_to(t[:, :, None], (RET_HEADS, RET_CHUNK, LANES))
    return intra, full(q_dec), full(k_dec), jnp.broadcast_to(c_dec[:, None, None], (RET_HEADS, RET_CHUNK, LANES))


def _retention(rq, rk, rv, rg, s0, chunk_len):
    b, t, _ = rq.shape
    if t % RET_CHUNK == 0:
        rows = RET_CHUNK
        n_chunks = min(8, t // rows)
    else:
        rows, n_chunks = t, 1
    tc = rows * n_chunks
    seq = pl.BlockSpec((1, tc, RET_W), lambda bi, s: (bi, s, 0))
    st = pl.BlockSpec((1, RET_HEADS, HEAD_DIM_AB, HEAD_DIM_AB), lambda bi, s: (bi, 0, 0, 0))
    tab = pl.BlockSpec((RET_HEADS, RET_CHUNK, LANES), lambda bi, s: (0, 0, 0))
    return pl.pallas_call(
        functools.partial(_ret_kernel, rows=rows, n_chunks=n_chunks),
        grid=(b, t // tc),
        in_specs=[seq] * 4 + [st] + [tab] * 4,
        out_specs=[seq, st],
        out_shape=[jax.ShapeDtypeStruct((b, t, RET_W), F32), jax.ShapeDtypeStruct(s0.shape, F32)],
        scratch_shapes=[pltpu.VMEM((RET_HEADS, HEAD_DIM_AB, HEAD_DIM_AB), F32)],
        compiler_params=_cparams(("parallel", "arbitrary")),
        name="retention",
    )(rq, rk, rv, rg, s0, *_ret_tables(chunk_len))


SB_SUB = 256
SB_ROWS = 128
LOG2E = 1.4426950408889634
SB_DECAYED = 152.0


def _sb_tile(q_ref, k_ref, v_ref, tri_ref, zbias, u_scr, sp_scr, w_scr, carry_scr, acc_scr, units, diagonal):
    d = HEAD_DIM_AB
    n_chunks = SB_SUB // SB_ROWS
    keys_of = lambda c: pl.ds(c * SB_SUB, SB_SUB)
    rows_of = lambda g: pl.ds(g * SB_SUB, SB_SUB)
    chunk_of = lambda g, r: pl.ds(g * SB_SUB + r * SB_ROWS, SB_ROWS)
    in_slot = lambda r: pl.ds(r * SB_ROWS, SB_ROWS)
    key = lax.broadcasted_iota(jnp.int32, (SB_ROWS, SB_SUB), 1)
    row = lax.broadcasted_iota(jnp.int32, (SB_ROWS, SB_SUB), 0)

    def mask_of(unit, r):
        c, g = unit
        return key < row + r * SB_ROWS if diagonal and g == c else None

    def scores(unit):
        c, g = unit
        return lax.dot_general(q_ref[0, rows_of(g), :], k_ref[0, keys_of(c), :], _NT, preferred_element_type=F32)

    def keep_logs(n, s):
        c, g = units[n]
        slot = n % 2
        for r in range(n_chunks):
            zs = s[r * SB_ROWS:(r + 1) * SB_ROWS, :] * (d ** -0.5 * LOG2E) + zbias
            sp = jnp.maximum(zs, 0.0) + jnp.log2(1.0 + jnp.exp2(-jnp.abs(zs)))
            carry = carry_scr[chunk_of(g, r), :]
            u_scr[slot, in_slot(r), :] = zs - sp - jnp.tile(carry, (1, SB_SUB // LANES))
            mask = mask_of(units[n], r)
            if mask is not None:
                sp = jnp.where(mask, sp, 0.0)
            sp_scr[slot, in_slot(r), :] = _bf(sp)
            carry_scr[chunk_of(g, r), :] = carry + jnp.sum(sp, axis=1, keepdims=True)
        return _dot(sp_scr[slot], tri_ref[...])

    def weights(n, sums):
        c, g = units[n]
        slot = n % 2
        for r in range(n_chunks):
            w = jnp.exp2(u_scr[slot, in_slot(r), :] - sums[r * SB_ROWS:(r + 1) * SB_ROWS, :])
            mask = mask_of(units[n], r)
            if mask is not None:
                w = jnp.where(mask, w, 0.0)
            w_scr[slot, in_slot(r), :] = _bf(w)
        acc_scr[rows_of(g), :] += _dot(w_scr[slot], v_ref[0, keys_of(c), :])

    s, sums = {}, {}
    for n in range(len(units) + 2):
        if n < len(units):
            s[n] = scores(units[n])
        if 1 <= n <= len(units):
            sums[n - 1] = keep_logs(n - 1, s.pop(n - 1))
        if n >= 2:
            weights(n - 2, sums.pop(n - 2))


def _sb_prompt_kernel(qi_tab, kj_tab, bias_ref, q_ref, k_ref, v_ref, tri_ref, o_ref,
                      acc_scr, carry_scr, u_scr, sp_scr, w_scr, *, tq):
    h = pl.program_id(0)
    p = pl.program_id(1)
    qi = qi_tab[p]
    kj = kj_tab[p]
    zbias = bias_ref[h] * LOG2E
    n_sub = tq // SB_SUB
    tile = functools.partial(_sb_tile, q_ref, k_ref, v_ref, tri_ref, zbias, u_scr, sp_scr, w_scr, carry_scr, acc_scr)

    @pl.when(kj == qi)
    def _():
        acc_scr[...] = jnp.zeros_like(acc_scr)
        carry_scr[...] = jnp.zeros_like(carry_scr)
        tile([(c, g) for c in reversed(range(n_sub)) for g in range(c, n_sub)], True)

    @pl.when(kj < qi)
    def _():
        least = carry_scr[...]
        while least.shape[0] > 8:
            half = least.shape[0] // 2
            least = jnp.minimum(least[:half], least[half:])

        @pl.when(jnp.min(least) <= SB_DECAYED)
        def _():
            tile([(c, g) for c in reversed(range(n_sub)) for g in range(n_sub)], False)

    @pl.when(kj == 0)
    def _():
        o_ref[...] = acc_scr[...].astype(o_ref.dtype)


def _tri_ones(n):
    i = np.arange(n)
    return jnp.asarray((i[:, None] > i[None, :]).astype(np.float32), BF16)


def _sb_prompt(q_hm, k_hm, v_hm, bias):
    nh, t, d = q_hm.shape
    tq = min(1024, t)
    nq = t // tq
    qi_tab = np.concatenate([np.full(i + 1, i) for i in range(nq)]).astype(np.int32)
    kj_tab = np.concatenate([np.arange(i, -1, -1) for i in range(nq)]).astype(np.int32)
    grid_spec = pltpu.PrefetchScalarGridSpec(
        num_scalar_prefetch=2,
        grid=(nh, len(qi_tab)),
        in_specs=[
            pl.BlockSpec(memory_space=pltpu.SMEM),
            pl.BlockSpec((1, tq, d), lambda h, p, qt, kt: (h, qt[p], 0)),
            pl.BlockSpec((1, tq, d), lambda h, p, qt, kt: (h, kt[p], 0)),
            pl.BlockSpec((1, tq, d), lambda h, p, qt, kt: (h, kt[p], 0)),
            pl.BlockSpec((SB_SUB, SB_SUB), lambda h, p, qt, kt: (0, 0)),
        ],
        out_specs=pl.BlockSpec((tq, d), lambda h, p, qt, kt: (qt[p], h)),
        scratch_shapes=[pltpu.VMEM((tq, d), F32), pltpu.VMEM((tq, LANES), F32), pltpu.VMEM((2, SB_SUB, SB_SUB), F32),
                        pltpu.VMEM((2, SB_SUB, SB_SUB), BF16), pltpu.VMEM((2, SB_SUB, SB_SUB), BF16)],
    )
    return pl.pallas_call(
        functools.partial(_sb_prompt_kernel, tq=tq),
        grid_spec=grid_spec,
        out_shape=jax.ShapeDtypeStruct((t, nh * d), BF16),
        compiler_params=_cparams(("parallel", "arbitrary")),
        name="sb_prompt",
    )(jnp.asarray(qi_tab), jnp.asarray(kj_tab), bias, q_hm, k_hm, v_hm, _tri_ones(SB_SUB))


SB_PAGES_PER_STEP = 16


def _sb_decode_kernel(pt_ref, qbd_ref, bias_ref, knew_ref, vnew_ref, trit_ref, *rest, n_pages_step, page):
    k_pages = rest[:n_pages_step]
    v_pages = rest[n_pages_step:2 * n_pages_step]
    o_ref, acc_scr, carry_scr = rest[2 * n_pages_step:]
    g = pl.program_id(1)
    qbd = qbd_ref[0]
    zbias = bias_ref[...] * LOG2E
    trit = trit_ref[...]
    d = HEAD_DIM_AB
    heads = range(SB_HEADS)

    def block(k_heads, v_heads, carry, accs, mask):
        s = functools.reduce(lambda a, c: a + c,
                             [_dot(_bf(k_heads[h]), qbd[h * d:(h + 1) * d, :]) for h in heads])
        zs = s * (d ** -0.5 * LOG2E) + zbias
        l2 = jnp.log2(1.0 + jnp.exp2(-jnp.abs(zs)))
        sp = jnp.maximum(zs, 0.0) + l2
        if mask is not None:
            sp = jnp.where(mask, sp, 0.0)
        later = _dot(trit[:page, :page], _bf(sp))
        wt = jnp.exp2(jnp.minimum(zs, 0.0) - l2 - later - carry)
        if mask is not None:
            wt = jnp.where(mask, wt, 0.0)
        w = jnp.transpose(wt)
        accs = [accs[h] + _dot(_bf(w[h * DEC_ROWS:(h + 1) * DEC_ROWS, :]), _bf(v_heads[h])) for h in heads]
        return carry + jnp.sum(sp, axis=0, keepdims=True), accs

    def store(carry, accs):
        carry_scr[...] = jnp.broadcast_to(carry, carry_scr.shape)
        for h in heads:
            acc_scr[h * DEC_ROWS:(h + 1) * DEC_ROWS, :] = accs[h]

    @pl.when(g == 0)
    def _():
        pad = jnp.zeros((page - DEC_ROWS, d), F32)
        k_heads = [jnp.concatenate([knew_ref[0, :, h * d:(h + 1) * d], pad], axis=0) for h in heads]
        v_heads = [jnp.concatenate([vnew_ref[0, :, h * d:(h + 1) * d], pad], axis=0) for h in heads]
        key = lax.broadcasted_iota(jnp.int32, (page, LANES), 0)
        qry = lax.broadcasted_iota(jnp.int32, (page, LANES), 1) & (DEC_ROWS - 1)
        store(*block(k_heads, v_heads, jnp.zeros((1, LANES), F32),
                     [jnp.zeros((DEC_ROWS, d), F32) for _ in heads], key < qry))

    unit_pages = trit_ref.shape[0] // page
    n_units = n_pages_step // unit_pages
    pages_of = lambda n: range(n_pages_step - (n + 1) * unit_pages, n_pages_step - n * unit_pages)
    head_rows = lambda ref, h: ref[0, pl.ds(h, page, stride=SB_HEADS), :]

    def scores(n):
        k_unit = jnp.concatenate([jnp.concatenate([head_rows(k_pages[u], h) for h in heads], axis=1)
                                  for u in pages_of(n)], axis=0)
        return _dot(_bf(k_unit), qbd) * (d ** -0.5 * LOG2E) + zbias

    def keep_logs(zs):
        sp = jnp.maximum(zs, 0.0) + jnp.log2(1.0 + jnp.exp2(-jnp.abs(zs)))
        return zs - sp, _dot(trit, _bf(sp)), jnp.sum(sp, axis=0, keepdims=True)

    def weights(n, logs, carry, accs):
        log_beta, later, total = logs
        wt = jnp.exp2(log_beta - later - carry)
        w = jnp.concatenate([jnp.transpose(wt[j * LANES:(j + 1) * LANES, :])
                             for j in range(wt.shape[0] // LANES)], axis=1)
        for h in heads:
            v_unit = jnp.concatenate([head_rows(v_pages[u], h) for u in pages_of(n)], axis=0)
            accs[h] = accs[h] + _dot(_bf(w[h * DEC_ROWS:(h + 1) * DEC_ROWS, :]), _bf(v_unit))
        return carry + total, accs

    carry = carry_scr[0:1, :]
    accs = [acc_scr[h * DEC_ROWS:(h + 1) * DEC_ROWS, :] for h in heads]
    zs, logs = {}, {}
    for m in range(n_units + 2):
        if m < n_units:
            zs[m] = scores(m)
        if 1 <= m <= n_units:
            logs[m - 1] = keep_logs(zs.pop(m - 1))
        if m >= 2:
            carry, accs = weights(m - 2, logs.pop(m - 2), carry, accs)
    store(carry, accs)

    @pl.when(g == pl.num_programs(1) - 1)
    def _():
        for h in heads:
            o_ref[0, :, h * d:(h + 1) * d] = acc_scr[h * DEC_ROWS:(h + 1) * DEC_ROWS, :]


def _sb_decode(sq, sk_new, sv_new, cache_k, cache_v, page_table, bias):
    b = sq.shape[0]
    n_pages = page_table.shape[1]
    page = cache_k.shape[1]
    g_pages = min(SB_PAGES_PER_STEP, n_pages)
    n_steps = n_pages // g_pages
    d = HEAD_DIM_AB
    q4 = sq.reshape(b, DEC_ROWS, SB_HEADS, d)
    eye = jnp.eye(SB_HEADS, dtype=F32)
    qbd = jnp.einsum('bihd,hg->bhdgi', q4, eye).reshape(b, SB_W, SB_HEADS * DEC_ROWS)
    qbd = _bf(jnp.pad(qbd, ((0, 0), (0, 0), (0, LANES - SB_HEADS * DEC_ROWS))))
    bias_lane = jnp.pad(jnp.repeat(bias.astype(F32), DEC_ROWS), (0, LANES - SB_HEADS * DEC_ROWS))[None, :]
    unit_keys = page * (2 if g_pages % 2 == 0 else 1)
    i = np.arange(unit_keys)
    trit = jnp.asarray((i[None, :] > i[:, None]).astype(np.float32), BF16)

    def page_spec(u):
        return pl.BlockSpec((1, page * SB_HEADS, d),
                            lambda bi, g, pt: (pt[bi * n_pages + (n_steps - 1 - g) * g_pages + u], 0, 0))

    per_seq = lambda r, w: pl.BlockSpec((1, r, w), lambda bi, g, pt: (bi, 0, 0))
    grid_spec = pltpu.PrefetchScalarGridSpec(
        num_scalar_prefetch=1,
        grid=(b, n_steps),
        in_specs=[per_seq(SB_W, LANES), pl.BlockSpec((1, LANES), lambda bi, g, pt: (0, 0)),
                  per_seq(DEC_ROWS, SB_W), per_seq(DEC_ROWS, SB_W),
                  pl.BlockSpec((unit_keys, unit_keys), lambda bi, g, pt: (0, 0))]
                 + [page_spec(u) for u in range(g_pages)] * 2,
        out_specs=per_seq(DEC_ROWS, SB_W),
        scratch_shapes=[pltpu.VMEM((SB_HEADS * DEC_ROWS, d), F32), pltpu.VMEM((DEC_ROWS, LANES), F32)],
    )
    return pl.pallas_call(
        functools.partial(_sb_decode_kernel, n_pages_step=g_pages, page=page),
        grid_spec=grid_spec,
        out_shape=jax.ShapeDtypeStruct((b, DEC_ROWS, SB_W), F32),
        compiler_params=_cparams(("parallel", "arbitrary")),
        name="sb_decode",
    )(page_table.reshape(-1), qbd, bias_lane, sk_new, sv_new, trit,
      *([cache_k.reshape(-1, page * SB_HEADS, d)] * g_pages), *([cache_v.reshape(-1, page * SB_HEADS, d)] * g_pages))


def _inproj_c_kernel(x_ref, g_ref, w_ref, q_ref, k_ref, v_ref):
    a = _rms(x_ref[...], g_ref[...])
    p = _dot(_bf(a), w_ref[...])
    q_ref[...] = p[:, :DIL_W]
    k_ref[...] = p[:, DIL_W:2 * DIL_W]
    v_ref[...] = p[:, 2 * DIL_W:]


def _inproj_c(x, gain, w_bf):
    m, dm = x.shape
    tm = min(m, 256)
    row = pl.BlockSpec((tm, DIL_W), lambda i: (i, 0))
    return pl.pallas_call(
        _inproj_c_kernel,
        grid=(m // tm,),
        in_specs=[pl.BlockSpec((tm, dm), lambda i: (i, 0)), pl.BlockSpec((1, dm), lambda i: (0, 0)),
                  pl.BlockSpec((dm, 3 * DIL_W), lambda i: (0, 0))],
        out_specs=[row] * 3,
        out_shape=[jax.ShapeDtypeStruct((m, DIL_W), F32)] * 3,
        compiler_params=_cparams(("parallel",)),
        name="inproj_c",
    )(x, gain, w_bf)


DIL_SUPER = DIL_WINDOW_MAX
DIL_SKEW = 2


def _dil_prompt_kernel(q_ref, kp_ref, kc_ref, vp_ref, vc_ref, o_ref, kwin, vwin, o_scr, lse_scr):
    sb = pl.program_id(0)
    blk = DIL_SPAN
    kwin[:DIL_SUPER, :] = kp_ref[...]
    kwin[DIL_SUPER:, :] = kc_ref[...]
    vwin[:DIL_SUPER, :] = vp_ref[...]
    vwin[DIL_SUPER:, :] = vc_ref[...]
    rows = lax.broadcasted_iota(jnp.int32, (blk, 2 * blk), 0)
    cols = lax.broadcasted_iota(jnp.int32, (blk, 2 * blk), 1)
    dist = blk + rows - cols
    in_span = (dist >= 0) & (dist <= DIL_SPAN)
    in_span_first = in_span & ((cols >= blk) | (sb > 0))
    lane = lax.broadcasted_iota(jnp.int32, (blk, LANES), 1)
    first = lane < DIL_HEAD_DIM
    scale = DIL_HEAD_DIM ** -0.5
    units = [(g, dil, r, j) for g, (_, dil) in enumerate(DIL_BRANCHES)
             for j in range(DIL_SUPER // (dil * blk)) for r in range(dil)]

    def scores(unit):
        g, dil, r, j = unit
        q = q_ref[pl.ds(r + dil * blk * j, blk, stride=dil), :]
        k2 = _bf(kwin[pl.ds(DIL_SUPER + r + dil * blk * (j - 1), 2 * blk, stride=dil), :])
        zero = jnp.zeros_like(q)
        return [lax.dot_general(_bf(qh), k2, _NT, preferred_element_type=F32) * scale
                for qh in (jnp.where(first, q, zero), jnp.where(first, zero, q))]

    def attend(unit, zs):
        g, dil, r, j = unit
        v2 = _bf(vwin[pl.ds(DIL_SUPER + r + dil * blk * (j - 1), 2 * blk, stride=dil), :])
        valid = in_span_first if j == 0 else in_span
        outs, lses = [], []
        for z in zs:
            z = jnp.where(valid, z, -jnp.inf)
            m = jnp.max(z, axis=-1, keepdims=True)
            e = jnp.exp(z - m)
            den = jnp.sum(e, axis=-1, keepdims=True)
            outs.append(_dot(_bf(e), v2) / den)
            lses.append(m + jnp.log(den))
        dst = pl.ds(r + dil * blk * j, blk, stride=dil)
        o_scr[g, dst, :] = jnp.where(first, outs[0], outs[1])
        lse_scr[g, dst, :] = jnp.where(first, lses[0], lses[1])

    pending = {}
    for n in range(len(units) + DIL_SKEW):
        if n < len(units):
            pending[n] = scores(units[n])
        if n >= DIL_SKEW:
            attend(units[n - DIL_SKEW], pending.pop(n - DIL_SKEW))

    n_br = len(DIL_BRANCHES)
    for c in range(DIL_SUPER // blk):
        rs = slice(c * blk, (c + 1) * blk)
        lses = [lse_scr[g, rs, :] for g in range(n_br)]
        top = functools.reduce(jnp.maximum, lses)
        ws = [jnp.exp(l - top) for l in lses]
        num = functools.reduce(lambda a, b: a + b, [w * o_scr[g, rs, :] for g, w in enumerate(ws)])
        o_ref[rs, :] = (num / functools.reduce(lambda a, b: a + b, ws)).astype(o_ref.dtype)


def _dil_prompt(q, k, v):
    t = q.shape[0]
    assert t % DIL_SUPER == 0
    cur = pl.BlockSpec((DIL_SUPER, LANES), lambda s, hp: (s, hp))
    prev = pl.BlockSpec((DIL_SUPER, LANES), lambda s, hp: (jnp.maximum(s - 1, 0), hp))
    n_br = len(DIL_BRANCHES)
    return pl.pallas_call(
        _dil_prompt_kernel,
        grid=(t // DIL_SUPER, DIL_W // LANES),
        in_specs=[cur, prev, cur, prev, cur],
        out_specs=cur,
        out_shape=jax.ShapeDtypeStruct((t, DIL_W), BF16),
        scratch_shapes=[pltpu.VMEM((2 * DIL_SUPER, LANES), F32), pltpu.VMEM((2 * DIL_SUPER, LANES), F32),
                        pltpu.VMEM((n_br, DIL_SUPER, LANES), F32), pltpu.VMEM((n_br, DIL_SUPER, LANES), F32)],
        compiler_params=_cparams(("parallel", "parallel")),
        name="dilated_prompt",
    )(q, k, k, v, v)


def _dil_multiplicity(dist):
    cnt = jnp.zeros(dist.shape, F32)
    for window, dil in DIL_BRANCHES:
        hit = (dist >= 0) & (dist <= window) & ((dist & (dil - 1)) == 0)
        cnt = cnt + jnp.where(hit, 1.0, 0.0)
    return cnt


def _dil_decode_kernel(q_ref, kt_ref, vt_ref, knew_ref, vnew_ref, o_ref, kout_ref, vout_ref, *, n_past, n_new):
    scale = DIL_HEAD_DIM ** -0.5
    heads = range(kt_ref.shape[1])
    qi = lax.broadcasted_iota(jnp.int32, (DEC_ROWS, n_past), 0)
    key = lax.broadcasted_iota(jnp.int32, (DEC_ROWS, n_past), 1)
    cnt_past = _dil_multiplicity(n_past + qi - key)
    qi_new = lax.broadcasted_iota(jnp.int32, (DEC_ROWS, LANES), 0)
    lane = lax.broadcasted_iota(jnp.int32, (DEC_ROWS, LANES), 1)
    j_new = lane - (LANES - n_new)
    cnt_new = jnp.where(j_new >= 0, _dil_multiplicity(qi_new - j_new), 0.0)

    qs = [_bf(q_ref[0, h]) for h in heads]
    z_past = [_dot(qs[h], _bf(kt_ref[0, h])) * scale for h in heads]
    z_new = [_dot(qs[h], _bf(knew_ref[0, h])) * scale for h in heads]
    ps = []
    for h in heads:
        m = jnp.maximum(jnp.max(jnp.where(cnt_past > 0, z_past[h], -jnp.inf), axis=1, keepdims=True),
                        jnp.max(jnp.where(cnt_new > 0, z_new[h], -jnp.inf), axis=1, keepdims=True))
        p_past = jnp.where(cnt_past > 0, cnt_past * jnp.exp(z_past[h] - m), 0.0)
        p_new = jnp.where(cnt_new > 0, cnt_new * jnp.exp(z_new[h] - m), 0.0)
        den = jnp.sum(p_past, axis=1, keepdims=True) + jnp.sum(p_new, axis=1, keepdims=True)
        ps.append((p_past, p_new, den))
    for h in heads:
        p_past, p_new, den = ps[h]
        num = (lax.dot_general(_bf(p_past), _bf(vt_ref[0, h]), _NT, preferred_element_type=F32)
               + lax.dot_general(_bf(p_new), _bf(vnew_ref[0, h]), _NT, preferred_element_type=F32))
        o_ref[0, h] = num / den

    tail = lax.broadcasted_iota(jnp.int32, (DIL_HEAD_DIM, LANES), 1) >= LANES - n_new
    for src, new, dst in ((kt_ref, knew_ref, kout_ref), (vt_ref, vnew_ref, vout_ref)):
        for h in heads:
            shifted = pltpu.roll(src[0, h], n_past - n_new, 1)
            dst[0, h, :, :n_past - LANES] = shifted[:, :n_past - LANES]
            dst[0, h, :, n_past - LANES:] = jnp.where(tail, new[0, h], shifted[:, n_past - LANES:])


DIL_DEC_HEADS_PER_STEP = 4


def _dil_decode(q, k_new, v_new, cache_k, cache_v):
    b, n_new, _ = q.shape
    n_past = cache_k.shape[1]
    assert n_past == DIL_WINDOW_MAX and n_new <= DEC_ROWS
    hd = (DIL_HEADS, DIL_HEAD_DIM)
    to_t = lambda a: jnp.transpose(a, (0, 2, 3, 1))
    new_t = lambda a: jnp.pad(to_t(a.reshape(b, n_new, *hd)), ((0, 0), (0, 0), (0, 0), (LANES - n_new, 0)))
    q_rows = jnp.pad(jnp.transpose(q.reshape(b, n_new, *hd), (0, 2, 1, 3)),
                     ((0, 0), (0, 0), (0, DEC_ROWS - n_new), (0, 0)))
    hg = DIL_DEC_HEADS_PER_STEP
    spec = lambda r, w: pl.BlockSpec((1, hg, r, w), lambda bi, g: (bi, g, 0, 0))
    cache_shape = jax.ShapeDtypeStruct((b, DIL_HEADS, DIL_HEAD_DIM, n_past), F32)
    o, k_out, v_out = pl.pallas_call(
        functools.partial(_dil_decode_kernel, n_past=n_past, n_new=n_new),
        grid=(b, DIL_HEADS // hg),
        in_specs=[spec(DEC_ROWS, DIL_HEAD_DIM), spec(DIL_HEAD_DIM, n_past), spec(DIL_HEAD_DIM, n_past),
                  spec(DIL_HEAD_DIM, LANES), spec(DIL_HEAD_DIM, LANES)],
        out_specs=[spec(DEC_ROWS, DIL_HEAD_DIM), spec(DIL_HEAD_DIM, n_past), spec(DIL_HEAD_DIM, n_past)],
        out_shape=[jax.ShapeDtypeStruct((b, DIL_HEADS, DEC_ROWS, DIL_HEAD_DIM), F32), cache_shape, cache_shape],
        compiler_params=_cparams(("parallel", "parallel")),
        name="dilated_decode",
    )(q_rows, to_t(cache_k), to_t(cache_v), new_t(k_new), new_t(v_new))
    from_t = lambda a: jnp.transpose(a, (0, 3, 1, 2))
    o = jnp.transpose(o[:, :, :n_new, :], (0, 2, 1, 3)).reshape(b, n_new, DIL_W)
    return o, from_t(k_out), from_t(v_out)


def _mix_concat(*refs):
    return jnp.concatenate([_bf(r[...]) for r in refs], axis=-1)


def _tail_kernel(*refs, n_mix, mix_fn):
    mix_refs = refs[:n_mix]
    (h_ref, wout_ref, gpost_ref, gpre_ref, w1_ref, w2_ref, gffn_ref,
     out_ref, h1_scr, a_scr, acc_scr) = refs[n_mix:]
    f = pl.program_id(1)

    @pl.when(f == 0)
    def _():
        m = _dot(mix_fn(*mix_refs), wout_ref[...])
        h1 = h_ref[...] + _rms(m, gpost_ref[...])
        h1_scr[...] = h1
        a_scr[...] = _bf(_rms(h1, gpre_ref[...]))
        acc_scr[...] = jnp.zeros_like(acc_scr)

    hid = jnp.square(jnp.maximum(_dot(a_scr[...], w1_ref[...]), 0.0))
    acc_scr[...] += _dot(_bf(hid), w2_ref[...])

    @pl.when(f == pl.num_programs(1) - 1)
    def _():
        out_ref[...] = h1_scr[...] + _rms(acc_scr[...], gffn_ref[...])


def _layer_tail(mix, mix_fn, h, w_out, g_post, g_ffn_pre, w1, w2, g_ffn_post, tm):
    m, dm = h.shape
    dff = w1.shape[1]
    tm = min(m, tm)
    tf = min(dff, 1024)
    row = lambda w: pl.BlockSpec((tm, w), lambda i, f: (i, 0))
    gain = pl.BlockSpec((1, dm), lambda i, f: (0, 0))
    return pl.pallas_call(
        functools.partial(_tail_kernel, n_mix=len(mix), mix_fn=mix_fn),
        grid=(m // tm, dff // tf),
        in_specs=[row(a.shape[1]) for a in mix]
                 + [row(dm), pl.BlockSpec(w_out.shape, lambda i, f: (0, 0)), gain, gain,
                    pl.BlockSpec((dm, tf), lambda i, f: (0, f)), pl.BlockSpec((tf, dm), lambda i, f: (f, 0)), gain],
        out_specs=row(dm),
        out_shape=jax.ShapeDtypeStruct((m, dm), F32),
        scratch_shapes=[pltpu.VMEM((tm, dm), F32), pltpu.VMEM((tm, dm), BF16), pltpu.VMEM((tm, dm), F32)],
        compiler_params=_cparams(("parallel", "arbitrary")),
        name="layer_tail",
    )(*mix, h, w_out, g_post, g_ffn_pre, w1, w2, g_ffn_post)


def _rope_tables(pos):
    half = HEAD_DIM_AB // 2
    inv_freq = jnp.power(ROPE_BASE, -jnp.linspace(0.0, 1.0, half, dtype=F32))
    ang = pos.astype(F32)[:, None] * inv_freq[None, :]
    cos, sin = jnp.cos(ang), jnp.sin(ang)
    return jnp.concatenate([cos, cos], axis=-1), jnp.concatenate([-sin, sin], axis=-1)


def _pad_rows(a, rows):
    return jnp.pad(a, ((0, 0), (0, rows - a.shape[1]), (0, 0)))


def kernel(x_prompt, x_sample, cache_sb_k, cache_sb_v, state_ret, cache_swa_k, cache_swa_v, page_table,
           w_in_ab, w_out_ab, sb_bias, w_in_c, w_out_c, w_ff1, w_ff2, g_mix_pre, g_mix_post, g_ffn_pre, g_ffn_post):
    bp, t, dm = x_prompt.shape
    bs, ts, _ = x_sample.shape
    assert bp == 1 and ts <= DEC_ROWS and t % RET_CHUNK == 0
    n_pages = page_table.shape[1]
    page = cache_sb_k.shape[1]
    past_len = n_pages * page
    d = HEAD_DIM_AB
    gain = lambda g, layer: g[layer][None, :].astype(F32)
    w_in_ab_bf, w_out_ab_bf, w_in_c_bf, w_out_c_bf = _bf(w_in_ab), _bf(w_out_ab), _bf(w_in_c), _bf(w_out_c)
    w_ff1_bf, w_ff2_bf = _bf(w_ff1), _bf(w_ff2)

    h_p = x_prompt.reshape(t, dm)
    h_s = x_sample.reshape(bs * ts, dm)

    cos_p, sin_p = _rope_tables(jnp.arange(t, dtype=jnp.int32))
    rq, rk, rv, rg, q_hm, k_hm, v_hm, sb_k_p, sb_v_p = _inproj_ab(h_p, gain(g_mix_pre, 0), w_in_ab_bf, cos_p, sin_p)
    seq = lambda a: a.reshape(1, t, RET_W)
    ret_zero = jnp.zeros((1, RET_HEADS, d, d), F32)
    y_ret_p, ret_p = _retention(seq(rq), seq(rk), seq(rv), seq(rg), ret_zero, float(RET_CHUNK))
    o_sb_p = _sb_prompt(q_hm, k_hm, v_hm, sb_bias.astype(F32))
    h_p = _layer_tail([y_ret_p.reshape(t, RET_W), o_sb_p], _mix_concat, h_p, w_out_ab_bf, gain(g_mix_post, 0),
                      gain(g_ffn_pre, 0), w_ff1_bf[0], w_ff2_bf[0], gain(g_ffn_post, 0), tm=512)

    cos_s, sin_s = _rope_tables(past_len + jnp.arange(ts, dtype=jnp.int32))
    tile_s = lambda a: jnp.tile(a, (bs, 1))
    rq, rk, rv, rg, q_hm, _, _, sb_k_s, sb_v_s = _inproj_ab(h_s, gain(g_mix_pre, 0), w_in_ab_bf,
                                                            tile_s(cos_s), tile_s(sin_s))
    dec = lambda a: _pad_rows(a.reshape(bs, ts, a.shape[-1]), DEC_ROWS)
    y_ret_s, ret_s = _retention(dec(rq), dec(rk), dec(rv), dec(rg), state_ret.astype(F32), float(ts))
    sq_s = jnp.transpose(q_hm.astype(F32), (1, 0, 2)).reshape(bs * ts, SB_W)
    o_sb_s = _sb_decode(dec(sq_s), dec(sb_k_s.reshape(bs * ts, SB_W)), dec(sb_v_s.reshape(bs * ts, SB_W)),
                        cache_sb_k, cache_sb_v, page_table, sb_bias.astype(F32))
    undec = lambda a: a[:, :ts].reshape(bs * ts, a.shape[-1])
    h_s = _layer_tail([undec(y_ret_s), undec(o_sb_s)], _mix_concat, h_s, w_out_ab_bf, gain(g_mix_post, 0),
                      gain(g_ffn_pre, 0), w_ff1_bf[0], w_ff2_bf[0], gain(g_ffn_post, 0), tm=512)

    q, k_f32, v_f32 = _inproj_c(h_p, gain(g_mix_pre, 1), w_in_c_bf)
    h_p = _layer_tail([_dil_prompt(q, k_f32, v_f32)], _mix_concat, h_p, w_out_c_bf, gain(g_mix_post, 1),
                      gain(g_ffn_pre, 1), w_ff1_bf[1], w_ff2_bf[1], gain(g_ffn_post, 1), tm=512)
    keep_p = min(DIL_WINDOW_MAX, t)
    swa_k_p = k_f32[t - keep_p:].reshape(1, keep_p, DIL_HEADS, DIL_HEAD_DIM)
    swa_v_p = v_f32[t - keep_p:].reshape(1, keep_p, DIL_HEADS, DIL_HEAD_DIM)

    q, k_f32, v_f32 = _inproj_c(h_s, gain(g_mix_pre, 1), w_in_c_bf)
    per_seq = lambda a: a.reshape(bs, ts, DIL_W)
    o_dil_s, swa_k_s, swa_v_s = _dil_decode(per_seq(q.astype(F32)), per_seq(k_f32), per_seq(v_f32),
                                            cache_swa_k.astype(F32), cache_swa_v.astype(F32))
    h_s = _layer_tail([o_dil_s.reshape(bs * ts, DIL_W)], _mix_concat, h_s, w_out_c_bf, gain(g_mix_post, 1),
                      gain(g_ffn_pre, 1), w_ff1_bf[1], w_ff2_bf[1], gain(g_ffn_post, 1), tm=512)

    heads_ab = lambda a, b: a.reshape(b, -1, SB_HEADS, d)
    return (h_p.reshape(1, t, dm), h_s.reshape(bs, ts, dm),
            heads_ab(sb_k_p, 1), heads_ab(sb_v_p, 1), ret_p,
            swa_k_p, swa_v_p,
            heads_ab(sb_k_s, bs), heads_ab(sb_v_s, bs), ret_s,
            swa_k_s, swa_v_s)
```

```python
import functools

import numpy as np
import jax
import jax.numpy as jnp
from jax import lax
from jax.experimental import pallas as pl
from jax.experimental.pallas import tpu as pltpu

F32 = jnp.float32
BF16 = jnp.bfloat16

NORM_EPS = 1e-6
ROPE_BASE = 10000.0
RET_HEADS = 4
SB_HEADS = 4
HEAD_DIM_AB = 128
RET_W = RET_HEADS * HEAD_DIM_AB
SB_W = SB_HEADS * HEAD_DIM_AB
RET_CHUNK = 128
DIL_HEADS = 16
DIL_HEAD_DIM = 64
DIL_W = DIL_HEADS * DIL_HEAD_DIM
DIL_BRANCHES = ((128, 1), (512, 4), (2048, 16))
DIL_WINDOW_MAX = 2048
DIL_SPAN = 128
LANES = 128
DEC_ROWS = 8
INPROJ_ROWS = 512
VMEM_LIMIT = 56 * 1024 * 1024

_NT = (((1,), (1,)), ((), ()))
_TN = (((0,), (0,)), ((), ()))


def _cparams(sem):
    return pltpu.CompilerParams(dimension_semantics=sem, vmem_limit_bytes=VMEM_LIMIT)


def _bf(x):
    return x.astype(BF16)


def _dot(a, b):
    return jnp.dot(a, b, preferred_element_type=F32)


def _rms(x, gain):
    return x * lax.rsqrt(jnp.mean(x * x, axis=-1, keepdims=True) + NORM_EPS) * gain


def _softplus(z):
    return jnp.maximum(z, 0.0) + jnp.log1p(jnp.exp(-jnp.abs(z)))


def _inproj_ab_kernel(x_ref, g_ref, w_ref, cos_ref, sin_ref,
                      rq_ref, rk_ref, rv_ref, rg_ref, q_hm, k_hm, v_hm, sbk_ref, sbv_ref):
    a = _rms(x_ref[...], g_ref[...])
    p = _dot(_bf(a), w_ref[...])
    cosf = cos_ref[...]
    sinf = sin_ref[...]
    d = HEAD_DIM_AB
    for h in range(RET_HEADS):
        qh = p[:, h * d:(h + 1) * d]
        kh = p[:, RET_W + h * d:RET_W + (h + 1) * d]
        rq_ref[:, h * d:(h + 1) * d] = qh * cosf + pltpu.roll(qh, d // 2, 1) * sinf
        rk_ref[:, h * d:(h + 1) * d] = (kh * cosf + pltpu.roll(kh, d // 2, 1) * sinf) * (d ** -0.5)
    rv_ref[...] = p[:, 2 * RET_W:3 * RET_W]
    rg_ref[...] = p[:, 3 * RET_W:4 * RET_W]
    base = 4 * RET_W
    for h in range(SB_HEADS):
        q_hm[h] = _bf(p[:, base + h * d:base + (h + 1) * d] * (d ** -0.5 * LOG2E))
        k_hm[h] = _bf(p[:, base + SB_W + h * d:base + SB_W + (h + 1) * d])
        v_hm[h] = _bf(p[:, base + 2 * SB_W + h * d:base + 2 * SB_W + (h + 1) * d])
    tm = p.shape[0]
    for h in range(SB_HEADS):
        sbk_ref[pl.ds(h, tm, stride=SB_HEADS), :] = p[:, base + SB_W + h * d:base + SB_W + (h + 1) * d]
        sbv_ref[pl.ds(h, tm, stride=SB_HEADS), :] = p[:, base + 2 * SB_W + h * d:base + 2 * SB_W + (h + 1) * d]


def _inproj_ab(x, gain, w_bf, cosf, sinf):
    m, dm = x.shape
    n = w_bf.shape[1]
    tm = min(m, INPROJ_ROWS)
    row = lambda w: pl.BlockSpec((tm, w), lambda i: (i, 0))
    hm = pl.BlockSpec((SB_HEADS, tm, HEAD_DIM_AB), lambda i: (0, i, 0))
    f32_rows = jax.ShapeDtypeStruct((m, RET_W), F32)
    hm_shape = jax.ShapeDtypeStruct((SB_HEADS, m, HEAD_DIM_AB), BF16)
    return pl.pallas_call(
        _inproj_ab_kernel,
        grid=(m // tm,),
        in_specs=[row(dm), pl.BlockSpec((1, dm), lambda i: (0, 0)),
                  pl.BlockSpec((dm, n), lambda i: (0, 0)), row(HEAD_DIM_AB), row(HEAD_DIM_AB)],
        out_specs=[row(RET_W)] * 4 + [hm] * 3 + [pl.BlockSpec((tm * SB_HEADS, HEAD_DIM_AB), lambda i: (i, 0))] * 2,
        out_shape=[f32_rows] * 4 + [hm_shape] * 3 + [jax.ShapeDtypeStruct((m * SB_HEADS, HEAD_DIM_AB), F32)] * 2,
        compiler_params=_cparams(("parallel",)),
        name="inproj_ab",
    )(x, gain, w_bf, cosf, sinf)


def _ret_kernel(rq, rk, rv, rg, s0, intra, qdec, kdec, cdec, y_ref, s_out, s_scr, *, rows, n_chunks):
    step = pl.program_id(1)
    d = HEAD_DIM_AB

    @pl.when(step == 0)
    def _():
        s_scr[...] = s0[0]

    for c in range(n_chunks):
        rs = slice(c * rows, (c + 1) * rows)
        for h in range(RET_HEADS):
            cs = slice(h * d, (h + 1) * d)
            q, k, v, g = rq[0, rs, cs], rk[0, rs, cs], rv[0, rs, cs], rg[0, rs, cs]
            if rows < RET_CHUNK:
                pad = jnp.zeros((RET_CHUNK - rows, d), F32)
                k = jnp.concatenate([k, pad], axis=0)
                v = jnp.concatenate([v, pad], axis=0)
            s_prev = s_scr[h]
            att = lax.dot_general(_bf(q), _bf(k), _NT, preferred_element_type=F32) * intra[h, :rows, :]
            o = _dot(_bf(att), _bf(v)) + _dot(_bf(q * qdec[h, :rows, :]), _bf(s_prev))
            kd = k * kdec[h]
            s_scr[h] = s_prev * cdec[h] + lax.dot_general(_bf(kd), _bf(v), _TN, preferred_element_type=F32)
            xc = o - jnp.mean(o, axis=-1, keepdims=True)
            var = jnp.mean(xc * xc, axis=-1, keepdims=True)
            y_ref[0, rs, cs] = xc * lax.rsqrt(var + NORM_EPS) * (g * jax.nn.sigmoid(g))

    @pl.when(step == pl.num_programs(1) - 1)
    def _():
        s_out[0] = s_scr[...]


def _ret_tables(chunk_len):
    lg = jnp.log1p(-jnp.exp2(-5.0 - jnp.arange(RET_HEADS, dtype=F32)))
    i = jnp.arange(RET_CHUNK, dtype=F32)
    live = i < chunk_len
    rel = i[:, None] - i[None, :]
    intra = jnp.where((rel >= 0) & live[:, None] & live[None, :],
                      jnp.exp(lg[:, None, None] * jnp.maximum(rel, 0.0)), 0.0)
    q_dec = jnp.where(live[None, :], jnp.exp(lg[:, None] * (i[None, :] + 1.0)), 0.0)
    k_dec = jnp.where(live[None, :], jnp.exp(lg[:, None] * (chunk_len - 1.0 - i[None, :])), 0.0)
    c_dec = jnp.exp(lg * chunk_len)
    full = lambda t: jnp.broadcast_to(t[:, :, None], (RET_HEADS, RET_CHUNK, LANES))
    return intra, full(q_dec), full(k_dec), jnp.broadcast_to(c_dec[:, None, None], (RET_HEADS, RET_CHUNK, LANES))


def _retention(rq, rk, rv, rg, s0, chunk_len):
    b, t, _ = rq.shape
    if t % RET_CHUNK == 0:
        rows = RET_CHUNK
        n_chunks = min(8, t // rows)
    else:
        rows, n_chunks = t, 1
    tc = rows * n_chunks
    seq = pl.BlockSpec((1, tc, RET_W), lambda bi, s: (bi, s, 0))
    st = pl.BlockSpec((1, RET_HEADS, HEAD_DIM_AB, HEAD_DIM_AB), lambda bi, s: (bi, 0, 0, 0))
    tab = pl.BlockSpec((RET_HEADS, RET_CHUNK, LANES), lambda bi, s: (0, 0, 0))
    return pl.pallas_call(
        functools.partial(_ret_kernel, rows=rows, n_chunks=n_chunks),
        grid=(b, t // tc),
        in_specs=[seq] * 4 + [st] + [tab] * 4,
        out_specs=[seq, st],
        out_shape=[jax.ShapeDtypeStruct((b, t, RET_W), F32), jax.ShapeDtypeStruct(s0.shape, F32)],
        scratch_shapes=[pltpu.VMEM((RET_HEADS, HEAD_DIM_AB, HEAD_DIM_AB), F32)],
        compiler_params=_cparams(("parallel", "arbitrary")),
        name="retention",
    )(rq, rk, rv, rg, s0, *_ret_tables(chunk_len))


SB_SUB = 256
SB_ROWS = 128
LOG2E = 1.4426950408889634
SB_DECAYED = 152.0


def _sb_tile(q_ref, k_ref, v_ref, tri_ref, zbias, u_scr, sp_scr, w_scr, carry_scr, acc_scr, units, diagonal):
    d = HEAD_DIM_AB
    n_chunks = SB_SUB // SB_ROWS
    keys_of = lambda c: pl.ds(c * SB_SUB, SB_SUB)
    rows_of = lambda g: pl.ds(g * SB_SUB, SB_SUB)
    chunk_of = lambda g, r: pl.ds(g * SB_SUB + r * SB_ROWS, SB_ROWS)
    in_slot = lambda r: pl.ds(r * SB_ROWS, SB_ROWS)
    key = lax.broadcasted_iota(jnp.int32, (SB_ROWS, SB_SUB), 1)
    row = lax.broadcasted_iota(jnp.int32, (SB_ROWS, SB_SUB), 0)

    def mask_of(unit, r):
        c, g = unit
        return key < row + r * SB_ROWS if diagonal and g == c else None

    def scores(unit):
        c, g = unit
        return lax.dot_general(q_ref[0, rows_of(g), :], k_ref[0, keys_of(c), :], _NT, preferred_element_type=F32)

    def keep_logs(n, s):
        c, g = units[n]
        slot = n % 2
        for r in range(n_chunks):
            zs = s[r * SB_ROWS:(r + 1) * SB_ROWS, :] + zbias
            sp = jnp.maximum(zs, 0.0) + jnp.log2(1.0 + jnp.exp2(-jnp.abs(zs)))
            carry = carry_scr[chunk_of(g, r), :]
            u_scr[slot, in_slot(r), :] = zs - sp - jnp.tile(carry, (1, SB_SUB // LANES))
            mask = mask_of(units[n], r)
            if mask is not None:
                sp = jnp.where(mask, sp, 0.0)
            sp_scr[slot, in_slot(r), :] = _bf(sp)
            carry_scr[chunk_of(g, r), :] = carry + jnp.sum(sp, axis=1, keepdims=True)
        return _dot(sp_scr[slot], tri_ref[...])

    def weights(n, sums):
        c, g = units[n]
        slot = n % 2
        for r in range(n_chunks):
            w = jnp.exp2(u_scr[slot, in_slot(r), :] - sums[r * SB_ROWS:(r + 1) * SB_ROWS, :])
            mask = mask_of(units[n], r)
            if mask is not None:
                w = jnp.where(mask, w, 0.0)
            w_scr[slot, in_slot(r), :] = _bf(w)
        acc_scr[rows_of(g), :] += _dot(w_scr[slot], v_ref[0, keys_of(c), :])

    s, sums = {}, {}
    for n in range(len(units) + 2):
        if n < len(units):
            s[n] = scores(units[n])
        if 1 <= n <= len(units):
            sums[n - 1] = keep_logs(n - 1, s.pop(n - 1))
        if n >= 2:
            weights(n - 2, sums.pop(n - 2))


def _sb_prompt_kernel(qi_tab, kj_tab, bias_ref, q_ref, k_ref, v_ref, tri_ref, o_ref,
                      acc_scr, carry_scr, u_scr, sp_scr, w_scr, least_scr, *, tq):
    h = pl.program_id(0)
    p = pl.program_id(1)
    qi = qi_tab[p]
    kj = kj_tab[p]
    zbias = bias_ref[h] * LOG2E
    n_sub = tq // SB_SUB
    tile = functools.partial(_sb_tile, q_ref, k_ref, v_ref, tri_ref, zbias, u_scr, sp_scr, w_scr, carry_scr, acc_scr)

    @pl.when(kj == qi)
    def _():
        acc_scr[...] = jnp.zeros_like(acc_scr)
        carry_scr[...] = jnp.zeros_like(carry_scr)
        tile([(c, g) for c in reversed(range(n_sub)) for g in range(c, n_sub)], True)
        least_scr[0] = 0.0

    @pl.when((kj < qi) & (least_scr[0] <= SB_DECAYED))
    def _():
        tile([(c, g) for c in reversed(range(n_sub)) for g in range(n_sub)], False)
        least = carry_scr[...]
        while least.shape[0] > 8:
            half = least.shape[0] // 2
            least = jnp.minimum(least[:half], least[half:])
        least_scr[0] = jnp.min(least)

    @pl.when(kj == 0)
    def _():
        o_ref[...] = acc_scr[...].astype(o_ref.dtype)


def _tri_ones(n):
    i = np.arange(n)
    return jnp.asarray((i[:, None] > i[None, :]).astype(np.float32), BF16)


def _sb_prompt(q_hm, k_hm, v_hm, bias):
    nh, t, d = q_hm.shape
    tq = min(1024, t)
    nq = t // tq
    qi_tab = np.concatenate([np.full(i + 1, i) for i in range(nq)]).astype(np.int32)
    kj_tab = np.concatenate([np.arange(i, -1, -1) for i in range(nq)]).astype(np.int32)
    grid_spec = pltpu.PrefetchScalarGridSpec(
        num_scalar_prefetch=2,
        grid=(nh, len(qi_tab)),
        in_specs=[
            pl.BlockSpec(memory_space=pltpu.SMEM),
            pl.BlockSpec((1, tq, d), lambda h, p, qt, kt: (h, qt[p], 0)),
            pl.BlockSpec((1, tq, d), lambda h, p, qt, kt: (h, kt[p], 0)),
            pl.BlockSpec((1, tq, d), lambda h, p, qt, kt: (h, kt[p], 0)),
            pl.BlockSpec((SB_SUB, SB_SUB), lambda h, p, qt, kt: (0, 0)),
        ],
        out_specs=pl.BlockSpec((tq, d), lambda h, p, qt, kt: (qt[p], h)),
        scratch_shapes=[pltpu.VMEM((tq, d), F32), pltpu.VMEM((tq, LANES), F32), pltpu.VMEM((2, SB_SUB, SB_SUB), F32),
                        pltpu.VMEM((2, SB_SUB, SB_SUB), BF16), pltpu.VMEM((2, SB_SUB, SB_SUB), BF16),
                        pltpu.SMEM((1,), F32)],
    )
    return pl.pallas_call(
        functools.partial(_sb_prompt_kernel, tq=tq),
        grid_spec=grid_spec,
        out_shape=jax.ShapeDtypeStruct((t, nh * d), BF16),
        compiler_params=_cparams(("parallel", "arbitrary")),
        name="sb_prompt",
    )(jnp.asarray(qi_tab), jnp.asarray(kj_tab), bias, q_hm, k_hm, v_hm, _tri_ones(SB_SUB))


SB_PAGES_PER_STEP = 16


def _sb_decode_kernel(pt_ref, qbd_ref, bias_ref, knew_ref, vnew_ref, trit_ref, *rest, n_pages_step, page):
    k_pages = rest[:n_pages_step]
    v_pages = rest[n_pages_step:2 * n_pages_step]
    o_ref, acc_scr, carry_scr = rest[2 * n_pages_step:]
    g = pl.program_id(1)
    qbd = qbd_ref[0]
    zbias = bias_ref[...] * LOG2E
    trit = trit_ref[...]
    d = HEAD_DIM_AB
    heads = range(SB_HEADS)

    def block(k_heads, v_heads, carry, accs, mask):
        s = functools.reduce(lambda a, c: a + c,
                             [_dot(_bf(k_heads[h]), qbd[h * d:(h + 1) * d, :]) for h in heads])
        zs = s + zbias
        l2 = jnp.log2(1.0 + jnp.exp2(-jnp.abs(zs)))
        sp = jnp.maximum(zs, 0.0) + l2
        if mask is not None:
            sp = jnp.where(mask, sp, 0.0)
        later = _dot(trit[:page, :page], _bf(sp))
        wt = jnp.exp2(jnp.minimum(zs, 0.0) - l2 - later - carry)
        if mask is not None:
            wt = jnp.where(mask, wt, 0.0)
        w = jnp.transpose(wt)
        accs = [accs[h] + _dot(_bf(w[h * DEC_ROWS:(h + 1) * DEC_ROWS, :]), _bf(v_heads[h])) for h in heads]
        return carry + jnp.sum(sp, axis=0, keepdims=True), accs

    def store(carry, accs):
        carry_scr[...] = jnp.broadcast_to(carry, carry_scr.shape)
        for h in heads:
            acc_scr[h * DEC_ROWS:(h + 1) * DEC_ROWS, :] = accs[h]

    @pl.when(g == 0)
    def _():
        pad = jnp.zeros((page - DEC_ROWS, d), F32)
        k_heads = [jnp.concatenate([knew_ref[0, :, h * d:(h + 1) * d], pad], axis=0) for h in heads]
        v_heads = [jnp.concatenate([vnew_ref[0, :, h * d:(h + 1) * d], pad], axis=0) for h in heads]
        key = lax.broadcasted_iota(jnp.int32, (page, LANES), 0)
        qry = lax.broadcasted_iota(jnp.int32, (page, LANES), 1) & (DEC_ROWS - 1)
        store(*block(k_heads, v_heads, jnp.zeros((1, LANES), F32),
                     [jnp.zeros((DEC_ROWS, d), F32) for _ in heads], key < qry))

    unit_pages = trit_ref.shape[0] // page
    n_units = n_pages_step // unit_pages
    pages_of = lambda n: range(n_pages_step - (n + 1) * unit_pages, n_pages_step - n * unit_pages)
    head_rows = lambda ref, h: ref[0, pl.ds(h, page, stride=SB_HEADS), :]

    def scores(n):
        k_unit = jnp.concatenate([jnp.concatenate([head_rows(k_pages[u], h) for h in heads], axis=1)
                                  for u in pages_of(n)], axis=0)
        return _dot(_bf(k_unit), qbd) + zbias

    def keep_logs(zs):
        sp = jnp.maximum(zs, 0.0) + jnp.log2(1.0 + jnp.exp2(-jnp.abs(zs)))
        return zs - sp, _dot(trit, _bf(sp)), jnp.sum(sp, axis=0, keepdims=True)

    def weights(n, logs, carry, accs):
        log_beta, later, total = logs
        wt = jnp.exp2(log_beta - later - carry)
        w = jnp.concatenate([jnp.transpose(wt[j * LANES:(j + 1) * LANES, :])
                             for j in range(wt.shape[0] // LANES)], axis=1)
        for h in heads:
            v_unit = jnp.concatenate([head_rows(v_pages[u], h) for u in pages_of(n)], axis=0)
            accs[h] = accs[h] + _dot(_bf(w[h * DEC_ROWS:(h + 1) * DEC_ROWS, :]), _bf(v_unit))
        return carry + total, accs

    carry = carry_scr[0:1, :]
    accs = [acc_scr[h * DEC_ROWS:(h + 1) * DEC_ROWS, :] for h in heads]
    zs, logs = {}, {}
    for m in range(n_units + 2):
        if m < n_units:
            zs[m] = scores(m)
        if 1 <= m <= n_units:
            logs[m - 1] = keep_logs(zs.pop(m - 1))
        if m >= 2:
            carry, accs = weights(m - 2, logs.pop(m - 2), carry, accs)
    store(carry, accs)

    @pl.when(g == pl.num_programs(1) - 1)
    def _():
        for h in heads:
            o_ref[0, :, h * d:(h + 1) * d] = acc_scr[h * DEC_ROWS:(h + 1) * DEC_ROWS, :]


def _sb_decode(sq, sk_new, sv_new, cache_k, cache_v, page_table, bias):
    b = sq.shape[0]
    n_pages = page_table.shape[1]
    page = cache_k.shape[1]
    g_pages = min(SB_PAGES_PER_STEP, n_pages)
    n_steps = n_pages // g_pages
    d = HEAD_DIM_AB
    q4 = sq.reshape(b, DEC_ROWS, SB_HEADS, d)
    eye = jnp.eye(SB_HEADS, dtype=F32)
    qbd = jnp.einsum('bihd,hg->bhdgi', q4, eye).reshape(b, SB_W, SB_HEADS * DEC_ROWS)
    qbd = _bf(jnp.pad(qbd, ((0, 0), (0, 0), (0, LANES - SB_HEADS * DEC_ROWS))))
    bias_lane = jnp.pad(jnp.repeat(bias.astype(F32), DEC_ROWS), (0, LANES - SB_HEADS * DEC_ROWS))[None, :]
    unit_keys = page * (2 if g_pages % 2 == 0 else 1)
    i = np.arange(unit_keys)
    trit = jnp.asarray((i[None, :] > i[:, None]).astype(np.float32), BF16)

    def page_spec(u):
        return pl.BlockSpec((1, page * SB_HEADS, d),
                            lambda bi, g, pt: (pt[bi * n_pages + (n_steps - 1 - g) * g_pages + u], 0, 0))

    per_seq = lambda r, w: pl.BlockSpec((1, r, w), lambda bi, g, pt: (bi, 0, 0))
    grid_spec = pltpu.PrefetchScalarGridSpec(
        num_scalar_prefetch=1,
        grid=(b, n_steps),
        in_specs=[per_seq(SB_W, LANES), pl.BlockSpec((1, LANES), lambda bi, g, pt: (0, 0)),
                  per_seq(DEC_ROWS, SB_W), per_seq(DEC_ROWS, SB_W),
                  pl.BlockSpec((unit_keys, unit_keys), lambda bi, g, pt: (0, 0))]
                 + [page_spec(u) for u in range(g_pages)] * 2,
        out_specs=per_seq(DEC_ROWS, SB_W),
        scratch_shapes=[pltpu.VMEM((SB_HEADS * DEC_ROWS, d), F32), pltpu.VMEM((DEC_ROWS, LANES), F32)],
    )
    return pl.pallas_call(
        functools.partial(_sb_decode_kernel, n_pages_step=g_pages, page=page),
        grid_spec=grid_spec,
        out_shape=jax.ShapeDtypeStruct((b, DEC_ROWS, SB_W), F32),
        compiler_params=_cparams(("parallel", "arbitrary")),
        name="sb_decode",
    )(page_table.reshape(-1), qbd, bias_lane, sk_new, sv_new, trit,
      *([cache_k.reshape(-1, page * SB_HEADS, d)] * g_pages), *([cache_v.reshape(-1, page * SB_HEADS, d)] * g_pages))


def _inproj_c_kernel(x_ref, g_ref, w_ref, q_ref, k_ref, v_ref):
    a = _rms(x_ref[...], g_ref[...])
    p = _dot(_bf(a), w_ref[...])
    q_ref[...] = p[:, :DIL_W]
    k_ref[...] = p[:, DIL_W:2 * DIL_W]
    v_ref[...] = p[:, 2 * DIL_W:]


def _inproj_c(x, gain, w_bf):
    m, dm = x.shape
    tm = min(m, INPROJ_ROWS)
    row = pl.BlockSpec((tm, DIL_W), lambda i: (i, 0))
    return pl.pallas_call(
        _inproj_c_kernel,
        grid=(m // tm,),
        in_specs=[pl.BlockSpec((tm, dm), lambda i: (i, 0)), pl.BlockSpec((1, dm), lambda i: (0, 0)),
                  pl.BlockSpec((dm, 3 * DIL_W), lambda i: (0, 0))],
        out_specs=[row] * 3,
        out_shape=[jax.ShapeDtypeStruct((m, DIL_W), F32)] * 3,
        compiler_params=_cparams(("parallel",)),
        name="inproj_c",
    )(x, gain, w_bf)


DIL_SUPER = DIL_WINDOW_MAX
DIL_SKEW = 2


def _dil_prompt_kernel(q_ref, kp_ref, kc_ref, vp_ref, vc_ref, o_ref, kwin, vwin, o_scr, lse_scr):
    sb = pl.program_id(0)
    blk = DIL_SPAN
    kwin[:DIL_SUPER, :] = kp_ref[...]
    kwin[DIL_SUPER:, :] = kc_ref[...]
    vwin[:DIL_SUPER, :] = vp_ref[...]
    vwin[DIL_SUPER:, :] = vc_ref[...]
    rows = lax.broadcasted_iota(jnp.int32, (blk, 2 * blk), 0)
    cols = lax.broadcasted_iota(jnp.int32, (blk, 2 * blk), 1)
    dist = blk + rows - cols
    in_span = (dist >= 0) & (dist <= DIL_SPAN)
    in_span_first = in_span & ((cols >= blk) | (sb > 0))
    lane = lax.broadcasted_iota(jnp.int32, (blk, LANES), 1)
    first = lane < DIL_HEAD_DIM
    scale = DIL_HEAD_DIM ** -0.5
    units = [(g, dil, r, j) for g, (_, dil) in enumerate(DIL_BRANCHES)
             for j in range(DIL_SUPER // (dil * blk)) for r in range(dil)]

    def scores(unit):
        g, dil, r, j = unit
        q = q_ref[pl.ds(r + dil * blk * j, blk, stride=dil), :]
        k2 = _bf(kwin[pl.ds(DIL_SUPER + r + dil * blk * (j - 1), 2 * blk, stride=dil), :])
        zero = jnp.zeros_like(q)
        return [lax.dot_general(_bf(qh), k2, _NT, preferred_element_type=F32) * scale
                for qh in (jnp.where(first, q, zero), jnp.where(first, zero, q))]

    def attend(unit, zs):
        g, dil, r, j = unit
        v2 = _bf(vwin[pl.ds(DIL_SUPER + r + dil * blk * (j - 1), 2 * blk, stride=dil), :])
        valid = in_span_first if j == 0 else in_span
        outs, lses = [], []
        for z in zs:
            z = jnp.where(valid, z, -jnp.inf)
            m = jnp.max(z, axis=-1, keepdims=True)
            e = jnp.exp(z - m)
            den = jnp.sum(e, axis=-1, keepdims=True)
            outs.append(_dot(_bf(e), v2) / den)
            lses.append(m + jnp.log(den))
        dst = pl.ds(r + dil * blk * j, blk, stride=dil)
        o_scr[g, dst, :] = jnp.where(first, outs[0], outs[1])
        lse_scr[g, dst, :] = jnp.where(first, lses[0], lses[1])

    pending = {}
    for n in range(len(units) + DIL_SKEW):
        if n < len(units):
            pending[n] = scores(units[n])
        if n >= DIL_SKEW:
            attend(units[n - DIL_SKEW], pending.pop(n - DIL_SKEW))

    n_br = len(DIL_BRANCHES)
    for c in range(DIL_SUPER // blk):
        rs = slice(c * blk, (c + 1) * blk)
        lses = [lse_scr[g, rs, :] for g in range(n_br)]
        top = functools.reduce(jnp.maximum, lses)
        ws = [jnp.exp(l - top) for l in lses]
        num = functools.reduce(lambda a, b: a + b, [w * o_scr[g, rs, :] for g, w in enumerate(ws)])
        o_ref[rs, :] = (num / functools.reduce(lambda a, b: a + b, ws)).astype(o_ref.dtype)


def _dil_prompt(q, k, v):
    t = q.shape[0]
    assert t % DIL_SUPER == 0
    cur = pl.BlockSpec((DIL_SUPER, LANES), lambda s, hp: (s, hp))
    prev = pl.BlockSpec((DIL_SUPER, LANES), lambda s, hp: (jnp.maximum(s - 1, 0), hp))
    n_br = len(DIL_BRANCHES)
    return pl.pallas_call(
        _dil_prompt_kernel,
        grid=(t // DIL_SUPER, DIL_W // LANES),
        in_specs=[cur, prev, cur, prev, cur],
        out_specs=cur,
        out_shape=jax.ShapeDtypeStruct((t, DIL_W), BF16),
        scratch_shapes=[pltpu.VMEM((2 * DIL_SUPER, LANES), F32), pltpu.VMEM((2 * DIL_SUPER, LANES), F32),
                        pltpu.VMEM((n_br, DIL_SUPER, LANES), F32), pltpu.VMEM((n_br, DIL_SUPER, LANES), F32)],
        compiler_params=_cparams(("parallel", "parallel")),
        name="dilated_prompt",
    )(q, k, k, v, v)


def _dil_multiplicity(dist):
    cnt = jnp.zeros(dist.shape, F32)
    for window, dil in DIL_BRANCHES:
        hit = (dist >= 0) & (dist <= window) & ((dist & (dil - 1)) == 0)
        cnt = cnt + jnp.where(hit, 1.0, 0.0)
    return cnt


def _dil_decode_kernel(q_ref, kt_ref, vt_ref, knew_ref, vnew_ref, o_ref, kout_ref, vout_ref, *, n_past, n_new):
    scale = DIL_HEAD_DIM ** -0.5
    heads = range(kt_ref.shape[1])
    qi = lax.broadcasted_iota(jnp.int32, (DEC_ROWS, n_past), 0)
    key = lax.broadcasted_iota(jnp.int32, (DEC_ROWS, n_past), 1)
    cnt_past = _dil_multiplicity(n_past + qi - key)
    qi_new = lax.broadcasted_iota(jnp.int32, (DEC_ROWS, LANES), 0)
    lane = lax.broadcasted_iota(jnp.int32, (DEC_ROWS, LANES), 1)
    j_new = lane - (LANES - n_new)
    cnt_new = jnp.where(j_new >= 0, _dil_multiplicity(qi_new - j_new), 0.0)

    qs = [_bf(q_ref[0, h]) for h in heads]
    z_past = [_dot(qs[h], _bf(kt_ref[0, h])) * scale for h in heads]
    z_new = [_dot(qs[h], _bf(knew_ref[0, h])) * scale for h in heads]
    ps = []
    for h in heads:
        m = jnp.maximum(jnp.max(jnp.where(cnt_past > 0, z_past[h], -jnp.inf), axis=1, keepdims=True),
                        jnp.max(jnp.where(cnt_new > 0, z_new[h], -jnp.inf), axis=1, keepdims=True))
        p_past = jnp.where(cnt_past > 0, cnt_past * jnp.exp(z_past[h] - m), 0.0)
        p_new = jnp.where(cnt_new > 0, cnt_new * jnp.exp(z_new[h] - m), 0.0)
        den = jnp.sum(p_past, axis=1, keepdims=True) + jnp.sum(p_new, axis=1, keepdims=True)
        ps.append((p_past, p_new, den))
    for h in heads:
        p_past, p_new, den = ps[h]
        num = (lax.dot_general(_bf(p_past), _bf(vt_ref[0, h]), _NT, preferred_element_type=F32)
               + lax.dot_general(_bf(p_new), _bf(vnew_ref[0, h]), _NT, preferred_element_type=F32))
        o_ref[0, h] = num / den

    tail = lax.broadcasted_iota(jnp.int32, (DIL_HEAD_DIM, LANES), 1) >= LANES - n_new
    for src, new, dst in ((kt_ref, knew_ref, kout_ref), (vt_ref, vnew_ref, vout_ref)):
        for h in heads:
            shifted = pltpu.roll(src[0, h], n_past - n_new, 1)
            dst[0, h, :, :n_past - LANES] = shifted[:, :n_past - LANES]
            dst[0, h, :, n_past - LANES:] = jnp.where(tail, new[0, h], shifted[:, n_past - LANES:])


DIL_DEC_HEADS_PER_STEP = 4


def _dil_decode(q, k_new, v_new, cache_k, cache_v):
    b, n_new, _ = q.shape
    n_past = cache_k.shape[1]
    assert n_past == DIL_WINDOW_MAX and n_new <= DEC_ROWS
    hd = (DIL_HEADS, DIL_HEAD_DIM)
    to_t = lambda a: jnp.transpose(a, (0, 2, 3, 1))
    new_t = lambda a: jnp.pad(to_t(a.reshape(b, n_new, *hd)), ((0, 0), (0, 0), (0, 0), (LANES - n_new, 0)))
    q_rows = jnp.pad(jnp.transpose(q.reshape(b, n_new, *hd), (0, 2, 1, 3)),
                     ((0, 0), (0, 0), (0, DEC_ROWS - n_new), (0, 0)))
    hg = DIL_DEC_HEADS_PER_STEP
    spec = lambda r, w: pl.BlockSpec((1, hg, r, w), lambda bi, g: (bi, g, 0, 0))
    cache_shape = jax.ShapeDtypeStruct((b, DIL_HEADS, DIL_HEAD_DIM, n_past), F32)
    o, k_out, v_out = pl.pallas_call(
        functools.partial(_dil_decode_kernel, n_past=n_past, n_new=n_new),
        grid=(b, DIL_HEADS // hg),
        in_specs=[spec(DEC_ROWS, DIL_HEAD_DIM), spec(DIL_HEAD_DIM, n_past), spec(DIL_HEAD_DIM, n_past),
                  spec(DIL_HEAD_DIM, LANES), spec(DIL_HEAD_DIM, LANES)],
        out_specs=[spec(DEC_ROWS, DIL_HEAD_DIM), spec(DIL_HEAD_DIM, n_past), spec(DIL_HEAD_DIM, n_past)],
        out_shape=[jax.ShapeDtypeStruct((b, DIL_HEADS, DEC_ROWS, DIL_HEAD_DIM), F32), cache_shape, cache_shape],
        compiler_params=_cparams(("parallel", "parallel")),
        name="dilated_decode",
    )(q_rows, to_t(cache_k), to_t(cache_v), new_t(k_new), new_t(v_new))
    from_t = lambda a: jnp.transpose(a, (0, 3, 1, 2))
    o = jnp.transpose(o[:, :, :n_new, :], (0, 2, 1, 3)).reshape(b, n_new, DIL_W)
    return o, from_t(k_out), from_t(v_out)


def _mix_concat(*refs):
    return jnp.concatenate([_bf(r[...]) for r in refs], axis=-1)


def _tail_kernel(*refs, n_mix, mix_fn):
    mix_refs = refs[:n_mix]
    (h_ref, wout_ref, gpost_ref, gpre_ref, w1_ref, w2_ref, gffn_ref,
     out_ref, h1_scr, a_scr, acc_scr) = refs[n_mix:]
    f = pl.program_id(1)

    @pl.when(f == 0)
    def _():
        m = _dot(mix_fn(*mix_refs), wout_ref[...])
        h1 = h_ref[...] + _rms(m, gpost_ref[...])
        h1_scr[...] = h1
        a_scr[...] = _bf(_rms(h1, gpre_ref[...]))
        acc_scr[...] = jnp.zeros_like(acc_scr)

    hid = jnp.square(jnp.maximum(_dot(a_scr[...], w1_ref[...]), 0.0))
    acc_scr[...] += _dot(_bf(hid), w2_ref[...])

    @pl.when(f == pl.num_programs(1) - 1)
    def _():
        out_ref[...] = h1_scr[...] + _rms(acc_scr[...], gffn_ref[...])


TAIL_ROWS = 1024
TAIL_FF = 512


def _layer_tail(mix, mix_fn, h, w_out, g_post, g_ffn_pre, w1, w2, g_ffn_post):
    m, dm = h.shape
    dff = w1.shape[1]
    tm = min(m, TAIL_ROWS)
    tf = min(dff, TAIL_FF)
    row = lambda w: pl.BlockSpec((tm, w), lambda i, f: (i, 0))
    gain = pl.BlockSpec((1, dm), lambda i, f: (0, 0))
    return pl.pallas_call(
        functools.partial(_tail_kernel, n_mix=len(mix), mix_fn=mix_fn),
        grid=(m // tm, dff // tf),
        in_specs=[row(a.shape[1]) for a in mix]
                 + [row(dm), pl.BlockSpec(w_out.shape, lambda i, f: (0, 0)), gain, gain,
                    pl.BlockSpec((dm, tf), lambda i, f: (0, f)), pl.BlockSpec((tf, dm), lambda i, f: (f, 0)), gain],
        out_specs=row(dm),
        out_shape=jax.ShapeDtypeStruct((m, dm), F32),
        scratch_shapes=[pltpu.VMEM((tm, dm), F32), pltpu.VMEM((tm, dm), BF16), pltpu.VMEM((tm, dm), F32)],
        compiler_params=_cparams(("parallel", "arbitrary")),
        name="layer_tail",
    )(*mix, h, w_out, g_post, g_ffn_pre, w1, w2, g_ffn_post)


def _rope_tables(pos):
    half = HEAD_DIM_AB // 2
    inv_freq = jnp.power(ROPE_BASE, -jnp.linspace(0.0, 1.0, half, dtype=F32))
    ang = pos.astype(F32)[:, None] * inv_freq[None, :]
    cos, sin = jnp.cos(ang), jnp.sin(ang)
    return jnp.concatenate([cos, cos], axis=-1), jnp.concatenate([-sin, sin], axis=-1)


def _pad_rows(a, rows):
    return jnp.pad(a, ((0, 0), (0, rows - a.shape[1]), (0, 0)))


def kernel(x_prompt, x_sample, cache_sb_k, cache_sb_v, state_ret, cache_swa_k, cache_swa_v, page_table,
           w_in_ab, w_out_ab, sb_bias, w_in_c, w_out_c, w_ff1, w_ff2, g_mix_pre, g_mix_post, g_ffn_pre, g_ffn_post):
    bp, t, dm = x_prompt.shape
    bs, ts, _ = x_sample.shape
    assert bp == 1 and ts <= DEC_ROWS and t % RET_CHUNK == 0
    n_pages = page_table.shape[1]
    page = cache_sb_k.shape[1]
    past_len = n_pages * page
    d = HEAD_DIM_AB
    gain = lambda g, layer: g[layer][None, :].astype(F32)
    w_in_ab_bf, w_out_ab_bf, w_in_c_bf, w_out_c_bf = _bf(w_in_ab), _bf(w_out_ab), _bf(w_in_c), _bf(w_out_c)
    w_ff1_bf, w_ff2_bf = _bf(w_ff1), _bf(w_ff2)

    h_p = x_prompt.reshape(t, dm)
    h_s = x_sample.reshape(bs * ts, dm)

    cos_p, sin_p = _rope_tables(jnp.arange(t, dtype=jnp.int32))
    rq, rk, rv, rg, q_hm, k_hm, v_hm, sb_k_p, sb_v_p = _inproj_ab(h_p, gain(g_mix_pre, 0), w_in_ab_bf, cos_p, sin_p)
    seq = lambda a: a.reshape(1, t, RET_W)
    ret_zero = jnp.zeros((1, RET_HEADS, d, d), F32)
    y_ret_p, ret_p = _retention(seq(rq), seq(rk), seq(rv), seq(rg), ret_zero, float(RET_CHUNK))
    o_sb_p = _sb_prompt(q_hm, k_hm, v_hm, sb_bias.astype(F32))
    h_p = _layer_tail([y_ret_p.reshape(t, RET_W), o_sb_p], _mix_concat, h_p, w_out_ab_bf, gain(g_mix_post, 0),
                      gain(g_ffn_pre, 0), w_ff1_bf[0], w_ff2_bf[0], gain(g_ffn_post, 0))

    cos_s, sin_s = _rope_tables(past_len + jnp.arange(ts, dtype=jnp.int32))
    tile_s = lambda a: jnp.tile(a, (bs, 1))
    rq, rk, rv, rg, q_hm, _, _, sb_k_s, sb_v_s = _inproj_ab(h_s, gain(g_mix_pre, 0), w_in_ab_bf,
                                                            tile_s(cos_s), tile_s(sin_s))
    dec = lambda a: _pad_rows(a.reshape(bs, ts, a.shape[-1]), DEC_ROWS)
    y_ret_s, ret_s = _retention(dec(rq), dec(rk), dec(rv), dec(rg), state_ret.astype(F32), float(ts))
    sq_s = jnp.transpose(q_hm.astype(F32), (1, 0, 2)).reshape(bs * ts, SB_W)
    o_sb_s = _sb_decode(dec(sq_s), dec(sb_k_s.reshape(bs * ts, SB_W)), dec(sb_v_s.reshape(bs * ts, SB_W)),
                        cache_sb_k, cache_sb_v, page_table, sb_bias.astype(F32))
    undec = lambda a: a[:, :ts].reshape(bs * ts, a.shape[-1])
    h_s = _layer_tail([undec(y_ret_s), undec(o_sb_s)], _mix_concat, h_s, w_out_ab_bf, gain(g_mix_post, 0),
                      gain(g_ffn_pre, 0), w_ff1_bf[0], w_ff2_bf[0], gain(g_ffn_post, 0))

    q, k_f32, v_f32 = _inproj_c(h_p, gain(g_mix_pre, 1), w_in_c_bf)
    h_p = _layer_tail([_dil_prompt(q, k_f32, v_f32)], _mix_concat, h_p, w_out_c_bf, gain(g_mix_post, 1),
                      gain(g_ffn_pre, 1), w_ff1_bf[1], w_ff2_bf[1], gain(g_ffn_post, 1))
    keep_p = min(DIL_WINDOW_MAX, t)
    swa_k_p = k_f32[t - keep_p:].reshape(1, keep_p, DIL_HEADS, DIL_HEAD_DIM)
    swa_v_p = v_f32[t - keep_p:].reshape(1, keep_p, DIL_HEADS, DIL_HEAD_DIM)

    q, k_f32, v_f32 = _inproj_c(h_s, gain(g_mix_pre, 1), w_in_c_bf)
    per_seq = lambda a: a.reshape(bs, ts, DIL_W)
    o_dil_s, swa_k_s, swa_v_s = _dil_decode(per_seq(q.astype(F32)), per_seq(k_f32), per_seq(v_f32),
                                            cache_swa_k.astype(F32), cache_swa_v.astype(F32))
    h_s = _layer_tail([o_dil_s.reshape(bs * ts, DIL_W)], _mix_concat, h_s, w_out_c_bf, gain(g_mix_post, 1),
                      gain(g_ffn_pre, 1), w_ff1_bf[1], w_ff2_bf[1], gain(g_ffn_post, 1))

    heads_ab = lambda a, b: a.reshape(b, -1, SB_HEADS, d)
    return (h_p.reshape(1, t, dm), h_s.reshape(bs, ts, dm),
            heads_ab(sb_k_p, 1), heads_ab(sb_v_p, 1), ret_p,
            swa_k_p, swa_v_p,
            heads_ab(sb_k_s, bs), heads_ab(sb_v_s, bs), ret_s,
            swa_k_s, swa_v_s)
```

```python
import functools

import numpy as np
import jax
import jax.numpy as jnp
from jax import lax
from jax.experimental import pallas as pl
from jax.experimental.pallas import tpu as pltpu

F32 = jnp.float32
BF16 = jnp.bfloat16

NORM_EPS = 1e-6
ROPE_BASE = 10000.0
RET_HEADS = 4
SB_HEADS = 4
HEAD_DIM_AB = 128
RET_W = RET_HEADS * HEAD_DIM_AB
SB_W = SB_HEADS * HEAD_DIM_AB
RET_CHUNK = 128
DIL_HEADS = 16
DIL_HEAD_DIM = 64
DIL_W = DIL_HEADS * DIL_HEAD_DIM
DIL_BRANCHES = ((128, 1), (512, 4), (2048, 16))
DIL_WINDOW_MAX = 2048
DIL_SPAN = 128
LANES = 128
DEC_ROWS = 8
INPROJ_ROWS = 512
VMEM_LIMIT = 56 * 1024 * 1024

_NT = (((1,), (1,)), ((), ()))
_TN = (((0,), (0,)), ((), ()))


def _cparams(sem):
    return pltpu.CompilerParams(dimension_semantics=sem, vmem_limit_bytes=VMEM_LIMIT)


def _bf(x):
    return x.astype(BF16)


def _dot(a, b):
    return jnp.dot(a, b, preferred_element_type=F32)


def _rms(x, gain):
    return x * lax.rsqrt(jnp.mean(x * x, axis=-1, keepdims=True) + NORM_EPS) * gain


def _softplus(z):
    return jnp.maximum(z, 0.0) + jnp.log1p(jnp.exp(-jnp.abs(z)))


def _inproj_ab_kernel(x_ref, g_ref, w_ref, cos_ref, sin_ref,
                      rq_ref, rk_ref, rv_ref, rg_ref, q_hm, k_hm, v_hm, sbk_ref, sbv_ref):
    a = _rms(x_ref[...], g_ref[...])
    p = _dot(_bf(a), w_ref[...])
    cosf = cos_ref[...]
    sinf = sin_ref[...]
    d = HEAD_DIM_AB
    for h in range(RET_HEADS):
        qh = p[:, h * d:(h + 1) * d]
        kh = p[:, RET_W + h * d:RET_W + (h + 1) * d]
        rq_ref[:, h * d:(h + 1) * d] = qh * cosf + pltpu.roll(qh, d // 2, 1) * sinf
        rk_ref[:, h * d:(h + 1) * d] = (kh * cosf + pltpu.roll(kh, d // 2, 1) * sinf) * (d ** -0.5)
    rv_ref[...] = p[:, 2 * RET_W:3 * RET_W]
    rg_ref[...] = p[:, 3 * RET_W:4 * RET_W]
    base = 4 * RET_W
    for h in range(SB_HEADS):
        q_hm[h] = _bf(p[:, base + h * d:base + (h + 1) * d] * (d ** -0.5 * LOG2E))
        k_hm[h] = _bf(p[:, base + SB_W + h * d:base + SB_W + (h + 1) * d])
        v_hm[h] = _bf(p[:, base + 2 * SB_W + h * d:base + 2 * SB_W + (h + 1) * d])
    tm = p.shape[0]
    for h in range(SB_HEADS):
        sbk_ref[pl.ds(h, tm, stride=SB_HEADS), :] = p[:, base + SB_W + h * d:base + SB_W + (h + 1) * d]
        sbv_ref[pl.ds(h, tm, stride=SB_HEADS), :] = p[:, base + 2 * SB_W + h * d:base + 2 * SB_W + (h + 1) * d]


def _inproj_ab(x, gain, w_bf, cosf, sinf):
    m, dm = x.shape
    n = w_bf.shape[1]
    tm = min(m, INPROJ_ROWS)
    row = lambda w: pl.BlockSpec((tm, w), lambda i: (i, 0))
    hm = pl.BlockSpec((SB_HEADS, tm, HEAD_DIM_AB), lambda i: (0, i, 0))
    f32_rows = jax.ShapeDtypeStruct((m, RET_W), F32)
    hm_shape = jax.ShapeDtypeStruct((SB_HEADS, m, HEAD_DIM_AB), BF16)
    return pl.pallas_call(
        _inproj_ab_kernel,
        grid=(m // tm,),
        in_specs=[row(dm), pl.BlockSpec((1, dm), lambda i: (0, 0)),
                  pl.BlockSpec((dm, n), lambda i: (0, 0)), row(HEAD_DIM_AB), row(HEAD_DIM_AB)],
        out_specs=[row(RET_W)] * 4 + [hm] * 3 + [pl.BlockSpec((tm * SB_HEADS, HEAD_DIM_AB), lambda i: (i, 0))] * 2,
        out_shape=[f32_rows] * 4 + [hm_shape] * 3 + [jax.ShapeDtypeStruct((m * SB_HEADS, HEAD_DIM_AB), F32)] * 2,
        compiler_params=_cparams(("parallel",)),
        name="inproj_ab",
    )(x, gain, w_bf, cosf, sinf)


def _ret_kernel(rq, rk, rv, rg, s0, intra, qdec, kdec, cdec, y_ref, s_out, s_scr, *, rows, n_chunks):
    step = pl.program_id(1)
    d = HEAD_DIM_AB

    @pl.when(step == 0)
    def _():
        s_scr[...] = s0[0]

    for c in range(n_chunks):
        rs = slice(c * rows, (c + 1) * rows)
        for h in range(RET_HEADS):
            cs = slice(h * d, (h + 1) * d)
            q, k, v, g = rq[0, rs, cs], rk[0, rs, cs], rv[0, rs, cs], rg[0, rs, cs]
            if rows < RET_CHUNK:
                pad = jnp.zeros((RET_CHUNK - rows, d), F32)
                k = jnp.concatenate([k, pad], axis=0)
                v = jnp.concatenate([v, pad], axis=0)
            s_prev = s_scr[h]
            att = lax.dot_general(_bf(q), _bf(k), _NT, preferred_element_type=F32) * intra[h, :rows, :]
            o = _dot(_bf(att), _bf(v)) + _dot(_bf(q * qdec[h, :rows, :]), _bf(s_prev))
            kd = k * kdec[h]
            s_scr[h] = s_prev * cdec[h] + lax.dot_general(_bf(kd), _bf(v), _TN, preferred_element_type=F32)
            xc = o - jnp.mean(o, axis=-1, keepdims=True)
            var = jnp.mean(xc * xc, axis=-1, keepdims=True)
            y_ref[0, rs, cs] = xc * lax.rsqrt(var + NORM_EPS) * (g * jax.nn.sigmoid(g))

    @pl.when(step == pl.num_programs(1) - 1)
    def _():
        s_out[0] = s_scr[...]


def _ret_tables(chunk_len):
    lg = jnp.log1p(-jnp.exp2(-5.0 - jnp.arange(RET_HEADS, dtype=F32)))
    i = jnp.arange(RET_CHUNK, dtype=F32)
    live = i < chunk_len
    rel = i[:, None] - i[None, :]
    intra = jnp.where((rel >= 0) & live[:, None] & live[None, :],
                      jnp.exp(lg[:, None, None] * jnp.maximum(rel, 0.0)), 0.0)
    q_dec = jnp.where(live[None, :], jnp.exp(lg[:, None] * (i[None, :] + 1.0)), 0.0)
    k_dec = jnp.where(live[None, :], jnp.exp(lg[:, None] * (chunk_len - 1.0 - i[None, :])), 0.0)
    c_dec = jnp.exp(lg * chunk_len)
    full = lambda t: jnp.broadcast_to(t[:, :, None], (RET_HEADS, RET_CHUNK, LANES))
    return intra, full(q_dec), full(k_dec), jnp.broadcast_to(c_dec[:, None, None], (RET_HEADS, RET_CHUNK, LANES))


def _retention(rq, rk, rv, rg, s0, chunk_len):
    b, t, _ = rq.shape
    if t % RET_CHUNK == 0:
        rows = RET_CHUNK
        n_chunks = min(8, t // rows)
    else:
        rows, n_chunks = t, 1
    tc = rows * n_chunks
    seq = pl.BlockSpec((1, tc, RET_W), lambda bi, s: (bi, s, 0))
    st = pl.BlockSpec((1, RET_HEADS, HEAD_DIM_AB, HEAD_DIM_AB), lambda bi, s: (bi, 0, 0, 0))
    tab = pl.BlockSpec((RET_HEADS, RET_CHUNK, LANES), lambda bi, s: (0, 0, 0))
    return pl.pallas_call(
        functools.partial(_ret_kernel, rows=rows, n_chunks=n_chunks),
        grid=(b, t // tc),
        in_specs=[seq] * 4 + [st] + [tab] * 4,
        out_specs=[seq, st],
        out_shape=[jax.ShapeDtypeStruct((b, t, RET_W), F32), jax.ShapeDtypeStruct(s0.shape, F32)],
        scratch_shapes=[pltpu.VMEM((RET_HEADS, HEAD_DIM_AB, HEAD_DIM_AB), F32)],
        compiler_params=_cparams(("parallel", "arbitrary")),
        name="retention",
    )(rq, rk, rv, rg, s0, *_ret_tables(chunk_len))


SB_SUB = 256
SB_ROWS = 128
LOG2E = 1.4426950408889634
SB_DECAYED = 152.0


def _sb_tile(q_ref, k_ref, v_ref, tri_ref, zbias, u_scr, sp_scr, w_scr, carry_scr, acc_scr, units, diagonal):
    d = HEAD_DIM_AB
    n_chunks = SB_SUB // SB_ROWS
    keys_of = lambda c: pl.ds(c * SB_SUB, SB_SUB)
    rows_of = lambda g: pl.ds(g * SB_SUB, SB_SUB)
    chunk_of = lambda g, r: pl.ds(g * SB_SUB + r * SB_ROWS, SB_ROWS)
    in_slot = lambda r: pl.ds(r * SB_ROWS, SB_ROWS)
    key = lax.broadcasted_iota(jnp.int32, (SB_ROWS, SB_SUB), 1)
    row = lax.broadcasted_iota(jnp.int32, (SB_ROWS, SB_SUB), 0)

    def mask_of(unit, r):
        c, g = unit
        return key < row + r * SB_ROWS if diagonal and g == c else None

    def scores(unit):
        c, g = unit
        return lax.dot_general(q_ref[rows_of(g), :], k_ref[keys_of(c), :], _NT, preferred_element_type=F32)

    def keep_logs(n, s):
        c, g = units[n]
        slot = n % 2
        for r in range(n_chunks):
            zs = s[r * SB_ROWS:(r + 1) * SB_ROWS, :] + zbias
            sp = jnp.maximum(zs, 0.0) + jnp.log2(1.0 + jnp.exp2(-jnp.abs(zs)))
            carry = carry_scr[chunk_of(g, r), :]
            u_scr[slot, in_slot(r), :] = zs - sp - jnp.tile(carry, (1, SB_SUB // LANES))
            mask = mask_of(units[n], r)
            if mask is not None:
                sp = jnp.where(mask, sp, 0.0)
            sp_scr[slot, in_slot(r), :] = _bf(sp)
            carry_scr[chunk_of(g, r), :] = carry + jnp.sum(sp, axis=1, keepdims=True)
        return _dot(sp_scr[slot], tri_ref[...])

    def weights(n, sums):
        c, g = units[n]
        slot = n % 2
        for r in range(n_chunks):
            w = jnp.exp2(u_scr[slot, in_slot(r), :] - sums[r * SB_ROWS:(r + 1) * SB_ROWS, :])
            mask = mask_of(units[n], r)
            if mask is not None:
                w = jnp.where(mask, w, 0.0)
            w_scr[slot, in_slot(r), :] = _bf(w)
        acc_scr[rows_of(g), :] += _dot(w_scr[slot], v_ref[keys_of(c), :])

    s, sums = {}, {}
    for n in range(len(units) + 2):
        if n < len(units):
            s[n] = scores(units[n])
        if 1 <= n <= len(units):
            sums[n - 1] = keep_logs(n - 1, s.pop(n - 1))
        if n >= 2:
            weights(n - 2, sums.pop(n - 2))


def _sb_prompt_kernel(qi_tab, kj_tab, bias_ref, q_ref, k_ref, v_ref, tri_ref, o_ref,
                      acc_scr, carry_scr, u_scr, sp_scr, w_scr, least_scr, *, tq):
    p = pl.program_id(0)
    qi = qi_tab[p]
    kj = kj_tab[p]
    n_heads = q_ref.shape[0]
    n_sub = tq // SB_SUB

    def tile(h, units, diagonal):
        _sb_tile(q_ref.at[h], k_ref.at[h], v_ref.at[h], tri_ref, bias_ref[h] * LOG2E, u_scr, sp_scr, w_scr,
                 carry_scr.at[h], acc_scr.at[h], units, diagonal)

    def for_heads(fn):
        def body(h, carry):
            fn(h)
            return carry
        lax.fori_loop(0, n_heads, body, 0)

    @pl.when(kj == qi)
    def _():
        acc_scr[...] = jnp.zeros_like(acc_scr)
        carry_scr[...] = jnp.zeros_like(carry_scr)

        def diagonal(h):
            tile(h, [(c, g) for c in reversed(range(n_sub)) for g in range(c, n_sub)], True)
            least_scr[h] = 0.0
        for_heads(diagonal)

    @pl.when(kj < qi)
    def _():
        def earlier(h):
            @pl.when(least_scr[h] <= SB_DECAYED)
            def _():
                tile(h, [(c, g) for c in reversed(range(n_sub)) for g in range(n_sub)], False)
                least = carry_scr[h]
                while least.shape[0] > 8:
                    half = least.shape[0] // 2
                    least = jnp.minimum(least[:half], least[half:])
                least_scr[h] = jnp.min(least)
        for_heads(earlier)

    @pl.when(kj == 0)
    def _():
        o_ref[...] = acc_scr[...].astype(o_ref.dtype)


def _tri_ones(n):
    i = np.arange(n)
    return jnp.asarray((i[:, None] > i[None, :]).astype(np.float32), BF16)


def _sb_prompt(q_hm, k_hm, v_hm, bias):
    nh, t, d = q_hm.shape
    tq = min(1024, t)
    nq = t // tq
    qi_tab = np.concatenate([np.full(i + 1, i) for i in range(nq)]).astype(np.int32)
    kj_tab = np.concatenate([np.arange(i, -1, -1) for i in range(nq)]).astype(np.int32)
    grid_spec = pltpu.PrefetchScalarGridSpec(
        num_scalar_prefetch=2,
        grid=(len(qi_tab),),
        in_specs=[
            pl.BlockSpec(memory_space=pltpu.SMEM),
            pl.BlockSpec((nh, tq, d), lambda p, qt, kt: (0, qt[p], 0)),
            pl.BlockSpec((nh, tq, d), lambda p, qt, kt: (0, kt[p], 0)),
            pl.BlockSpec((nh, tq, d), lambda p, qt, kt: (0, kt[p], 0)),
            pl.BlockSpec((SB_SUB, SB_SUB), lambda p, qt, kt: (0, 0)),
        ],
        out_specs=pl.BlockSpec((nh, tq, d), lambda p, qt, kt: (0, qt[p], 0)),
        scratch_shapes=[pltpu.VMEM((nh, tq, d), F32), pltpu.VMEM((nh, tq, LANES), F32),
                        pltpu.VMEM((2, SB_SUB, SB_SUB), F32),
                        pltpu.VMEM((2, SB_SUB, SB_SUB), BF16), pltpu.VMEM((2, SB_SUB, SB_SUB), BF16),
                        pltpu.SMEM((nh,), F32)],
    )
    return pl.pallas_call(
        functools.partial(_sb_prompt_kernel, tq=tq),
        grid_spec=grid_spec,
        out_shape=jax.ShapeDtypeStruct((nh, t, d), BF16),
        compiler_params=_cparams(("arbitrary",)),
        name="sb_prompt",
    )(jnp.asarray(qi_tab), jnp.asarray(kj_tab), bias, q_hm, k_hm, v_hm, _tri_ones(SB_SUB))


SB_PAGES_PER_STEP = 16


def _sb_decode_kernel(pt_ref, qbd_ref, bias_ref, knew_ref, vnew_ref, trit_ref, *rest, n_pages_step, page):
    k_pages = rest[:n_pages_step]
    v_pages = rest[n_pages_step:2 * n_pages_step]
    o_ref, acc_scr, carry_scr = rest[2 * n_pages_step:]
    g = pl.program_id(1)
    qbd = qbd_ref[0]
    zbias = bias_ref[...] * LOG2E
    trit = trit_ref[...]
    d = HEAD_DIM_AB
    heads = range(SB_HEADS)

    def block(k_heads, v_heads, carry, accs, mask):
        s = functools.reduce(lambda a, c: a + c,
                             [_dot(_bf(k_heads[h]), qbd[h * d:(h + 1) * d, :]) for h in heads])
        zs = s + zbias
        l2 = jnp.log2(1.0 + jnp.exp2(-jnp.abs(zs)))
        sp = jnp.maximum(zs, 0.0) + l2
        if mask is not None:
            sp = jnp.where(mask, sp, 0.0)
        later = _dot(trit[:page, :page], _bf(sp))
        wt = jnp.exp2(jnp.minimum(zs, 0.0) - l2 - later - carry)
        if mask is not None:
            wt = jnp.where(mask, wt, 0.0)
        w = jnp.transpose(wt)
        accs = [accs[h] + _dot(_bf(w[h * DEC_ROWS:(h + 1) * DEC_ROWS, :]), _bf(v_heads[h])) for h in heads]
        return carry + jnp.sum(sp, axis=0, keepdims=True), accs

    def store(carry, accs):
        carry_scr[...] = jnp.broadcast_to(carry, carry_scr.shape)
        for h in heads:
            acc_scr[h * DEC_ROWS:(h + 1) * DEC_ROWS, :] = accs[h]

    @pl.when(g == 0)
    def _():
        pad = jnp.zeros((page - DEC_ROWS, d), F32)
        k_heads = [jnp.concatenate([knew_ref[0, :, h * d:(h + 1) * d], pad], axis=0) for h in heads]
        v_heads = [jnp.concatenate([vnew_ref[0, :, h * d:(h + 1) * d], pad], axis=0) for h in heads]
        key = lax.broadcasted_iota(jnp.int32, (page, LANES), 0)
        qry = lax.broadcasted_iota(jnp.int32, (page, LANES), 1) & (DEC_ROWS - 1)
        store(*block(k_heads, v_heads, jnp.zeros((1, LANES), F32),
                     [jnp.zeros((DEC_ROWS, d), F32) for _ in heads], key < qry))

    unit_pages = trit_ref.shape[0] // page
    n_units = n_pages_step // unit_pages
    pages_of = lambda n: range(n_pages_step - (n + 1) * unit_pages, n_pages_step - n * unit_pages)
    head_rows = lambda ref, h: ref[0, pl.ds(h, page, stride=SB_HEADS), :]

    def scores(n):
        k_unit = jnp.concatenate([jnp.concatenate([head_rows(k_pages[u], h) for h in heads], axis=1)
                                  for u in pages_of(n)], axis=0)
        return _dot(_bf(k_unit), qbd) + zbias

    def keep_logs(zs):
        sp = jnp.maximum(zs, 0.0) + jnp.log2(1.0 + jnp.exp2(-jnp.abs(zs)))
        return zs - sp, _dot(trit, _bf(sp)), jnp.sum(sp, axis=0, keepdims=True)

    def weights(n, logs, carry, accs):
        log_beta, later, total = logs
        wt = jnp.exp2(log_beta - later - carry)
        w = jnp.concatenate([jnp.transpose(wt[j * LANES:(j + 1) * LANES, :])
                             for j in range(wt.shape[0] // LANES)], axis=1)
        for h in heads:
            v_unit = jnp.concatenate([head_rows(v_pages[u], h) for u in pages_of(n)], axis=0)
            accs[h] = accs[h] + _dot(_bf(w[h * DEC_ROWS:(h + 1) * DEC_ROWS, :]), _bf(v_unit))
        return carry + total, accs

    carry = carry_scr[0:1, :]
    accs = [acc_scr[h * DEC_ROWS:(h + 1) * DEC_ROWS, :] for h in heads]
    zs, logs = {}, {}
    for m in range(n_units + 2):
        if m < n_units:
            zs[m] = scores(m)
        if 1 <= m <= n_units:
            logs[m - 1] = keep_logs(zs.pop(m - 1))
        if m >= 2:
            carry, accs = weights(m - 2, logs.pop(m - 2), carry, accs)
    store(carry, accs)

    @pl.when(g == pl.num_programs(1) - 1)
    def _():
        for h in heads:
            o_ref[0, :, h * d:(h + 1) * d] = acc_scr[h * DEC_ROWS:(h + 1) * DEC_ROWS, :]


def _sb_decode(sq, sk_new, sv_new, cache_k, cache_v, page_table, bias):
    b = sq.shape[0]
    n_pages = page_table.shape[1]
    page = cache_k.shape[1]
    g_pages = min(SB_PAGES_PER_STEP, n_pages)
    n_steps = n_pages // g_pages
    d = HEAD_DIM_AB
    q4 = sq.reshape(b, DEC_ROWS, SB_HEADS, d)
    eye = jnp.eye(SB_HEADS, dtype=F32)
    qbd = jnp.einsum('bihd,hg->bhdgi', q4, eye).reshape(b, SB_W, SB_HEADS * DEC_ROWS)
    qbd = _bf(jnp.pad(qbd, ((0, 0), (0, 0), (0, LANES - SB_HEADS * DEC_ROWS))))
    bias_lane = jnp.pad(jnp.repeat(bias.astype(F32), DEC_ROWS), (0, LANES - SB_HEADS * DEC_ROWS))[None, :]
    unit_keys = page * (2 if g_pages % 2 == 0 else 1)
    i = np.arange(unit_keys)
    trit = jnp.asarray((i[None, :] > i[:, None]).astype(np.float32), BF16)

    def page_spec(u):
        return pl.BlockSpec((1, page * SB_HEADS, d),
                            lambda bi, g, pt: (pt[bi * n_pages + (n_steps - 1 - g) * g_pages + u], 0, 0))

    per_seq = lambda r, w: pl.BlockSpec((1, r, w), lambda bi, g, pt: (bi, 0, 0))
    grid_spec = pltpu.PrefetchScalarGridSpec(
        num_scalar_prefetch=1,
        grid=(b, n_steps),
        in_specs=[per_seq(SB_W, LANES), pl.BlockSpec((1, LANES), lambda bi, g, pt: (0, 0)),
                  per_seq(DEC_ROWS, SB_W), per_seq(DEC_ROWS, SB_W),
                  pl.BlockSpec((unit_keys, unit_keys), lambda bi, g, pt: (0, 0))]
                 + [page_spec(u) for u in range(g_pages)] * 2,
        out_specs=per_seq(DEC_ROWS, SB_W),
        scratch_shapes=[pltpu.VMEM((SB_HEADS * DEC_ROWS, d), F32), pltpu.VMEM((DEC_ROWS, LANES), F32)],
    )
    return pl.pallas_call(
        functools.partial(_sb_decode_kernel, n_pages_step=g_pages, page=page),
        grid_spec=grid_spec,
        out_shape=jax.ShapeDtypeStruct((b, DEC_ROWS, SB_W), F32),
        compiler_params=_cparams(("parallel", "arbitrary")),
        name="sb_decode",
    )(page_table.reshape(-1), qbd, bias_lane, sk_new, sv_new, trit,
      *([cache_k.reshape(-1, page * SB_HEADS, d)] * g_pages), *([cache_v.reshape(-1, page * SB_HEADS, d)] * g_pages))


def _inproj_c_kernel(x_ref, g_ref, w_ref, q_ref, k_ref, v_ref):
    a = _rms(x_ref[...], g_ref[...])
    p = _dot(_bf(a), w_ref[...])
    q_ref[...] = p[:, :DIL_W]
    k_ref[...] = p[:, DIL_W:2 * DIL_W]
    v_ref[...] = p[:, 2 * DIL_W:]


def _inproj_c(x, gain, w_bf):
    m, dm = x.shape
    tm = min(m, INPROJ_ROWS)
    row = pl.BlockSpec((tm, DIL_W), lambda i: (i, 0))
    return pl.pallas_call(
        _inproj_c_kernel,
        grid=(m // tm,),
        in_specs=[pl.BlockSpec((tm, dm), lambda i: (i, 0)), pl.BlockSpec((1, dm), lambda i: (0, 0)),
                  pl.BlockSpec((dm, 3 * DIL_W), lambda i: (0, 0))],
        out_specs=[row] * 3,
        out_shape=[jax.ShapeDtypeStruct((m, DIL_W), F32)] * 3,
        compiler_params=_cparams(("parallel",)),
        name="inproj_c",
    )(x, gain, w_bf)


DIL_SUPER = DIL_WINDOW_MAX
DIL_SKEW = 2


def _dil_prompt_kernel(q_ref, kp_ref, kc_ref, vp_ref, vc_ref, o_ref, kwin, vwin, o_scr, lse_scr):
    sb = pl.program_id(0)
    blk = DIL_SPAN
    kwin[:DIL_SUPER, :] = kp_ref[...]
    kwin[DIL_SUPER:, :] = kc_ref[...]
    vwin[:DIL_SUPER, :] = vp_ref[...]
    vwin[DIL_SUPER:, :] = vc_ref[...]
    rows = lax.broadcasted_iota(jnp.int32, (blk, 2 * blk), 0)
    cols = lax.broadcasted_iota(jnp.int32, (blk, 2 * blk), 1)
    dist = blk + rows - cols
    in_span = (dist >= 0) & (dist <= DIL_SPAN)
    in_span_first = in_span & ((cols >= blk) | (sb > 0))
    lane = lax.broadcasted_iota(jnp.int32, (blk, LANES), 1)
    first = lane < DIL_HEAD_DIM
    scale = DIL_HEAD_DIM ** -0.5
    units = [(g, dil, r, j) for g, (_, dil) in enumerate(DIL_BRANCHES)
             for j in range(DIL_SUPER // (dil * blk)) for r in range(dil)]

    def scores(unit):
        g, dil, r, j = unit
        q = q_ref[pl.ds(r + dil * blk * j, blk, stride=dil), :]
        k2 = _bf(kwin[pl.ds(DIL_SUPER + r + dil * blk * (j - 1), 2 * blk, stride=dil), :])
        zero = jnp.zeros_like(q)
        return [lax.dot_general(_bf(qh), k2, _NT, preferred_element_type=F32) * scale
                for qh in (jnp.where(first, q, zero), jnp.where(first, zero, q))]

    def attend(unit, zs):
        g, dil, r, j = unit
        v2 = _bf(vwin[pl.ds(DIL_SUPER + r + dil * blk * (j - 1), 2 * blk, stride=dil), :])
        valid = in_span_first if j == 0 else in_span
        outs, lses = [], []
        for z in zs:
            z = jnp.where(valid, z, -jnp.inf)
            m = jnp.max(z, axis=-1, keepdims=True)
            e = jnp.exp(z - m)
            den = jnp.sum(e, axis=-1, keepdims=True)
            outs.append(_dot(_bf(e), v2) / den)
            lses.append(m + jnp.log(den))
        dst = pl.ds(r + dil * blk * j, blk, stride=dil)
        o_scr[g, dst, :] = jnp.where(first, outs[0], outs[1])
        lse_scr[g, dst, :] = jnp.where(first, lses[0], lses[1])

    pending = {}
    for n in range(len(units) + DIL_SKEW):
        if n < len(units):
            pending[n] = scores(units[n])
        if n >= DIL_SKEW:
            attend(units[n - DIL_SKEW], pending.pop(n - DIL_SKEW))

    n_br = len(DIL_BRANCHES)
    for c in range(DIL_SUPER // blk):
        rs = slice(c * blk, (c + 1) * blk)
        lses = [lse_scr[g, rs, :] for g in range(n_br)]
        top = functools.reduce(jnp.maximum, lses)
        ws = [jnp.exp(l - top) for l in lses]
        num = functools.reduce(lambda a, b: a + b, [w * o_scr[g, rs, :] for g, w in enumerate(ws)])
        o_ref[rs, :] = (num / functools.reduce(lambda a, b: a + b, ws)).astype(o_ref.dtype)


def _dil_prompt(q, k, v):
    t = q.shape[0]
    assert t % DIL_SUPER == 0
    cur = pl.BlockSpec((DIL_SUPER, LANES), lambda s, hp: (s, hp))
    prev = pl.BlockSpec((DIL_SUPER, LANES), lambda s, hp: (jnp.maximum(s - 1, 0), hp))
    n_br = len(DIL_BRANCHES)
    return pl.pallas_call(
        _dil_prompt_kernel,
        grid=(t // DIL_SUPER, DIL_W // LANES),
        in_specs=[cur, prev, cur, prev, cur],
        out_specs=cur,
        out_shape=jax.ShapeDtypeStruct((t, DIL_W), BF16),
        scratch_shapes=[pltpu.VMEM((2 * DIL_SUPER, LANES), F32), pltpu.VMEM((2 * DIL_SUPER, LANES), F32),
                        pltpu.VMEM((n_br, DIL_SUPER, LANES), F32), pltpu.VMEM((n_br, DIL_SUPER, LANES), F32)],
        compiler_params=_cparams(("parallel", "parallel")),
        name="dilated_prompt",
    )(q, k, k, v, v)


def _dil_multiplicity(dist):
    cnt = jnp.zeros(dist.shape, F32)
    for window, dil in DIL_BRANCHES:
        hit = (dist >= 0) & (dist <= window) & ((dist & (dil - 1)) == 0)
        cnt = cnt + jnp.where(hit, 1.0, 0.0)
    return cnt


def _dil_decode_kernel(q_ref, kt_ref, vt_ref, knew_ref, vnew_ref, o_ref, kout_ref, vout_ref, *, n_past, n_new):
    scale = DIL_HEAD_DIM ** -0.5
    heads = range(kt_ref.shape[1])
    qi = lax.broadcasted_iota(jnp.int32, (DEC_ROWS, n_past), 0)
    key = lax.broadcasted_iota(jnp.int32, (DEC_ROWS, n_past), 1)
    cnt_past = _dil_multiplicity(n_past + qi - key)
    qi_new = lax.broadcasted_iota(jnp.int32, (DEC_ROWS, LANES), 0)
    lane = lax.broadcasted_iota(jnp.int32, (DEC_ROWS, LANES), 1)
    j_new = lane - (LANES - n_new)
    cnt_new = jnp.where(j_new >= 0, _dil_multiplicity(qi_new - j_new), 0.0)

    qs = [_bf(q_ref[0, h]) for h in heads]
    z_past = [_dot(qs[h], _bf(kt_ref[0, h])) * scale for h in heads]
    z_new = [_dot(qs[h], _bf(knew_ref[0, h])) * scale for h in heads]
    ps = []
    for h in heads:
        m = jnp.maximum(jnp.max(jnp.where(cnt_past > 0, z_past[h], -jnp.inf), axis=1, keepdims=True),
                        jnp.max(jnp.where(cnt_new > 0, z_new[h], -jnp.inf), axis=1, keepdims=True))
        p_past = jnp.where(cnt_past > 0, cnt_past * jnp.exp(z_past[h] - m), 0.0)
        p_new = jnp.where(cnt_new > 0, cnt_new * jnp.exp(z_new[h] - m), 0.0)
        den = jnp.sum(p_past, axis=1, keepdims=True) + jnp.sum(p_new, axis=1, keepdims=True)
        ps.append((p_past, p_new, den))
    for h in heads:
        p_past, p_new, den = ps[h]
        num = (lax.dot_general(_bf(p_past), _bf(vt_ref[0, h]), _NT, preferred_element_type=F32)
               + lax.dot_general(_bf(p_new), _bf(vnew_ref[0, h]), _NT, preferred_element_type=F32))
        o_ref[0, h] = num / den

    tail = lax.broadcasted_iota(jnp.int32, (DIL_HEAD_DIM, LANES), 1) >= LANES - n_new
    for src, new, dst in ((kt_ref, knew_ref, kout_ref), (vt_ref, vnew_ref, vout_ref)):
        for h in heads:
            shifted = pltpu.roll(src[0, h], n_past - n_new, 1)
            dst[0, h, :, :n_past - LANES] = shifted[:, :n_past - LANES]
            dst[0, h, :, n_past - LANES:] = jnp.where(tail, new[0, h], shifted[:, n_past - LANES:])


DIL_DEC_HEADS_PER_STEP = 4


def _dil_decode(q, k_new, v_new, cache_k, cache_v):
    b, n_new, _ = q.shape
    n_past = cache_k.shape[1]
    assert n_past == DIL_WINDOW_MAX and n_new <= DEC_ROWS
    hd = (DIL_HEADS, DIL_HEAD_DIM)
    to_t = lambda a: jnp.transpose(a, (0, 2, 3, 1))
    new_t = lambda a: jnp.pad(to_t(a.reshape(b, n_new, *hd)), ((0, 0), (0, 0), (0, 0), (LANES - n_new, 0)))
    q_rows = jnp.pad(jnp.transpose(q.reshape(b, n_new, *hd), (0, 2, 1, 3)),
                     ((0, 0), (0, 0), (0, DEC_ROWS - n_new), (0, 0)))
    hg = DIL_DEC_HEADS_PER_STEP
    spec = lambda r, w: pl.BlockSpec((1, hg, r, w), lambda bi, g: (bi, g, 0, 0))
    cache_shape = jax.ShapeDtypeStruct((b, DIL_HEADS, DIL_HEAD_DIM, n_past), F32)
    o, k_out, v_out = pl.pallas_call(
        functools.partial(_dil_decode_kernel, n_past=n_past, n_new=n_new),
        grid=(b, DIL_HEADS // hg),
        in_specs=[spec(DEC_ROWS, DIL_HEAD_DIM), spec(DIL_HEAD_DIM, n_past), spec(DIL_HEAD_DIM, n_past),
                  spec(DIL_HEAD_DIM, LANES), spec(DIL_HEAD_DIM, LANES)],
        out_specs=[spec(DEC_ROWS, DIL_HEAD_DIM), spec(DIL_HEAD_DIM, n_past), spec(DIL_HEAD_DIM, n_past)],
        out_shape=[jax.ShapeDtypeStruct((b, DIL_HEADS, DEC_ROWS, DIL_HEAD_DIM), F32), cache_shape, cache_shape],
        compiler_params=_cparams(("parallel", "parallel")),
        name="dilated_decode",
    )(q_rows, to_t(cache_k), to_t(cache_v), new_t(k_new), new_t(v_new))
    from_t = lambda a: jnp.transpose(a, (0, 3, 1, 2))
    o = jnp.transpose(o[:, :, :n_new, :], (0, 2, 1, 3)).reshape(b, n_new, DIL_W)
    return o, from_t(k_out), from_t(v_out)


def _mix_concat(*refs):
    parts = []
    for r in refs:
        parts += [_bf(r[h]) for h in range(r.shape[0])] if len(r.shape) == 3 else [_bf(r[...])]
    return jnp.concatenate(parts, axis=-1)


def _tail_kernel(*refs, n_mix, mix_fn):
    mix_refs = refs[:n_mix]
    (h_ref, wout_ref, gpost_ref, gpre_ref, w1_ref, w2_ref, gffn_ref,
     out_ref, h1_scr, a_scr, acc_scr) = refs[n_mix:]
    f = pl.program_id(1)

    @pl.when(f == 0)
    def _():
        m = _dot(mix_fn(*mix_refs), wout_ref[...])
        h1 = h_ref[...] + _rms(m, gpost_ref[...])
        h1_scr[...] = h1
        a_scr[...] = _bf(_rms(h1, gpre_ref[...]))
        acc_scr[...] = jnp.zeros_like(acc_scr)

    hid = jnp.square(jnp.maximum(_dot(a_scr[...], w1_ref[...]), 0.0))
    acc_scr[...] += _dot(_bf(hid), w2_ref[...])

    @pl.when(f == pl.num_programs(1) - 1)
    def _():
        out_ref[...] = h1_scr[...] + _rms(acc_scr[...], gffn_ref[...])


TAIL_ROWS = 1024
TAIL_FF = 512


def _layer_tail(mix, mix_fn, h, w_out, g_post, g_ffn_pre, w1, w2, g_ffn_post):
    m, dm = h.shape
    dff = w1.shape[1]
    tm = min(m, TAIL_ROWS)
    tf = min(dff, TAIL_FF)
    row = lambda w: pl.BlockSpec((tm, w), lambda i, f: (i, 0))
    gain = pl.BlockSpec((1, dm), lambda i, f: (0, 0))
    return pl.pallas_call(
        functools.partial(_tail_kernel, n_mix=len(mix), mix_fn=mix_fn),
        grid=(m // tm, dff // tf),
        in_specs=[row(a.shape[1]) if a.ndim == 2 else pl.BlockSpec((a.shape[0], tm, a.shape[2]), lambda i, f: (0, i, 0))
                  for a in mix]
                 + [row(dm), pl.BlockSpec(w_out.shape, lambda i, f: (0, 0)), gain, gain,
                    pl.BlockSpec((dm, tf), lambda i, f: (0, f)), pl.BlockSpec((tf, dm), lambda i, f: (f, 0)), gain],
        out_specs=row(dm),
        out_shape=jax.ShapeDtypeStruct((m, dm), F32),
        scratch_shapes=[pltpu.VMEM((tm, dm), F32), pltpu.VMEM((tm, dm), BF16), pltpu.VMEM((tm, dm), F32)],
        compiler_params=_cparams(("parallel", "arbitrary")),
        name="layer_tail",
    )(*mix, h, w_out, g_post, g_ffn_pre, w1, w2, g_ffn_post)


def _rope_tables(pos):
    half = HEAD_DIM_AB // 2
    inv_freq = jnp.power(ROPE_BASE, -jnp.linspace(0.0, 1.0, half, dtype=F32))
    ang = pos.astype(F32)[:, None] * inv_freq[None, :]
    cos, sin = jnp.cos(ang), jnp.sin(ang)
    return jnp.concatenate([cos, cos], axis=-1), jnp.concatenate([-sin, sin], axis=-1)


def _pad_rows(a, rows):
    return jnp.pad(a, ((0, 0), (0, rows - a.shape[1]), (0, 0)))


def kernel(x_prompt, x_sample, cache_sb_k, cache_sb_v, state_ret, cache_swa_k, cache_swa_v, page_table,
           w_in_ab, w_out_ab, sb_bias, w_in_c, w_out_c, w_ff1, w_ff2, g_mix_pre, g_mix_post, g_ffn_pre, g_ffn_post):
    bp, t, dm = x_prompt.shape
    bs, ts, _ = x_sample.shape
    assert bp == 1 and ts <= DEC_ROWS and t % RET_CHUNK == 0
    n_pages = page_table.shape[1]
    page = cache_sb_k.shape[1]
    past_len = n_pages * page
    d = HEAD_DIM_AB
    gain = lambda g, layer: g[layer][None, :].astype(F32)
    w_in_ab_bf, w_out_ab_bf, w_in_c_bf, w_out_c_bf = _bf(w_in_ab), _bf(w_out_ab), _bf(w_in_c), _bf(w_out_c)
    w_ff1_bf, w_ff2_bf = _bf(w_ff1), _bf(w_ff2)

    h_p = x_prompt.reshape(t, dm)
    h_s = x_sample.reshape(bs * ts, dm)

    cos_p, sin_p = _rope_tables(jnp.arange(t, dtype=jnp.int32))
    rq, rk, rv, rg, q_hm, k_hm, v_hm, sb_k_p, sb_v_p = _inproj_ab(h_p, gain(g_mix_pre, 0), w_in_ab_bf, cos_p, sin_p)
    seq = lambda a: a.reshape(1, t, RET_W)
    ret_zero = jnp.zeros((1, RET_HEADS, d, d), F32)
    y_ret_p, ret_p = _retention(seq(rq), seq(rk), seq(rv), seq(rg), ret_zero, float(RET_CHUNK))
    o_sb_p = _sb_prompt(q_hm, k_hm, v_hm, sb_bias.astype(F32))
    h_p = _layer_tail([y_ret_p.reshape(t, RET_W), o_sb_p], _mix_concat, h_p, w_out_ab_bf, gain(g_mix_post, 0),
                      gain(g_ffn_pre, 0), w_ff1_bf[0], w_ff2_bf[0], gain(g_ffn_post, 0))

    cos_s, sin_s = _rope_tables(past_len + jnp.arange(ts, dtype=jnp.int32))
    tile_s = lambda a: jnp.tile(a, (bs, 1))
    rq, rk, rv, rg, q_hm, _, _, sb_k_s, sb_v_s = _inproj_ab(h_s, gain(g_mix_pre, 0), w_in_ab_bf,
                                                            tile_s(cos_s), tile_s(sin_s))
    dec = lambda a: _pad_rows(a.reshape(bs, ts, a.shape[-1]), DEC_ROWS)
    y_ret_s, ret_s = _retention(dec(rq), dec(rk), dec(rv), dec(rg), state_ret.astype(F32), float(ts))
    sq_s = jnp.transpose(q_hm.astype(F32), (1, 0, 2)).reshape(bs * ts, SB_W)
    o_sb_s = _sb_decode(dec(sq_s), dec(sb_k_s.reshape(bs * ts, SB_W)), dec(sb_v_s.reshape(bs * ts, SB_W)),
                        cache_sb_k, cache_sb_v, page_table, sb_bias.astype(F32))
    undec = lambda a: a[:, :ts].reshape(bs * ts, a.shape[-1])
    h_s = _layer_tail([undec(y_ret_s), undec(o_sb_s)], _mix_concat, h_s, w_out_ab_bf, gain(g_mix_post, 0),
                      gain(g_ffn_pre, 0), w_ff1_bf[0], w_ff2_bf[0], gain(g_ffn_post, 0))

    q, k_f32, v_f32 = _inproj_c(h_p, gain(g_mix_pre, 1), w_in_c_bf)
    h_p = _layer_tail([_dil_prompt(q, k_f32, v_f32)], _mix_concat, h_p, w_out_c_bf, gain(g_mix_post, 1),
                      gain(g_ffn_pre, 1), w_ff1_bf[1], w_ff2_bf[1], gain(g_ffn_post, 1))
    keep_p = min(DIL_WINDOW_MAX, t)
    swa_k_p = k_f32[t - keep_p:].reshape(1, keep_p, DIL_HEADS, DIL_HEAD_DIM)
    swa_v_p = v_f32[t - keep_p:].reshape(1, keep_p, DIL_HEADS, DIL_HEAD_DIM)

    q, k_f32, v_f32 = _inproj_c(h_s, gain(g_mix_pre, 1), w_in_c_bf)
    per_seq = lambda a: a.reshape(bs, ts, DIL_W)
    o_dil_s, swa_k_s, swa_v_s = _dil_decode(per_seq(q.astype(F32)), per_seq(k_f32), per_seq(v_f32),
                                            cache_swa_k.astype(F32), cache_swa_v.astype(F32))
    h_s = _layer_tail([o_dil_s.reshape(bs * ts, DIL_W)], _mix_concat, h_s, w_out_c_bf, gain(g_mix_post, 1),
                      gain(g_ffn_pre, 1), w_ff1_bf[1], w_ff2_bf[1], gain(g_ffn_post, 1))

    heads_ab = lambda a, b: a.reshape(b, -1, SB_HEADS, d)
    return (h_p.reshape(1, t, dm), h_s.reshape(bs, ts, dm),
            heads_ab(sb_k_p, 1), heads_ab(sb_v_p, 1), ret_p,
            swa_k_p, swa_v_p,
            heads_ab(sb_k_s, bs), heads_ab(sb_v_s, bs), ret_s,
            swa_k_s, swa_v_s)
```

```python
import functools

import numpy as np
import jax
import jax.numpy as jnp
from jax import lax
from jax.experimental import pallas as pl
from jax.experimental.pallas import tpu as pltpu

F32 = jnp.float32
BF16 = jnp.bfloat16

NORM_EPS = 1e-6
ROPE_BASE = 10000.0
RET_HEADS = 4
SB_HEADS = 4
HEAD_DIM_AB = 128
RET_W = RET_HEADS * HEAD_DIM_AB
SB_W = SB_HEADS * HEAD_DIM_AB
RET_CHUNK = 128
DIL_HEADS = 16
DIL_HEAD_DIM = 64
DIL_W = DIL_HEADS * DIL_HEAD_DIM
DIL_BRANCHES = ((128, 1), (512, 4), (2048, 16))
DIL_WINDOW_MAX = 2048
DIL_SPAN = 128
LANES = 128
DEC_ROWS = 8
INPROJ_ROWS = 512
VMEM_LIMIT = 56 * 1024 * 1024

_NT = (((1,), (1,)), ((), ()))
_TN = (((0,), (0,)), ((), ()))


def _cparams(sem):
    return pltpu.CompilerParams(dimension_semantics=sem, vmem_limit_bytes=VMEM_LIMIT)


def _bf(x):
    return x.astype(BF16)


def _dot(a, b):
    return jnp.dot(a, b, preferred_element_type=F32)


def _rms(x, gain):
    return x * lax.rsqrt(jnp.mean(x * x, axis=-1, keepdims=True) + NORM_EPS) * gain


def _softplus(z):
    return jnp.maximum(z, 0.0) + jnp.log1p(jnp.exp(-jnp.abs(z)))


def _inproj_ab_kernel(x_ref, g_ref, w_ref, cos_ref, sin_ref,
                      rq_ref, rk_ref, rv_ref, rg_ref, q_hm, k_hm, v_hm, sbk_ref, sbv_ref):
    a = _rms(x_ref[...], g_ref[...])
    p = _dot(_bf(a), w_ref[...])
    cosf = cos_ref[...]
    sinf = sin_ref[...]
    d = HEAD_DIM_AB
    for h in range(RET_HEADS):
        qh = p[:, h * d:(h + 1) * d]
        kh = p[:, RET_W + h * d:RET_W + (h + 1) * d]
        rq_ref[:, h * d:(h + 1) * d] = qh * cosf + pltpu.roll(qh, d // 2, 1) * sinf
        rk_ref[:, h * d:(h + 1) * d] = (kh * cosf + pltpu.roll(kh, d // 2, 1) * sinf) * (d ** -0.5)
    rv_ref[...] = p[:, 2 * RET_W:3 * RET_W]
    rg_ref[...] = p[:, 3 * RET_W:4 * RET_W]
    base = 4 * RET_W
    for h in range(SB_HEADS):
        q_hm[h] = _bf(p[:, base + h * d:base + (h + 1) * d] * (d ** -0.5 * LOG2E))
        k_hm[h] = _bf(p[:, base + SB_W + h * d:base + SB_W + (h + 1) * d])
        v_hm[h] = _bf(p[:, base + 2 * SB_W + h * d:base + 2 * SB_W + (h + 1) * d])
    tm = p.shape[0]
    for h in range(SB_HEADS):
        sbk_ref[pl.ds(h, tm, stride=SB_HEADS), :] = p[:, base + SB_W + h * d:base + SB_W + (h + 1) * d]
        sbv_ref[pl.ds(h, tm, stride=SB_HEADS), :] = p[:, base + 2 * SB_W + h * d:base + 2 * SB_W + (h + 1) * d]


def _inproj_ab(x, gain, w_bf, cosf, sinf):
    m, dm = x.shape
    n = w_bf.shape[1]
    tm = min(m, INPROJ_ROWS)
    row = lambda w: pl.BlockSpec((tm, w), lambda i: (i, 0))
    hm = pl.BlockSpec((SB_HEADS, tm, HEAD_DIM_AB), lambda i: (0, i, 0))
    f32_rows = jax.ShapeDtypeStruct((m, RET_W), F32)
    hm_shape = jax.ShapeDtypeStruct((SB_HEADS, m, HEAD_DIM_AB), BF16)
    return pl.pallas_call(
        _inproj_ab_kernel,
        grid=(m // tm,),
        in_specs=[row(dm), pl.BlockSpec((1, dm), lambda i: (0, 0)),
                  pl.BlockSpec((dm, n), lambda i: (0, 0)), row(HEAD_DIM_AB), row(HEAD_DIM_AB)],
        out_specs=[row(RET_W)] * 4 + [hm] * 3 + [pl.BlockSpec((tm * SB_HEADS, HEAD_DIM_AB), lambda i: (i, 0))] * 2,
        out_shape=[f32_rows] * 4 + [hm_shape] * 3 + [jax.ShapeDtypeStruct((m * SB_HEADS, HEAD_DIM_AB), F32)] * 2,
        compiler_params=_cparams(("parallel",)),
        name="inproj_ab",
    )(x, gain, w_bf, cosf, sinf)


def _ret_kernel(rq, rk, rv, rg, s0, intra, qdec, kdec, cdec, y_ref, s_out, s_scr, *, rows, n_chunks):
    step = pl.program_id(1)
    d = HEAD_DIM_AB

    @pl.when(step == 0)
    def _():
        s_scr[...] = s0[0]

    heads = range(RET_HEADS)
    cols = [slice(h * d, (h + 1) * d) for h in heads]
    for c in range(n_chunks):
        rs = slice(c * rows, (c + 1) * rows)
        qs = [rq[0, rs, cs] for cs in cols]
        ks = [rk[0, rs, cs] for cs in cols]
        vs = [rv[0, rs, cs] for cs in cols]
        if rows < RET_CHUNK:
            pad = jnp.zeros((RET_CHUNK - rows, d), F32)
            ks = [jnp.concatenate([k, pad], axis=0) for k in ks]
            vs = [jnp.concatenate([v, pad], axis=0) for v in vs]
        vb = [_bf(v) for v in vs]
        atts = [lax.dot_general(_bf(qs[h]), _bf(ks[h]), _NT, preferred_element_type=F32) for h in heads]
        cross = [_dot(_bf(qs[h] * qdec[h, :rows, :]), _bf(s_scr[h])) for h in heads]
        grow = [lax.dot_general(_bf(ks[h] * kdec[h]), vb[h], _TN, preferred_element_type=F32) for h in heads]
        inner = [_dot(_bf(atts[h] * intra[h, :rows, :]), vb[h]) for h in heads]
        for h in heads:
            s_scr[h] = s_scr[h] * cdec[h] + grow[h]
            o = inner[h] + cross[h]
            g = rg[0, rs, cols[h]]
            xc = o - jnp.mean(o, axis=-1, keepdims=True)
            var = jnp.mean(xc * xc, axis=-1, keepdims=True)
            y_ref[0, rs, cols[h]] = xc * lax.rsqrt(var + NORM_EPS) * (g * jax.nn.sigmoid(g))

    @pl.when(step == pl.num_programs(1) - 1)
    def _():
        s_out[0] = s_scr[...]


def _ret_tables(chunk_len):
    lg = jnp.log1p(-jnp.exp2(-5.0 - jnp.arange(RET_HEADS, dtype=F32)))
    i = jnp.arange(RET_CHUNK, dtype=F32)
    live = i < chunk_len
    rel = i[:, None] - i[None, :]
    intra = jnp.where((rel >= 0) & live[:, None] & live[None, :],
                      jnp.exp(lg[:, None, None] * jnp.maximum(rel, 0.0)), 0.0)
    q_dec = jnp.where(live[None, :], jnp.exp(lg[:, None] * (i[None, :] + 1.0)), 0.0)
    k_dec = jnp.where(live[None, :], jnp.exp(lg[:, None] * (chunk_len - 1.0 - i[None, :])), 0.0)
    c_dec = jnp.exp(lg * chunk_len)
    full = lambda t: jnp.broadcast_to(t[:, :, None], (RET_HEADS, RET_CHUNK, LANES))
    return intra, full(q_dec), full(k_dec), jnp.broadcast_to(c_dec[:, None, None], (RET_HEADS, RET_CHUNK, LANES))


def _retention(rq, rk, rv, rg, s0, chunk_len):
    b, t, _ = rq.shape
    if t % RET_CHUNK == 0:
        rows = RET_CHUNK
        n_chunks = min(8, t // rows)
    else:
        rows, n_chunks = t, 1
    tc = rows * n_chunks
    seq = pl.BlockSpec((1, tc, RET_W), lambda bi, s: (bi, s, 0))
    st = pl.BlockSpec((1, RET_HEADS, HEAD_DIM_AB, HEAD_DIM_AB), lambda bi, s: (bi, 0, 0, 0))
    tab = pl.BlockSpec((RET_HEADS, RET_CHUNK, LANES), lambda bi, s: (0, 0, 0))
    return pl.pallas_call(
        functools.partial(_ret_kernel, rows=rows, n_chunks=n_chunks),
        grid=(b, t // tc),
        in_specs=[seq] * 4 + [st] + [tab] * 4,
        out_specs=[seq, st],
        out_shape=[jax.ShapeDtypeStruct((b, t, RET_W), F32), jax.ShapeDtypeStruct(s0.shape, F32)],
        scratch_shapes=[pltpu.VMEM((RET_HEADS, HEAD_DIM_AB, HEAD_DIM_AB), F32)],
        compiler_params=_cparams(("parallel", "arbitrary")),
        name="retention",
    )(rq, rk, rv, rg, s0, *_ret_tables(chunk_len))


SB_SUB = 256
SB_ROWS = 128
LOG2E = 1.4426950408889634
SB_DECAYED = 152.0


def _sb_tile(q_ref, k_ref, v_ref, tri_ref, zbias, u_scr, sp_scr, w_scr, carry_scr, acc_scr, units, diagonal):
    d = HEAD_DIM_AB
    n_chunks = SB_SUB // SB_ROWS
    keys_of = lambda c: pl.ds(c * SB_SUB, SB_SUB)
    rows_of = lambda g: pl.ds(g * SB_SUB, SB_SUB)
    chunk_of = lambda g, r: pl.ds(g * SB_SUB + r * SB_ROWS, SB_ROWS)
    in_slot = lambda r: pl.ds(r * SB_ROWS, SB_ROWS)
    key = lax.broadcasted_iota(jnp.int32, (SB_ROWS, SB_SUB), 1)
    row = lax.broadcasted_iota(jnp.int32, (SB_ROWS, SB_SUB), 0)

    def mask_of(unit, r):
        c, g = unit
        return key < row + r * SB_ROWS if diagonal and g == c else None

    def scores(unit):
        c, g = unit
        return lax.dot_general(q_ref[rows_of(g), :], k_ref[keys_of(c), :], _NT, preferred_element_type=F32)

    def keep_logs(n, s):
        c, g = units[n]
        slot = n % 2
        for r in range(n_chunks):
            zs = s[r * SB_ROWS:(r + 1) * SB_ROWS, :] + zbias
            sp = jnp.maximum(zs, 0.0) + jnp.log2(1.0 + jnp.exp2(-jnp.abs(zs)))
            carry = carry_scr[chunk_of(g, r), :]
            u_scr[slot, in_slot(r), :] = zs - sp - jnp.tile(carry, (1, SB_SUB // LANES))
            mask = mask_of(units[n], r)
            if mask is not None:
                sp = jnp.where(mask, sp, 0.0)
            sp_scr[slot, in_slot(r), :] = _bf(sp)
            carry_scr[chunk_of(g, r), :] = carry + jnp.sum(sp, axis=1, keepdims=True)
        return _dot(sp_scr[slot], tri_ref[...])

    def weights(n, sums):
        c, g = units[n]
        slot = n % 2
        for r in range(n_chunks):
            w = jnp.exp2(u_scr[slot, in_slot(r), :] - sums[r * SB_ROWS:(r + 1) * SB_ROWS, :])
            mask = mask_of(units[n], r)
            if mask is not None:
                w = jnp.where(mask, w, 0.0)
            w_scr[slot, in_slot(r), :] = _bf(w)
        acc_scr[rows_of(g), :] += _dot(w_scr[slot], v_ref[keys_of(c), :])

    s, sums = {}, {}
    for n in range(len(units) + 2):
        if n < len(units):
            s[n] = scores(units[n])
        if 1 <= n <= len(units):
            sums[n - 1] = keep_logs(n - 1, s.pop(n - 1))
        if n >= 2:
            weights(n - 2, sums.pop(n - 2))


def _sb_prompt_kernel(qi_tab, kj_tab, bias_ref, q_ref, k_ref, v_ref, tri_ref, o_ref,
                      acc_scr, carry_scr, u_scr, sp_scr, w_scr, least_scr, *, tq):
    p = pl.program_id(0)
    qi = qi_tab[p]
    kj = kj_tab[p]
    n_heads = q_ref.shape[0]
    n_sub = tq // SB_SUB

    def tile(h, units, diagonal):
        _sb_tile(q_ref.at[h], k_ref.at[h], v_ref.at[h], tri_ref, bias_ref[h] * LOG2E, u_scr, sp_scr, w_scr,
                 carry_scr.at[h], acc_scr.at[h], units, diagonal)

    def for_heads(fn):
        def body(h, carry):
            fn(h)
            return carry
        lax.fori_loop(0, n_heads, body, 0)

    @pl.when(kj == qi)
    def _():
        acc_scr[...] = jnp.zeros_like(acc_scr)
        carry_scr[...] = jnp.zeros_like(carry_scr)

        def diagonal(h):
            tile(h, [(c, g) for c in reversed(range(n_sub)) for g in range(c, n_sub)], True)
            least_scr[h] = 0.0
        for_heads(diagonal)

    @pl.when(kj < qi)
    def _():
        def earlier(h):
            @pl.when(least_scr[h] <= SB_DECAYED)
            def _():
                tile(h, [(c, g) for c in reversed(range(n_sub)) for g in range(n_sub)], False)
                least = carry_scr[h]
                while least.shape[0] > 8:
                    half = least.shape[0] // 2
                    least = jnp.minimum(least[:half], least[half:])
                least_scr[h] = jnp.min(least)
        for_heads(earlier)

    @pl.when(kj == 0)
    def _():
        o_ref[...] = acc_scr[...].astype(o_ref.dtype)


def _tri_ones(n):
    i = np.arange(n)
    return jnp.asarray((i[:, None] > i[None, :]).astype(np.float32), BF16)


def _sb_prompt(q_hm, k_hm, v_hm, bias):
    nh, t, d = q_hm.shape
    tq = min(1024, t)
    nq = t // tq
    qi_tab = np.concatenate([np.full(i + 1, i) for i in range(nq)]).astype(np.int32)
    kj_tab = np.concatenate([np.arange(i, -1, -1) for i in range(nq)]).astype(np.int32)
    grid_spec = pltpu.PrefetchScalarGridSpec(
        num_scalar_prefetch=2,
        grid=(len(qi_tab),),
        in_specs=[
            pl.BlockSpec(memory_space=pltpu.SMEM),
            pl.BlockSpec((nh, tq, d), lambda p, qt, kt: (0, qt[p], 0)),
            pl.BlockSpec((nh, tq, d), lambda p, qt, kt: (0, kt[p], 0)),
            pl.BlockSpec((nh, tq, d), lambda p, qt, kt: (0, kt[p], 0)),
            pl.BlockSpec((SB_SUB, SB_SUB), lambda p, qt, kt: (0, 0)),
        ],
        out_specs=pl.BlockSpec((nh, tq, d), lambda p, qt, kt: (0, qt[p], 0)),
        scratch_shapes=[pltpu.VMEM((nh, tq, d), F32), pltpu.VMEM((nh, tq, LANES), F32),
                        pltpu.VMEM((2, SB_SUB, SB_SUB), F32),
                        pltpu.VMEM((2, SB_SUB, SB_SUB), BF16), pltpu.VMEM((2, SB_SUB, SB_SUB), BF16),
                        pltpu.SMEM((nh,), F32)],
    )
    return pl.pallas_call(
        functools.partial(_sb_prompt_kernel, tq=tq),
        grid_spec=grid_spec,
        out_shape=jax.ShapeDtypeStruct((nh, t, d), BF16),
        compiler_params=_cparams(("arbitrary",)),
        name="sb_prompt",
    )(jnp.asarray(qi_tab), jnp.asarray(kj_tab), bias, q_hm, k_hm, v_hm, _tri_ones(SB_SUB))


SB_PAGES_PER_STEP = 16


def _sb_decode_kernel(pt_ref, qbd_ref, bias_ref, knew_ref, vnew_ref, trit_ref, *rest, n_pages_step, page):
    k_pages = rest[:n_pages_step]
    v_pages = rest[n_pages_step:2 * n_pages_step]
    o_ref, acc_scr, carry_scr = rest[2 * n_pages_step:]
    g = pl.program_id(1)
    qbd = qbd_ref[0]
    zbias = bias_ref[...] * LOG2E
    trit = trit_ref[...]
    d = HEAD_DIM_AB
    heads = range(SB_HEADS)

    def block(k_heads, v_heads, carry, accs, mask):
        s = functools.reduce(lambda a, c: a + c,
                             [_dot(_bf(k_heads[h]), qbd[h * d:(h + 1) * d, :]) for h in heads])
        zs = s + zbias
        l2 = jnp.log2(1.0 + jnp.exp2(-jnp.abs(zs)))
        sp = jnp.maximum(zs, 0.0) + l2
        if mask is not None:
            sp = jnp.where(mask, sp, 0.0)
        later = _dot(trit[:page, :page], _bf(sp))
        wt = jnp.exp2(jnp.minimum(zs, 0.0) - l2 - later - carry)
        if mask is not None:
            wt = jnp.where(mask, wt, 0.0)
        w = jnp.transpose(wt)
        accs = [accs[h] + _dot(_bf(w[h * DEC_ROWS:(h + 1) * DEC_ROWS, :]), _bf(v_heads[h])) for h in heads]
        return carry + jnp.sum(sp, axis=0, keepdims=True), accs

    def store(carry, accs):
        carry_scr[...] = jnp.broadcast_to(carry, carry_scr.shape)
        for h in heads:
            acc_scr[h * DEC_ROWS:(h + 1) * DEC_ROWS, :] = accs[h]

    @pl.when(g == 0)
    def _():
        pad = jnp.zeros((page - DEC_ROWS, d), F32)
        k_heads = [jnp.concatenate([knew_ref[0, :, h * d:(h + 1) * d], pad], axis=0) for h in heads]
        v_heads = [jnp.concatenate([vnew_ref[0, :, h * d:(h + 1) * d], pad], axis=0) for h in heads]
        key = lax.broadcasted_iota(jnp.int32, (page, LANES), 0)
        qry = lax.broadcasted_iota(jnp.int32, (page, LANES), 1) & (DEC_ROWS - 1)
        store(*block(k_heads, v_heads, jnp.zeros((1, LANES), F32),
                     [jnp.zeros((DEC_ROWS, d), F32) for _ in heads], key < qry))

    unit_pages = trit_ref.shape[0] // page
    n_units = n_pages_step // unit_pages
    pages_of = lambda n: range(n_pages_step - (n + 1) * unit_pages, n_pages_step - n * unit_pages)
    head_rows = lambda ref, h: ref[0, pl.ds(h, page, stride=SB_HEADS), :]

    def scores(n):
        k_unit = jnp.concatenate([jnp.concatenate([head_rows(k_pages[u], h) for h in heads], axis=1)
                                  for u in pages_of(n)], axis=0)
        return _dot(_bf(k_unit), qbd) + zbias

    def keep_logs(zs):
        sp = jnp.maximum(zs, 0.0) + jnp.log2(1.0 + jnp.exp2(-jnp.abs(zs)))
        return zs - sp, _dot(trit, _bf(sp)), jnp.sum(sp, axis=0, keepdims=True)

    def weights(n, logs, carry, accs):
        log_beta, later, total = logs
        wt = jnp.exp2(log_beta - later - carry)
        w = jnp.concatenate([jnp.transpose(wt[j * LANES:(j + 1) * LANES, :])
                             for j in range(wt.shape[0] // LANES)], axis=1)
        for h in heads:
            v_unit = jnp.concatenate([head_rows(v_pages[u], h) for u in pages_of(n)], axis=0)
            accs[h] = accs[h] + _dot(_bf(w[h * DEC_ROWS:(h + 1) * DEC_ROWS, :]), _bf(v_unit))
        return carry + total, accs

    carry = carry_scr[0:1, :]
    accs = [acc_scr[h * DEC_ROWS:(h + 1) * DEC_ROWS, :] for h in heads]
    zs, logs = {}, {}
    for m in range(n_units + 2):
        if m < n_units:
            zs[m] = scores(m)
        if 1 <= m <= n_units:
            logs[m - 1] = keep_logs(zs.pop(m - 1))
        if m >= 2:
            carry, accs = weights(m - 2, logs.pop(m - 2), carry, accs)
    store(carry, accs)

    @pl.when(g == pl.num_programs(1) - 1)
    def _():
        for h in heads:
            o_ref[0, :, h * d:(h + 1) * d] = acc_scr[h * DEC_ROWS:(h + 1) * DEC_ROWS, :]


def _sb_decode(sq, sk_new, sv_new, cache_k, cache_v, page_table, bias):
    b = sq.shape[0]
    n_pages = page_table.shape[1]
    page = cache_k.shape[1]
    g_pages = min(SB_PAGES_PER_STEP, n_pages)
    n_steps = n_pages // g_pages
    d = HEAD_DIM_AB
    q4 = sq.reshape(b, DEC_ROWS, SB_HEADS, d)
    eye = jnp.eye(SB_HEADS, dtype=F32)
    qbd = jnp.einsum('bihd,hg->bhdgi', q4, eye).reshape(b, SB_W, SB_HEADS * DEC_ROWS)
    qbd = _bf(jnp.pad(qbd, ((0, 0), (0, 0), (0, LANES - SB_HEADS * DEC_ROWS))))
    bias_lane = jnp.pad(jnp.repeat(bias.astype(F32), DEC_ROWS), (0, LANES - SB_HEADS * DEC_ROWS))[None, :]
    unit_keys = page * (2 if g_pages % 2 == 0 else 1)
    i = np.arange(unit_keys)
    trit = jnp.asarray((i[None, :] > i[:, None]).astype(np.float32), BF16)

    def page_spec(u):
        return pl.BlockSpec((1, page * SB_HEADS, d),
                            lambda bi, g, pt: (pt[bi * n_pages + (n_steps - 1 - g) * g_pages + u], 0, 0))

    per_seq = lambda r, w: pl.BlockSpec((1, r, w), lambda bi, g, pt: (bi, 0, 0))
    grid_spec = pltpu.PrefetchScalarGridSpec(
        num_scalar_prefetch=1,
        grid=(b, n_steps),
        in_specs=[per_seq(SB_W, LANES), pl.BlockSpec((1, LANES), lambda bi, g, pt: (0, 0)),
                  per_seq(DEC_ROWS, SB_W), per_seq(DEC_ROWS, SB_W),
                  pl.BlockSpec((unit_keys, unit_keys), lambda bi, g, pt: (0, 0))]
                 + [page_spec(u) for u in range(g_pages)] * 2,
        out_specs=per_seq(DEC_ROWS, SB_W),
        scratch_shapes=[pltpu.VMEM((SB_HEADS * DEC_ROWS, d), F32), pltpu.VMEM((DEC_ROWS, LANES), F32)],
    )
    return pl.pallas_call(
        functools.partial(_sb_decode_kernel, n_pages_step=g_pages, page=page),
        grid_spec=grid_spec,
        out_shape=jax.ShapeDtypeStruct((b, DEC_ROWS, SB_W), F32),
        compiler_params=_cparams(("parallel", "arbitrary")),
        name="sb_decode",
    )(page_table.reshape(-1), qbd, bias_lane, sk_new, sv_new, trit,
      *([cache_k.reshape(-1, page * SB_HEADS, d)] * g_pages), *([cache_v.reshape(-1, page * SB_HEADS, d)] * g_pages))


def _inproj_c_kernel(x_ref, g_ref, w_ref, q_ref, k_ref, v_ref):
    a = _rms(x_ref[...], g_ref[...])
    p = _dot(_bf(a), w_ref[...])
    q_ref[...] = p[:, :DIL_W]
    k_ref[...] = p[:, DIL_W:2 * DIL_W]
    v_ref[...] = p[:, 2 * DIL_W:]


def _inproj_c(x, gain, w_bf):
    m, dm = x.shape
    tm = min(m, INPROJ_ROWS)
    row = pl.BlockSpec((tm, DIL_W), lambda i: (i, 0))
    return pl.pallas_call(
        _inproj_c_kernel,
        grid=(m // tm,),
        in_specs=[pl.BlockSpec((tm, dm), lambda i: (i, 0)), pl.BlockSpec((1, dm), lambda i: (0, 0)),
                  pl.BlockSpec((dm, 3 * DIL_W), lambda i: (0, 0))],
        out_specs=[row] * 3,
        out_shape=[jax.ShapeDtypeStruct((m, DIL_W), F32)] * 3,
        compiler_params=_cparams(("parallel",)),
        name="inproj_c",
    )(x, gain, w_bf)


DIL_SUPER = DIL_WINDOW_MAX
DIL_SKEW = 2


def _dil_prompt_kernel(q_ref, kp_ref, kc_ref, vp_ref, vc_ref, o_ref, o_scr, lse_scr):
    sb = pl.program_id(0)
    blk = DIL_SPAN

    def window(prev_ref, cur_ref, dil, r, j):
        if j > 0:
            return cur_ref[pl.ds(r + dil * blk * (j - 1), 2 * blk, stride=dil), :]
        return jnp.concatenate([prev_ref[pl.ds(DIL_SUPER - dil * blk + r, blk, stride=dil), :],
                                cur_ref[pl.ds(r, blk, stride=dil), :]], axis=0)

    rows = lax.broadcasted_iota(jnp.int32, (blk, 2 * blk), 0)
    cols = lax.broadcasted_iota(jnp.int32, (blk, 2 * blk), 1)
    dist = blk + rows - cols
    in_span = (dist >= 0) & (dist <= DIL_SPAN)
    in_span_first = in_span & ((cols >= blk) | (sb > 0))
    lane = lax.broadcasted_iota(jnp.int32, (blk, LANES), 1)
    first = lane < DIL_HEAD_DIM
    scale = DIL_HEAD_DIM ** -0.5
    units = [(g, dil, r, j) for g, (_, dil) in enumerate(DIL_BRANCHES)
             for j in range(DIL_SUPER // (dil * blk)) for r in range(dil)]

    def scores(unit):
        g, dil, r, j = unit
        q = q_ref[pl.ds(r + dil * blk * j, blk, stride=dil), :]
        k2 = _bf(window(kp_ref, kc_ref, dil, r, j))
        zero = jnp.zeros_like(q)
        return [lax.dot_general(_bf(qh), k2, _NT, preferred_element_type=F32) * scale
                for qh in (jnp.where(first, q, zero), jnp.where(first, zero, q))]

    def attend(unit, zs):
        g, dil, r, j = unit
        v2 = _bf(window(vp_ref, vc_ref, dil, r, j))
        valid = in_span_first if j == 0 else in_span
        outs, lses = [], []
        for z in zs:
            z = jnp.where(valid, z, -jnp.inf)
            m = jnp.max(z, axis=-1, keepdims=True)
            e = jnp.exp(z - m)
            den = jnp.sum(e, axis=-1, keepdims=True)
            outs.append(_dot(_bf(e), v2) / den)
            lses.append(m + jnp.log(den))
        dst = pl.ds(r + dil * blk * j, blk, stride=dil)
        o_scr[g, dst, :] = jnp.where(first, outs[0], outs[1])
        lse_scr[g, dst, :] = jnp.where(first, lses[0], lses[1])

    pending = {}
    for n in range(len(units) + DIL_SKEW):
        if n < len(units):
            pending[n] = scores(units[n])
        if n >= DIL_SKEW:
            attend(units[n - DIL_SKEW], pending.pop(n - DIL_SKEW))

    n_br = len(DIL_BRANCHES)
    for c in range(DIL_SUPER // blk):
        rs = slice(c * blk, (c + 1) * blk)
        lses = [lse_scr[g, rs, :] for g in range(n_br)]
        top = functools.reduce(jnp.maximum, lses)
        ws = [jnp.exp(l - top) for l in lses]
        num = functools.reduce(lambda a, b: a + b, [w * o_scr[g, rs, :] for g, w in enumerate(ws)])
        o_ref[rs, :] = (num / functools.reduce(lambda a, b: a + b, ws)).astype(o_ref.dtype)


def _dil_prompt(q, k, v):
    t = q.shape[0]
    assert t % DIL_SUPER == 0
    cur = pl.BlockSpec((DIL_SUPER, LANES), lambda s, hp: (s, hp))
    prev = pl.BlockSpec((DIL_SUPER, LANES), lambda s, hp: (jnp.maximum(s - 1, 0), hp))
    n_br = len(DIL_BRANCHES)
    return pl.pallas_call(
        _dil_prompt_kernel,
        grid=(t // DIL_SUPER, DIL_W // LANES),
        in_specs=[cur, prev, cur, prev, cur],
        out_specs=cur,
        out_shape=jax.ShapeDtypeStruct((t, DIL_W), BF16),
        scratch_shapes=[pltpu.VMEM((n_br, DIL_SUPER, LANES), F32), pltpu.VMEM((n_br, DIL_SUPER, LANES), F32)],
        compiler_params=_cparams(("parallel", "parallel")),
        name="dilated_prompt",
    )(q, k, k, v, v)


def _dil_multiplicity(dist):
    cnt = jnp.zeros(dist.shape, F32)
    for window, dil in DIL_BRANCHES:
        hit = (dist >= 0) & (dist <= window) & ((dist & (dil - 1)) == 0)
        cnt = cnt + jnp.where(hit, 1.0, 0.0)
    return cnt


def _dil_decode_kernel(q_ref, kt_ref, vt_ref, knew_ref, vnew_ref, o_ref, kout_ref, vout_ref, *, n_past, n_new):
    scale = DIL_HEAD_DIM ** -0.5
    heads = range(kt_ref.shape[1])
    qi = lax.broadcasted_iota(jnp.int32, (DEC_ROWS, n_past), 0)
    key = lax.broadcasted_iota(jnp.int32, (DEC_ROWS, n_past), 1)
    cnt_past = _dil_multiplicity(n_past + qi - key)
    qi_new = lax.broadcasted_iota(jnp.int32, (DEC_ROWS, LANES), 0)
    lane = lax.broadcasted_iota(jnp.int32, (DEC_ROWS, LANES), 1)
    j_new = lane - (LANES - n_new)
    cnt_new = jnp.where(j_new >= 0, _dil_multiplicity(qi_new - j_new), 0.0)

    qs = [_bf(q_ref[0, h]) for h in heads]
    z_past = [_dot(qs[h], _bf(kt_ref[0, h])) * scale for h in heads]
    z_new = [_dot(qs[h], _bf(knew_ref[0, h])) * scale for h in heads]
    ps = []
    for h in heads:
        m = jnp.maximum(jnp.max(jnp.where(cnt_past > 0, z_past[h], -jnp.inf), axis=1, keepdims=True),
                        jnp.max(jnp.where(cnt_new > 0, z_new[h], -jnp.inf), axis=1, keepdims=True))
        p_past = jnp.where(cnt_past > 0, cnt_past * jnp.exp(z_past[h] - m), 0.0)
        p_new = jnp.where(cnt_new > 0, cnt_new * jnp.exp(z_new[h] - m), 0.0)
        den = jnp.sum(p_past, axis=1, keepdims=True) + jnp.sum(p_new, axis=1, keepdims=True)
        ps.append((p_past, p_new, den))
    for h in heads:
        p_past, p_new, den = ps[h]
        num = (lax.dot_general(_bf(p_past), _bf(vt_ref[0, h]), _NT, preferred_element_type=F32)
               + lax.dot_general(_bf(p_new), _bf(vnew_ref[0, h]), _NT, preferred_element_type=F32))
        o_ref[0, h] = num / den

    tail = lax.broadcasted_iota(jnp.int32, (DIL_HEAD_DIM, LANES), 1) >= LANES - n_new
    for src, new, dst in ((kt_ref, knew_ref, kout_ref), (vt_ref, vnew_ref, vout_ref)):
        for h in heads:
            shifted = pltpu.roll(src[0, h], n_past - n_new, 1)
            dst[0, h, :, :n_past - LANES] = shifted[:, :n_past - LANES]
            dst[0, h, :, n_past - LANES:] = jnp.where(tail, new[0, h], shifted[:, n_past - LANES:])


DIL_DEC_HEADS_PER_STEP = 4


def _dil_decode(q, k_new, v_new, cache_k, cache_v):
    b, n_new, _ = q.shape
    n_past = cache_k.shape[1]
    assert n_past == DIL_WINDOW_MAX and n_new <= DEC_ROWS
    hd = (DIL_HEADS, DIL_HEAD_DIM)
    to_t = lambda a: jnp.transpose(a, (0, 2, 3, 1))
    new_t = lambda a: jnp.pad(to_t(a.reshape(b, n_new, *hd)), ((0, 0), (0, 0), (0, 0), (LANES - n_new, 0)))
    q_rows = jnp.pad(jnp.transpose(q.reshape(b, n_new, *hd), (0, 2, 1, 3)),
                     ((0, 0), (0, 0), (0, DEC_ROWS - n_new), (0, 0)))
    hg = DIL_DEC_HEADS_PER_STEP
    spec = lambda r, w: pl.BlockSpec((1, hg, r, w), lambda bi, g: (bi, g, 0, 0))
    cache_shape = jax.ShapeDtypeStruct((b, DIL_HEADS, DIL_HEAD_DIM, n_past), F32)
    o, k_out, v_out = pl.pallas_call(
        functools.partial(_dil_decode_kernel, n_past=n_past, n_new=n_new),
        grid=(b, DIL_HEADS // hg),
        in_specs=[spec(DEC_ROWS, DIL_HEAD_DIM), spec(DIL_HEAD_DIM, n_past), spec(DIL_HEAD_DIM, n_past),
                  spec(DIL_HEAD_DIM, LANES), spec(DIL_HEAD_DIM, LANES)],
        out_specs=[spec(DEC_ROWS, DIL_HEAD_DIM), spec(DIL_HEAD_DIM, n_past), spec(DIL_HEAD_DIM, n_past)],
        out_shape=[jax.ShapeDtypeStruct((b, DIL_HEADS, DEC_ROWS, DIL_HEAD_DIM), F32), cache_shape, cache_shape],
        compiler_params=_cparams(("parallel", "parallel")),
        name="dilated_decode",
    )(q_rows, to_t(cache_k), to_t(cache_v), new_t(k_new), new_t(v_new))
    from_t = lambda a: jnp.transpose(a, (0, 3, 1, 2))
    o = jnp.transpose(o[:, :, :n_new, :], (0, 2, 1, 3)).reshape(b, n_new, DIL_W)
    return o, from_t(k_out), from_t(v_out)


def _mix_concat(*refs):
    parts = []
    for r in refs:
        parts += [_bf(r[h]) for h in range(r.shape[0])] if len(r.shape) == 3 else [_bf(r[...])]
    return jnp.concatenate(parts, axis=-1)


def _tail_kernel(*refs, n_mix, mix_fn):
    mix_refs = refs[:n_mix]
    (h_ref, wout_ref, gpost_ref, gpre_ref, w1_ref, w2_ref, gffn_ref,
     out_ref, h1_scr, a_scr, acc_scr) = refs[n_mix:]
    f = pl.program_id(1)

    @pl.when(f == 0)
    def _():
        m = _dot(mix_fn(*mix_refs), wout_ref[...])
        h1 = h_ref[...] + _rms(m, gpost_ref[...])
        h1_scr[...] = h1
        a_scr[...] = _bf(_rms(h1, gpre_ref[...]))
        acc_scr[...] = jnp.zeros_like(acc_scr)

    hid = jnp.square(jnp.maximum(_dot(a_scr[...], w1_ref[...]), 0.0))
    acc_scr[...] += _dot(_bf(hid), w2_ref[...])

    @pl.when(f == pl.num_programs(1) - 1)
    def _():
        out_ref[...] = h1_scr[...] + _rms(acc_scr[...], gffn_ref[...])


TAIL_ROWS = 1024
TAIL_FF = 512


def _layer_tail(mix, mix_fn, h, w_out, g_post, g_ffn_pre, w1, w2, g_ffn_post):
    m, dm = h.shape
    dff = w1.shape[1]
    tm = min(m, TAIL_ROWS)
    tf = min(dff, TAIL_FF)
    row = lambda w: pl.BlockSpec((tm, w), lambda i, f: (i, 0))
    gain = pl.BlockSpec((1, dm), lambda i, f: (0, 0))
    return pl.pallas_call(
        functools.partial(_tail_kernel, n_mix=len(mix), mix_fn=mix_fn),
        grid=(m // tm, dff // tf),
        in_specs=[row(a.shape[1]) if a.ndim == 2 else pl.BlockSpec((a.shape[0], tm, a.shape[2]), lambda i, f: (0, i, 0))
                  for a in mix]
                 + [row(dm), pl.BlockSpec(w_out.shape, lambda i, f: (0, 0)), gain, gain,
                    pl.BlockSpec((dm, tf), lambda i, f: (0, f)), pl.BlockSpec((tf, dm), lambda i, f: (f, 0)), gain],
        out_specs=row(dm),
        out_shape=jax.ShapeDtypeStruct((m, dm), F32),
        scratch_shapes=[pltpu.VMEM((tm, dm), F32), pltpu.VMEM((tm, dm), BF16), pltpu.VMEM((tm, dm), F32)],
        compiler_params=_cparams(("parallel", "arbitrary")),
        name="layer_tail",
    )(*mix, h, w_out, g_post, g_ffn_pre, w1, w2, g_ffn_post)


def _rope_tables(pos):
    half = HEAD_DIM_AB // 2
    inv_freq = jnp.power(ROPE_BASE, -jnp.linspace(0.0, 1.0, half, dtype=F32))
    ang = pos.astype(F32)[:, None] * inv_freq[None, :]
    cos, sin = jnp.cos(ang), jnp.sin(ang)
    return jnp.concatenate([cos, cos], axis=-1), jnp.concatenate([-sin, sin], axis=-1)


def _pad_rows(a, rows):
    return jnp.pad(a, ((0, 0), (0, rows - a.shape[1]), (0, 0)))


def kernel(x_prompt, x_sample, cache_sb_k, cache_sb_v, state_ret, cache_swa_k, cache_swa_v, page_table,
           w_in_ab, w_out_ab, sb_bias, w_in_c, w_out_c, w_ff1, w_ff2, g_mix_pre, g_mix_post, g_ffn_pre, g_ffn_post):
    bp, t, dm = x_prompt.shape
    bs, ts, _ = x_sample.shape
    assert bp == 1 and ts <= DEC_ROWS and t % RET_CHUNK == 0
    n_pages = page_table.shape[1]
    page = cache_sb_k.shape[1]
    past_len = n_pages * page
    d = HEAD_DIM_AB
    gain = lambda g, layer: g[layer][None, :].astype(F32)
    w_in_ab_bf, w_out_ab_bf, w_in_c_bf, w_out_c_bf = _bf(w_in_ab), _bf(w_out_ab), _bf(w_in_c), _bf(w_out_c)
    w_ff1_bf, w_ff2_bf = _bf(w_ff1), _bf(w_ff2)

    h_p = x_prompt.reshape(t, dm)
    h_s = x_sample.reshape(bs * ts, dm)

    cos_p, sin_p = _rope_tables(jnp.arange(t, dtype=jnp.int32))
    rq, rk, rv, rg, q_hm, k_hm, v_hm, sb_k_p, sb_v_p = _inproj_ab(h_p, gain(g_mix_pre, 0), w_in_ab_bf, cos_p, sin_p)
    seq = lambda a: a.reshape(1, t, RET_W)
    ret_zero = jnp.zeros((1, RET_HEADS, d, d), F32)
    y_ret_p, ret_p = _retention(seq(rq), seq(rk), seq(rv), seq(rg), ret_zero, float(RET_CHUNK))
    o_sb_p = _sb_prompt(q_hm, k_hm, v_hm, sb_bias.astype(F32))
    h_p = _layer_tail([y_ret_p.reshape(t, RET_W), o_sb_p], _mix_concat, h_p, w_out_ab_bf, gain(g_mix_post, 0),
                      gain(g_ffn_pre, 0), w_ff1_bf[0], w_ff2_bf[0], gain(g_ffn_post, 0))

    cos_s, sin_s = _rope_tables(past_len + jnp.arange(ts, dtype=jnp.int32))
    tile_s = lambda a: jnp.tile(a, (bs, 1))
    rq, rk, rv, rg, q_hm, _, _, sb_k_s, sb_v_s = _inproj_ab(h_s, gain(g_mix_pre, 0), w_in_ab_bf,
                                                            tile_s(cos_s), tile_s(sin_s))
    dec = lambda a: _pad_rows(a.reshape(bs, ts, a.shape[-1]), DEC_ROWS)
    y_ret_s, ret_s = _retention(dec(rq), dec(rk), dec(rv), dec(rg), state_ret.astype(F32), float(ts))
    sq_s = jnp.transpose(q_hm.astype(F32), (1, 0, 2)).reshape(bs * ts, SB_W)
    o_sb_s = _sb_decode(dec(sq_s), dec(sb_k_s.reshape(bs * ts, SB_W)), dec(sb_v_s.reshape(bs * ts, SB_W)),
                        cache_sb_k, cache_sb_v, page_table, sb_bias.astype(F32))
    undec = lambda a: a[:, :ts].reshape(bs * ts, a.shape[-1])
    h_s = _layer_tail([undec(y_ret_s), undec(o_sb_s)], _mix_concat, h_s, w_out_ab_bf, gain(g_mix_post, 0),
                      gain(g_ffn_pre, 0), w_ff1_bf[0], w_ff2_bf[0], gain(g_ffn_post, 0))

    q, k_f32, v_f32 = _inproj_c(h_p, gain(g_mix_pre, 1), w_in_c_bf)
    h_p = _layer_tail([_dil_prompt(q, k_f32, v_f32)], _mix_concat, h_p, w_out_c_bf, gain(g_mix_post, 1),
                      gain(g_ffn_pre, 1), w_ff1_bf[1], w_ff2_bf[1], gain(g_ffn_post, 1))
    keep_p = min(DIL_WINDOW_MAX, t)
    swa_k_p = k_f32[t - keep_p:].reshape(1, keep_p, DIL_HEADS, DIL_HEAD_DIM)
    swa_v_p = v_f32[t - keep_p:].reshape(1, keep_p, DIL_HEADS, DIL_HEAD_DIM)

    q, k_f32, v_f32 = _inproj_c(h_s, gain(g_mix_pre, 1), w_in_c_bf)
    per_seq = lambda a: a.reshape(bs, ts, DIL_W)
    o_dil_s, swa_k_s, swa_v_s = _dil_decode(per_seq(q.astype(F32)), per_seq(k_f32), per_seq(v_f32),
                                            cache_swa_k.astype(F32), cache_swa_v.astype(F32))
    h_s = _layer_tail([o_dil_s.reshape(bs * ts, DIL_W)], _mix_concat, h_s, w_out_c_bf, gain(g_mix_post, 1),
                      gain(g_ffn_pre, 1), w_ff1_bf[1], w_ff2_bf[1], gain(g_ffn_post, 1))

    heads_ab = lambda a, b: a.reshape(b, -1, SB_HEADS, d)
    return (h_p.reshape(1, t, dm), h_s.reshape(bs, ts, dm),
            heads_ab(sb_k_p, 1), heads_ab(sb_v_p, 1), ret_p,
            swa_k_p, swa_v_p,
            heads_ab(sb_k_s, bs), heads_ab(sb_v_s, bs), ret_s,
            swa_k_s, swa_v_s)
```

```python
import functools

import numpy as np
import jax
import jax.numpy as jnp
from jax import lax
from jax.experimental import pallas as pl
from jax.experimental.pallas import tpu as pltpu

F32 = jnp.float32
BF16 = jnp.bfloat16

NORM_EPS = 1e-6
ROPE_BASE = 10000.0
RET_HEADS = 4
SB_HEADS = 4
HEAD_DIM_AB = 128
RET_W = RET_HEADS * HEAD_DIM_AB
SB_W = SB_HEADS * HEAD_DIM_AB
RET_CHUNK = 128
DIL_HEADS = 16
DIL_HEAD_DIM = 64
DIL_W = DIL_HEADS * DIL_HEAD_DIM
DIL_BRANCHES = ((128, 1), (512, 4), (2048, 16))
DIL_WINDOW_MAX = 2048
DIL_SPAN = 128
LANES = 128
DEC_ROWS = 8
INPROJ_ROWS = 512
VMEM_LIMIT = 56 * 1024 * 1024

_NT = (((1,), (1,)), ((), ()))
_TN = (((0,), (0,)), ((), ()))


def _cparams(sem):
    return pltpu.CompilerParams(dimension_semantics=sem, vmem_limit_bytes=VMEM_LIMIT)


def _bf(x):
    return x.astype(BF16)


def _dot(a, b):
    return jnp.dot(a, b, preferred_element_type=F32)


def _rms(x, gain):
    return x * lax.rsqrt(jnp.mean(x * x, axis=-1, keepdims=True) + NORM_EPS) * gain


def _softplus(z):
    return jnp.maximum(z, 0.0) + jnp.log1p(jnp.exp(-jnp.abs(z)))


def _inproj_ab_kernel(x_ref, g_ref, w_ref, cos_ref, sin_ref,
                      rq_ref, rk_ref, rv_ref, rg_ref, q_hm, k_hm, v_hm, sbk_ref, sbv_ref):
    a = _rms(x_ref[...], g_ref[...])
    p = _dot(_bf(a), w_ref[...])
    cosf = cos_ref[...]
    sinf = sin_ref[...]
    d = HEAD_DIM_AB
    for h in range(RET_HEADS):
        qh = p[:, h * d:(h + 1) * d]
        kh = p[:, RET_W + h * d:RET_W + (h + 1) * d]
        rq_ref[:, h * d:(h + 1) * d] = qh * cosf + pltpu.roll(qh, d // 2, 1) * sinf
        rk_ref[:, h * d:(h + 1) * d] = (kh * cosf + pltpu.roll(kh, d // 2, 1) * sinf) * (d ** -0.5)
    rv_ref[...] = p[:, 2 * RET_W:3 * RET_W]
    rg_ref[...] = p[:, 3 * RET_W:4 * RET_W]
    base = 4 * RET_W
    for h in range(SB_HEADS):
        q_hm[h] = _bf(p[:, base + h * d:base + (h + 1) * d] * (d ** -0.5 * LOG2E))
        k_hm[h] = _bf(p[:, base + SB_W + h * d:base + SB_W + (h + 1) * d])
        v_hm[h] = _bf(p[:, base + 2 * SB_W + h * d:base + 2 * SB_W + (h + 1) * d])
    tm = p.shape[0]
    for h in range(SB_HEADS):
        sbk_ref[pl.ds(h, tm, stride=SB_HEADS), :] = p[:, base + SB_W + h * d:base + SB_W + (h + 1) * d]
        sbv_ref[pl.ds(h, tm, stride=SB_HEADS), :] = p[:, base + 2 * SB_W + h * d:base + 2 * SB_W + (h + 1) * d]


def _inproj_ab(x, gain, w_bf, cosf, sinf):
    m, dm = x.shape
    n = w_bf.shape[1]
    tm = min(m, INPROJ_ROWS)
    row = lambda w: pl.BlockSpec((tm, w), lambda i: (i, 0))
    hm = pl.BlockSpec((SB_HEADS, tm, HEAD_DIM_AB), lambda i: (0, i, 0))
    f32_rows = jax.ShapeDtypeStruct((m, RET_W), F32)
    hm_shape = jax.ShapeDtypeStruct((SB_HEADS, m, HEAD_DIM_AB), BF16)
    return pl.pallas_call(
        _inproj_ab_kernel,
        grid=(m // tm,),
        in_specs=[row(dm), pl.BlockSpec((1, dm), lambda i: (0, 0)),
                  pl.BlockSpec((dm, n), lambda i: (0, 0)), row(HEAD_DIM_AB), row(HEAD_DIM_AB)],
        out_specs=[row(RET_W)] * 4 + [hm] * 3 + [pl.BlockSpec((tm * SB_HEADS, HEAD_DIM_AB), lambda i: (i, 0))] * 2,
        out_shape=[f32_rows] * 4 + [hm_shape] * 3 + [jax.ShapeDtypeStruct((m * SB_HEADS, HEAD_DIM_AB), F32)] * 2,
        compiler_params=_cparams(("parallel",)),
        name="inproj_ab",
    )(x, gain, w_bf, cosf, sinf)


def _ret_kernel(rq, rk, rv, rg, s0, intra, qdec, kdec, cdec, y_ref, s_out, s_scr, *, rows, n_chunks):
    step = pl.program_id(1)
    d = HEAD_DIM_AB

    @pl.when(step == 0)
    def _():
        s_scr[...] = s0[0]

    heads = range(RET_HEADS)
    cols = [slice(h * d, (h + 1) * d) for h in heads]
    for c in range(n_chunks):
        rs = slice(c * rows, (c + 1) * rows)
        qs = [rq[0, rs, cs] for cs in cols]
        ks = [rk[0, rs, cs] for cs in cols]
        vs = [rv[0, rs, cs] for cs in cols]
        if rows < RET_CHUNK:
            pad = jnp.zeros((RET_CHUNK - rows, d), F32)
            ks = [jnp.concatenate([k, pad], axis=0) for k in ks]
            vs = [jnp.concatenate([v, pad], axis=0) for v in vs]
        vb = [_bf(v) for v in vs]
        atts = [lax.dot_general(_bf(qs[h]), _bf(ks[h]), _NT, preferred_element_type=F32) for h in heads]
        cross = [_dot(_bf(qs[h] * qdec[h, :rows, :]), _bf(s_scr[h])) for h in heads]
        grow = [lax.dot_general(_bf(ks[h] * kdec[h]), vb[h], _TN, preferred_element_type=F32) for h in heads]
        inner = [_dot(_bf(atts[h] * intra[h, :rows, :]), vb[h]) for h in heads]
        for h in heads:
            s_scr[h] = s_scr[h] * cdec[h] + grow[h]
            o = inner[h] + cross[h]
            g = rg[0, rs, cols[h]]
            xc = o - jnp.mean(o, axis=-1, keepdims=True)
            var = jnp.mean(xc * xc, axis=-1, keepdims=True)
            y_ref[0, rs, cols[h]] = xc * lax.rsqrt(var + NORM_EPS) * (g * jax.nn.sigmoid(g))

    @pl.when(step == pl.num_programs(1) - 1)
    def _():
        s_out[0] = s_scr[...]


def _ret_tables(chunk_len):
    lg = jnp.log1p(-jnp.exp2(-5.0 - jnp.arange(RET_HEADS, dtype=F32)))
    i = jnp.arange(RET_CHUNK, dtype=F32)
    live = i < chunk_len
    rel = i[:, None] - i[None, :]
    intra = jnp.where((rel >= 0) & live[:, None] & live[None, :],
                      jnp.exp(lg[:, None, None] * jnp.maximum(rel, 0.0)), 0.0)
    q_dec = jnp.where(live[None, :], jnp.exp(lg[:, None] * (i[None, :] + 1.0)), 0.0)
    k_dec = jnp.where(live[None, :], jnp.exp(lg[:, None] * (chunk_len - 1.0 - i[None, :])), 0.0)
    c_dec = jnp.exp(lg * chunk_len)
    full = lambda t: jnp.broadcast_to(t[:, :, None], (RET_HEADS, RET_CHUNK, LANES))
    return intra, full(q_dec), full(k_dec), jnp.broadcast_to(c_dec[:, None, None], (RET_HEADS, RET_CHUNK, LANES))


def _retention(rq, rk, rv, rg, s0, chunk_len):
    b, t, _ = rq.shape
    if t % RET_CHUNK == 0:
        rows = RET_CHUNK
        n_chunks = min(8, t // rows)
    else:
        rows, n_chunks = t, 1
    tc = rows * n_chunks
    seq = pl.BlockSpec((1, tc, RET_W), lambda bi, s: (bi, s, 0))
    st = pl.BlockSpec((1, RET_HEADS, HEAD_DIM_AB, HEAD_DIM_AB), lambda bi, s: (bi, 0, 0, 0))
    tab = pl.BlockSpec((RET_HEADS, RET_CHUNK, LANES), lambda bi, s: (0, 0, 0))
    return pl.pallas_call(
        functools.partial(_ret_kernel, rows=rows, n_chunks=n_chunks),
        grid=(b, t // tc),
        in_specs=[seq] * 4 + [st] + [tab] * 4,
        out_specs=[seq, st],
        out_shape=[jax.ShapeDtypeStruct((b, t, RET_W), F32), jax.ShapeDtypeStruct(s0.shape, F32)],
        scratch_shapes=[pltpu.VMEM((RET_HEADS, HEAD_DIM_AB, HEAD_DIM_AB), F32)],
        compiler_params=_cparams(("parallel", "arbitrary")),
        name="retention",
    )(rq, rk, rv, rg, s0, *_ret_tables(chunk_len))


SB_SUB = 256
SB_ROWS = 128
LOG2E = 1.4426950408889634
SB_DECAYED = 152.0


def _sb_tile(q_ref, k_ref, v_ref, tri_ref, zbias, u_scr, sp_scr, w_scr, carry_scr, acc_scr, units, diagonal):
    d = HEAD_DIM_AB
    n_chunks = SB_SUB // SB_ROWS
    keys_of = lambda c: pl.ds(c * SB_SUB, SB_SUB)
    rows_of = lambda g: pl.ds(g * SB_SUB, SB_SUB)
    chunk_of = lambda g, r: pl.ds(g * SB_SUB + r * SB_ROWS, SB_ROWS)
    in_slot = lambda r: pl.ds(r * SB_ROWS, SB_ROWS)
    key = lax.broadcasted_iota(jnp.int32, (SB_ROWS, SB_SUB), 1)
    row = lax.broadcasted_iota(jnp.int32, (SB_ROWS, SB_SUB), 0)

    def mask_of(unit, r):
        c, g = unit
        return key < row + r * SB_ROWS if diagonal and g == c else None

    def scores(unit):
        c, g = unit
        return lax.dot_general(q_ref[rows_of(g), :], k_ref[keys_of(c), :], _NT, preferred_element_type=F32)

    def keep_logs(n, s):
        c, g = units[n]
        slot = n % 2
        for r in range(n_chunks):
            zs = s[r * SB_ROWS:(r + 1) * SB_ROWS, :] + zbias
            sp = jnp.maximum(zs, 0.0) + jnp.log2(1.0 + jnp.exp2(-jnp.abs(zs)))
            carry = carry_scr[chunk_of(g, r), :]
            u_scr[slot, in_slot(r), :] = zs - sp - jnp.tile(carry, (1, SB_SUB // LANES))
            mask = mask_of(units[n], r)
            if mask is not None:
                sp = jnp.where(mask, sp, 0.0)
            sp_scr[slot, in_slot(r), :] = _bf(sp)
            carry_scr[chunk_of(g, r), :] = carry + jnp.sum(sp, axis=1, keepdims=True)
        return _dot(sp_scr[slot], tri_ref[...])

    def weights(n, sums):
        c, g = units[n]
        slot = n % 2
        for r in range(n_chunks):
            w = jnp.exp2(u_scr[slot, in_slot(r), :] - sums[r * SB_ROWS:(r + 1) * SB_ROWS, :])
            mask = mask_of(units[n], r)
            if mask is not None:
                w = jnp.where(mask, w, 0.0)
            w_scr[slot, in_slot(r), :] = _bf(w)
        acc_scr[rows_of(g), :] += _dot(w_scr[slot], v_ref[keys_of(c), :])

    s, sums = {}, {}
    for n in range(len(units) + 2):
        if n < len(units):
            s[n] = scores(units[n])
        if 1 <= n <= len(units):
            sums[n - 1] = keep_logs(n - 1, s.pop(n - 1))
        if n >= 2:
            weights(n - 2, sums.pop(n - 2))


def _sb_prompt_kernel(qi_tab, kj_tab, bias_ref, q_ref, k_ref, v_ref, tri_ref, o_ref,
                      acc_scr, carry_scr, u_scr, sp_scr, w_scr, least_scr, *, tq):
    p = pl.program_id(0)
    qi = qi_tab[p]
    kj = kj_tab[p]
    n_heads = q_ref.shape[0]
    n_sub = tq // SB_SUB

    def tile(h, units, diagonal):
        _sb_tile(q_ref.at[h], k_ref.at[h], v_ref.at[h], tri_ref, bias_ref[h] * LOG2E, u_scr, sp_scr, w_scr,
                 carry_scr.at[h], acc_scr.at[h], units, diagonal)

    def for_heads(fn):
        def body(h, carry):
            fn(h)
            return carry
        lax.fori_loop(0, n_heads, body, 0)

    @pl.when(kj == qi)
    def _():
        acc_scr[...] = jnp.zeros_like(acc_scr)
        carry_scr[...] = jnp.zeros_like(carry_scr)

        def diagonal(h):
            tile(h, [(c, g) for c in reversed(range(n_sub)) for g in range(c, n_sub)], True)
            least_scr[h] = 0.0
        for_heads(diagonal)

    @pl.when(kj < qi)
    def _():
        def earlier(h):
            @pl.when(least_scr[h] <= SB_DECAYED)
            def _():
                tile(h, [(c, g) for c in reversed(range(n_sub)) for g in range(n_sub)], False)
                least = carry_scr[h]
                while least.shape[0] > 8:
                    half = least.shape[0] // 2
                    least = jnp.minimum(least[:half], least[half:])
                least_scr[h] = jnp.min(least)
        for_heads(earlier)

    @pl.when(kj == 0)
    def _():
        o_ref[...] = acc_scr[...].astype(o_ref.dtype)


def _tri_ones(n):
    i = np.arange(n)
    return jnp.asarray((i[:, None] > i[None, :]).astype(np.float32), BF16)


def _sb_prompt(q_hm, k_hm, v_hm, bias):
    nh, t, d = q_hm.shape
    tq = min(1024, t)
    nq = t // tq
    qi_tab = np.concatenate([np.full(i + 1, i) for i in range(nq)]).astype(np.int32)
    kj_tab = np.concatenate([np.arange(i, -1, -1) for i in range(nq)]).astype(np.int32)
    grid_spec = pltpu.PrefetchScalarGridSpec(
        num_scalar_prefetch=2,
        grid=(len(qi_tab),),
        in_specs=[
            pl.BlockSpec(memory_space=pltpu.SMEM),
            pl.BlockSpec((nh, tq, d), lambda p, qt, kt: (0, qt[p], 0)),
            pl.BlockSpec((nh, tq, d), lambda p, qt, kt: (0, kt[p], 0)),
            pl.BlockSpec((nh, tq, d), lambda p, qt, kt: (0, kt[p], 0)),
            pl.BlockSpec((SB_SUB, SB_SUB), lambda p, qt, kt: (0, 0)),
        ],
        out_specs=pl.BlockSpec((nh, tq, d), lambda p, qt, kt: (0, qt[p], 0)),
        scratch_shapes=[pltpu.VMEM((nh, tq, d), F32), pltpu.VMEM((nh, tq, LANES), F32),
                        pltpu.VMEM((2, SB_SUB, SB_SUB), F32),
                        pltpu.VMEM((2, SB_SUB, SB_SUB), BF16), pltpu.VMEM((2, SB_SUB, SB_SUB), BF16),
                        pltpu.SMEM((nh,), F32)],
    )
    return pl.pallas_call(
        functools.partial(_sb_prompt_kernel, tq=tq),
        grid_spec=grid_spec,
        out_shape=jax.ShapeDtypeStruct((nh, t, d), BF16),
        compiler_params=_cparams(("arbitrary",)),
        name="sb_prompt",
    )(jnp.asarray(qi_tab), jnp.asarray(kj_tab), bias, q_hm, k_hm, v_hm, _tri_ones(SB_SUB))


SB_PAGES_PER_STEP = 16


def _sb_decode_kernel(pt_ref, q_ref, bias_ref, knew_ref, vnew_ref, tri_ref, *rest, n_pages_step, page):
    k_pages = rest[:n_pages_step]
    v_pages = rest[n_pages_step:2 * n_pages_step]
    o_ref, acc_scr, carry_scr = rest[2 * n_pages_step:]
    g = pl.program_id(1)
    q_rows = q_ref[0]
    zbias = bias_ref[...] * LOG2E
    tri = tri_ref[...]
    d = HEAD_DIM_AB
    heads = range(SB_HEADS)
    n_rows = SB_HEADS * DEC_ROWS

    def scores(k_unit):
        s = lax.dot_general(q_rows, _bf(k_unit), _NT, preferred_element_type=F32)
        return s + jnp.tile(zbias, (1, k_unit.shape[0] // LANES))

    def keep_logs(zs, mask=None):
        sp = jnp.maximum(zs, 0.0) + jnp.log2(1.0 + jnp.exp2(-jnp.abs(zs)))
        log_beta = zs - sp
        if mask is not None:
            sp = jnp.where(mask, sp, 0.0)
        keys = zs.shape[1]
        return log_beta, _dot(_bf(sp), tri[:keys, :keys]), jnp.sum(sp, axis=1, keepdims=True)

    def weights(logs, v_heads, carry, accs, mask=None):
        log_beta, later, total = logs
        w = jnp.exp2(log_beta - later - carry)
        if mask is not None:
            w = jnp.where(mask, w, 0.0)
        accs = [accs[h] + _dot(_bf(w[h * DEC_ROWS:(h + 1) * DEC_ROWS, :]), _bf(v_heads[h])) for h in heads]
        return carry + total, accs

    def store(carry, accs):
        carry_scr[...] = jnp.broadcast_to(carry, carry_scr.shape)
        for h in heads:
            acc_scr[h * DEC_ROWS:(h + 1) * DEC_ROWS, :] = accs[h]

    @pl.when(g == 0)
    def _():
        k_new = jnp.concatenate([knew_ref[0], jnp.zeros((page - DEC_ROWS, SB_W), F32)], axis=0)
        pad = jnp.zeros((page - DEC_ROWS, d), F32)
        v_heads = [jnp.concatenate([vnew_ref[0, :, h * d:(h + 1) * d], pad], axis=0) for h in heads]
        key = lax.broadcasted_iota(jnp.int32, (n_rows, page), 1)
        qry = lax.broadcasted_iota(jnp.int32, (n_rows, page), 0) & (DEC_ROWS - 1)
        mask = key < qry
        store(*weights(keep_logs(scores(k_new), mask), v_heads, jnp.zeros((n_rows, 1), F32),
                       [jnp.zeros((DEC_ROWS, d), F32) for _ in heads], mask))

    unit_pages = tri_ref.shape[0] // page
    n_units = n_pages_step // unit_pages
    pages_of = lambda n: range(n_pages_step - (n + 1) * unit_pages, n_pages_step - n * unit_pages)
    head_rows = lambda ref, h: ref[0, pl.ds(h, page, stride=SB_HEADS), :]

    def unit_keys(n):
        return jnp.concatenate([jnp.concatenate([head_rows(k_pages[u], h) for h in heads], axis=1)
                                for u in pages_of(n)], axis=0)

    def unit_values(n):
        return [jnp.concatenate([head_rows(v_pages[u], h) for u in pages_of(n)], axis=0) for h in heads]

    carry = carry_scr[:, 0:1]
    accs = [acc_scr[h * DEC_ROWS:(h + 1) * DEC_ROWS, :] for h in heads]
    zs, logs = {}, {}
    for m in range(n_units + 2):
        if m < n_units:
            zs[m] = scores(unit_keys(m))
        if 1 <= m <= n_units:
            logs[m - 1] = keep_logs(zs.pop(m - 1))
        if m >= 2:
            carry, accs = weights(logs.pop(m - 2), unit_values(m - 2), carry, accs)
    store(carry, accs)

    @pl.when(g == pl.num_programs(1) - 1)
    def _():
        for h in heads:
            o_ref[0, :, h * d:(h + 1) * d] = acc_scr[h * DEC_ROWS:(h + 1) * DEC_ROWS, :]


def _sb_decode(sq, sk_new, sv_new, cache_k, cache_v, page_table, bias):
    b = sq.shape[0]
    n_pages = page_table.shape[1]
    page = cache_k.shape[1]
    g_pages = min(SB_PAGES_PER_STEP, n_pages)
    n_steps = n_pages // g_pages
    d = HEAD_DIM_AB
    n_rows = SB_HEADS * DEC_ROWS
    q4 = sq.reshape(b, DEC_ROWS, SB_HEADS, d)
    eye = jnp.eye(SB_HEADS, dtype=F32)
    q_rows = _bf(jnp.einsum('bihd,hg->bgihd', q4, eye).reshape(b, n_rows, SB_W))
    bias_rows = jnp.broadcast_to(jnp.repeat(bias.astype(F32), DEC_ROWS)[:, None], (n_rows, LANES))
    unit_keys = page * (2 if g_pages % 2 == 0 else 1)

    def page_spec(u):
        return pl.BlockSpec((1, page * SB_HEADS, d),
                            lambda bi, g, pt: (pt[bi * n_pages + (n_steps - 1 - g) * g_pages + u], 0, 0))

    per_seq = lambda r, w: pl.BlockSpec((1, r, w), lambda bi, g, pt: (bi, 0, 0))
    grid_spec = pltpu.PrefetchScalarGridSpec(
        num_scalar_prefetch=1,
        grid=(b, n_steps),
        in_specs=[per_seq(n_rows, SB_W), pl.BlockSpec((n_rows, LANES), lambda bi, g, pt: (0, 0)),
                  per_seq(DEC_ROWS, SB_W), per_seq(DEC_ROWS, SB_W),
                  pl.BlockSpec((unit_keys, unit_keys), lambda bi, g, pt: (0, 0))]
                 + [page_spec(u) for u in range(g_pages)] * 2,
        out_specs=per_seq(DEC_ROWS, SB_W),
        scratch_shapes=[pltpu.VMEM((n_rows, d), F32), pltpu.VMEM((n_rows, LANES), F32)],
    )
    return pl.pallas_call(
        functools.partial(_sb_decode_kernel, n_pages_step=g_pages, page=page),
        grid_spec=grid_spec,
        out_shape=jax.ShapeDtypeStruct((b, DEC_ROWS, SB_W), F32),
        compiler_params=_cparams(("parallel", "arbitrary")),
        name="sb_decode",
    )(page_table.reshape(-1), q_rows, bias_rows, sk_new, sv_new, _tri_ones(unit_keys),
      *([cache_k.reshape(-1, page * SB_HEADS, d)] * g_pages), *([cache_v.reshape(-1, page * SB_HEADS, d)] * g_pages))


def _inproj_c_kernel(x_ref, g_ref, w_ref, q_ref, k_ref, v_ref):
    a = _rms(x_ref[...], g_ref[...])
    p = _dot(_bf(a), w_ref[...])
    q_ref[...] = p[:, :DIL_W]
    k_ref[...] = p[:, DIL_W:2 * DIL_W]
    v_ref[...] = p[:, 2 * DIL_W:]


def _inproj_c(x, gain, w_bf):
    m, dm = x.shape
    tm = min(m, INPROJ_ROWS)
    row = pl.BlockSpec((tm, DIL_W), lambda i: (i, 0))
    return pl.pallas_call(
        _inproj_c_kernel,
        grid=(m // tm,),
        in_specs=[pl.BlockSpec((tm, dm), lambda i: (i, 0)), pl.BlockSpec((1, dm), lambda i: (0, 0)),
                  pl.BlockSpec((dm, 3 * DIL_W), lambda i: (0, 0))],
        out_specs=[row] * 3,
        out_shape=[jax.ShapeDtypeStruct((m, DIL_W), F32)] * 3,
        compiler_params=_cparams(("parallel",)),
        name="inproj_c",
    )(x, gain, w_bf)


DIL_SUPER = DIL_WINDOW_MAX
DIL_SKEW = 2


def _dil_prompt_kernel(q_ref, kp_ref, kc_ref, vp_ref, vc_ref, o_ref, o_scr, lse_scr):
    sb = pl.program_id(0)
    blk = DIL_SPAN

    def window(prev_ref, cur_ref, dil, r, j):
        if j > 0:
            return cur_ref[pl.ds(r + dil * blk * (j - 1), 2 * blk, stride=dil), :]
        return jnp.concatenate([prev_ref[pl.ds(DIL_SUPER - dil * blk + r, blk, stride=dil), :],
                                cur_ref[pl.ds(r, blk, stride=dil), :]], axis=0)

    rows = lax.broadcasted_iota(jnp.int32, (blk, 2 * blk), 0)
    cols = lax.broadcasted_iota(jnp.int32, (blk, 2 * blk), 1)
    dist = blk + rows - cols
    in_span = (dist >= 0) & (dist <= DIL_SPAN)
    in_span_first = in_span & ((cols >= blk) | (sb > 0))
    lane = lax.broadcasted_iota(jnp.int32, (blk, LANES), 1)
    first = lane < DIL_HEAD_DIM
    scale = DIL_HEAD_DIM ** -0.5
    units = [(g, dil, r, j) for g, (_, dil) in enumerate(DIL_BRANCHES)
             for j in range(DIL_SUPER // (dil * blk)) for r in range(dil)]

    def scores(unit):
        g, dil, r, j = unit
        q = q_ref[pl.ds(r + dil * blk * j, blk, stride=dil), :] * scale
        k2 = _bf(window(kp_ref, kc_ref, dil, r, j))
        zero = jnp.zeros_like(q)
        return [lax.dot_general(_bf(qh), k2, _NT, preferred_element_type=F32)
                for qh in (jnp.where(first, q, zero), jnp.where(first, zero, q))]

    def attend(unit, zs):
        g, dil, r, j = unit
        v2 = _bf(window(vp_ref, vc_ref, dil, r, j))
        valid = in_span_first if j == 0 else in_span
        nums, tops, dens = [], [], []
        for z in zs:
            z = jnp.where(valid, z, -jnp.inf)
            m = jnp.max(z, axis=-1, keepdims=True)
            e = jnp.exp(z - m)
            dens.append(jnp.sum(e, axis=-1, keepdims=True))
            tops.append(m)
            nums.append(_dot(_bf(e), v2))
        den = jnp.where(first, dens[0], dens[1])
        dst = pl.ds(r + dil * blk * j, blk, stride=dil)
        o_scr[g, dst, :] = jnp.where(first, nums[0], nums[1]) / den
        lse_scr[g, dst, :] = jnp.where(first, tops[0], tops[1]) + jnp.log(den)

    pending = {}
    for n in range(len(units) + DIL_SKEW):
        if n < len(units):
            pending[n] = scores(units[n])
        if n >= DIL_SKEW:
            attend(units[n - DIL_SKEW], pending.pop(n - DIL_SKEW))

    n_br = len(DIL_BRANCHES)
    for c in range(DIL_SUPER // blk):
        rs = slice(c * blk, (c + 1) * blk)
        lses = [lse_scr[g, rs, :] for g in range(n_br)]
        top = functools.reduce(jnp.maximum, lses)
        ws = [jnp.exp(l - top) for l in lses]
        num = functools.reduce(lambda a, b: a + b, [w * o_scr[g, rs, :] for g, w in enumerate(ws)])
        o_ref[rs, :] = (num / functools.reduce(lambda a, b: a + b, ws)).astype(o_ref.dtype)


def _dil_prompt(q, k, v):
    t = q.shape[0]
    assert t % DIL_SUPER == 0
    cur = pl.BlockSpec((DIL_SUPER, LANES), lambda s, hp: (s, hp))
    prev = pl.BlockSpec((DIL_SUPER, LANES), lambda s, hp: (jnp.maximum(s - 1, 0), hp))
    n_br = len(DIL_BRANCHES)
    return pl.pallas_call(
        _dil_prompt_kernel,
        grid=(t // DIL_SUPER, DIL_W // LANES),
        in_specs=[cur, prev, cur, prev, cur],
        out_specs=cur,
        out_shape=jax.ShapeDtypeStruct((t, DIL_W), BF16),
        scratch_shapes=[pltpu.VMEM((n_br, DIL_SUPER, LANES), F32), pltpu.VMEM((n_br, DIL_SUPER, LANES), F32)],
        compiler_params=_cparams(("parallel", "parallel")),
        name="dilated_prompt",
    )(q, k, k, v, v)


def _dil_multiplicity(dist):
    cnt = jnp.zeros(dist.shape, F32)
    for window, dil in DIL_BRANCHES:
        hit = (dist >= 0) & (dist <= window) & ((dist & (dil - 1)) == 0)
        cnt = cnt + jnp.where(hit, 1.0, 0.0)
    return cnt


def _dil_decode_kernel(q_ref, kt_ref, vt_ref, knew_ref, vnew_ref, o_ref, kout_ref, vout_ref, *, n_past, n_new):
    scale = DIL_HEAD_DIM ** -0.5
    heads = range(kt_ref.shape[1])
    qi = lax.broadcasted_iota(jnp.int32, (DEC_ROWS, n_past), 0)
    key = lax.broadcasted_iota(jnp.int32, (DEC_ROWS, n_past), 1)
    cnt_past = _dil_multiplicity(n_past + qi - key)
    qi_new = lax.broadcasted_iota(jnp.int32, (DEC_ROWS, LANES), 0)
    lane = lax.broadcasted_iota(jnp.int32, (DEC_ROWS, LANES), 1)
    j_new = lane - (LANES - n_new)
    cnt_new = jnp.where(j_new >= 0, _dil_multiplicity(qi_new - j_new), 0.0)

    qs = [_bf(q_ref[0, h]) for h in heads]
    z_past = [_dot(qs[h], _bf(kt_ref[0, h])) * scale for h in heads]
    z_new = [_dot(qs[h], _bf(knew_ref[0, h])) * scale for h in heads]
    ps = []
    for h in heads:
        m = jnp.maximum(jnp.max(jnp.where(cnt_past > 0, z_past[h], -jnp.inf), axis=1, keepdims=True),
                        jnp.max(jnp.where(cnt_new > 0, z_new[h], -jnp.inf), axis=1, keepdims=True))
        p_past = jnp.where(cnt_past > 0, cnt_past * jnp.exp(z_past[h] - m), 0.0)
        p_new = jnp.where(cnt_new > 0, cnt_new * jnp.exp(z_new[h] - m), 0.0)
        den = jnp.sum(p_past, axis=1, keepdims=True) + jnp.sum(p_new, axis=1, keepdims=True)
        ps.append((p_past, p_new, den))
    for h in heads:
        p_past, p_new, den = ps[h]
        num = (lax.dot_general(_bf(p_past), _bf(vt_ref[0, h]), _NT, preferred_element_type=F32)
               + lax.dot_general(_bf(p_new), _bf(vnew_ref[0, h]), _NT, preferred_element_type=F32))
        o_ref[0, h] = num / den

    tail = lax.broadcasted_iota(jnp.int32, (DIL_HEAD_DIM, LANES), 1) >= LANES - n_new
    for src, new, dst in ((kt_ref, knew_ref, kout_ref), (vt_ref, vnew_ref, vout_ref)):
        for h in heads:
            shifted = pltpu.roll(src[0, h], n_past - n_new, 1)
            dst[0, h, :, :n_past - LANES] = shifted[:, :n_past - LANES]
            dst[0, h, :, n_past - LANES:] = jnp.where(tail, new[0, h], shifted[:, n_past - LANES:])


DIL_DEC_HEADS_PER_STEP = 4


def _dil_decode(q, k_new, v_new, cache_k, cache_v):
    b, n_new, _ = q.shape
    n_past = cache_k.shape[1]
    assert n_past == DIL_WINDOW_MAX and n_new <= DEC_ROWS
    hd = (DIL_HEADS, DIL_HEAD_DIM)
    to_t = lambda a: jnp.transpose(a, (0, 2, 3, 1))
    new_t = lambda a: jnp.pad(to_t(a.reshape(b, n_new, *hd)), ((0, 0), (0, 0), (0, 0), (LANES - n_new, 0)))
    q_rows = jnp.pad(jnp.transpose(q.reshape(b, n_new, *hd), (0, 2, 1, 3)),
                     ((0, 0), (0, 0), (0, DEC_ROWS - n_new), (0, 0)))
    hg = DIL_DEC_HEADS_PER_STEP
    spec = lambda r, w: pl.BlockSpec((1, hg, r, w), lambda bi, g: (bi, g, 0, 0))
    cache_shape = jax.ShapeDtypeStruct((b, DIL_HEADS, DIL_HEAD_DIM, n_past), F32)
    o, k_out, v_out = pl.pallas_call(
        functools.partial(_dil_decode_kernel, n_past=n_past, n_new=n_new),
        grid=(b, DIL_HEADS // hg),
        in_specs=[spec(DEC_ROWS, DIL_HEAD_DIM), spec(DIL_HEAD_DIM, n_past), spec(DIL_HEAD_DIM, n_past),
                  spec(DIL_HEAD_DIM, LANES), spec(DIL_HEAD_DIM, LANES)],
        out_specs=[spec(DEC_ROWS, DIL_HEAD_DIM), spec(DIL_HEAD_DIM, n_past), spec(DIL_HEAD_DIM, n_past)],
        out_shape=[jax.ShapeDtypeStruct((b, DIL_HEADS, DEC_ROWS, DIL_HEAD_DIM), F32), cache_shape, cache_shape],
        compiler_params=_cparams(("parallel", "parallel")),
        name="dilated_decode",
    )(q_rows, to_t(cache_k), to_t(cache_v), new_t(k_new), new_t(v_new))
    from_t = lambda a: jnp.transpose(a, (0, 3, 1, 2))
    o = jnp.transpose(o[:, :, :n_new, :], (0, 2, 1, 3)).reshape(b, n_new, DIL_W)
    return o, from_t(k_out), from_t(v_out)


def _mix_concat(*refs):
    parts = []
    for r in refs:
        parts += [_bf(r[h]) for h in range(r.shape[0])] if len(r.shape) == 3 else [_bf(r[...])]
    return jnp.concatenate(parts, axis=-1)


def _tail_kernel(*refs, n_mix, mix_fn):
    mix_refs = refs[:n_mix]
    (h_ref, wout_ref, gpost_ref, gpre_ref, w1_ref, w2_ref, gffn_ref,
     out_ref, h1_scr, a_scr, acc_scr) = refs[n_mix:]
    f = pl.program_id(1)

    @pl.when(f == 0)
    def _():
        m = _dot(mix_fn(*mix_refs), wout_ref[...])
        h1 = h_ref[...] + _rms(m, gpost_ref[...])
        h1_scr[...] = h1
        a_scr[...] = _bf(_rms(h1, gpre_ref[...]))
        acc_scr[...] = jnp.zeros_like(acc_scr)

    hid = jnp.square(jnp.maximum(_dot(a_scr[...], w1_ref[...]), 0.0))
    acc_scr[...] += _dot(_bf(hid), w2_ref[...])

    @pl.when(f == pl.num_programs(1) - 1)
    def _():
        out_ref[...] = h1_scr[...] + _rms(acc_scr[...], gffn_ref[...])


TAIL_ROWS = 1024
TAIL_FF = 512


def _layer_tail(mix, mix_fn, h, w_out, g_post, g_ffn_pre, w1, w2, g_ffn_post):
    m, dm = h.shape
    dff = w1.shape[1]
    tm = min(m, TAIL_ROWS)
    tf = min(dff, TAIL_FF)
    row = lambda w: pl.BlockSpec((tm, w), lambda i, f: (i, 0))
    gain = pl.BlockSpec((1, dm), lambda i, f: (0, 0))
    return pl.pallas_call(
        functools.partial(_tail_kernel, n_mix=len(mix), mix_fn=mix_fn),
        grid=(m // tm, dff // tf),
        in_specs=[row(a.shape[1]) if a.ndim == 2 else pl.BlockSpec((a.shape[0], tm, a.shape[2]), lambda i, f: (0, i, 0))
                  for a in mix]
                 + [row(dm), pl.BlockSpec(w_out.shape, lambda i, f: (0, 0)), gain, gain,
                    pl.BlockSpec((dm, tf), lambda i, f: (0, f)), pl.BlockSpec((tf, dm), lambda i, f: (f, 0)), gain],
        out_specs=row(dm),
        out_shape=jax.ShapeDtypeStruct((m, dm), F32),
        scratch_shapes=[pltpu.VMEM((tm, dm), F32), pltpu.VMEM((tm, dm), BF16), pltpu.VMEM((tm, dm), F32)],
        compiler_params=_cparams(("parallel", "arbitrary")),
        name="layer_tail",
    )(*mix, h, w_out, g_post, g_ffn_pre, w1, w2, g_ffn_post)


def _rope_tables(pos):
    half = HEAD_DIM_AB // 2
    inv_freq = jnp.power(ROPE_BASE, -jnp.linspace(0.0, 1.0, half, dtype=F32))
    ang = pos.astype(F32)[:, None] * inv_freq[None, :]
    cos, sin = jnp.cos(ang), jnp.sin(ang)
    return jnp.concatenate([cos, cos], axis=-1), jnp.concatenate([-sin, sin], axis=-1)


def _pad_rows(a, rows):
    return jnp.pad(a, ((0, 0), (0, rows - a.shape[1]), (0, 0)))


def kernel(x_prompt, x_sample, cache_sb_k, cache_sb_v, state_ret, cache_swa_k, cache_swa_v, page_table,
           w_in_ab, w_out_ab, sb_bias, w_in_c, w_out_c, w_ff1, w_ff2, g_mix_pre, g_mix_post, g_ffn_pre, g_ffn_post):
    bp, t, dm = x_prompt.shape
    bs, ts, _ = x_sample.shape
    assert bp == 1 and ts <= DEC_ROWS and t % RET_CHUNK == 0
    n_pages = page_table.shape[1]
    page = cache_sb_k.shape[1]
    past_len = n_pages * page
    d = HEAD_DIM_AB
    gain = lambda g, layer: g[layer][None, :].astype(F32)
    w_in_ab_bf, w_out_ab_bf, w_in_c_bf, w_out_c_bf = _bf(w_in_ab), _bf(w_out_ab), _bf(w_in_c), _bf(w_out_c)
    w_ff1_bf, w_ff2_bf = _bf(w_ff1), _bf(w_ff2)

    h_p = x_prompt.reshape(t, dm)
    h_s = x_sample.reshape(bs * ts, dm)

    cos_p, sin_p = _rope_tables(jnp.arange(t, dtype=jnp.int32))
    rq, rk, rv, rg, q_hm, k_hm, v_hm, sb_k_p, sb_v_p = _inproj_ab(h_p, gain(g_mix_pre, 0), w_in_ab_bf, cos_p, sin_p)
    seq = lambda a: a.reshape(1, t, RET_W)
    ret_zero = jnp.zeros((1, RET_HEADS, d, d), F32)
    y_ret_p, ret_p = _retention(seq(rq), seq(rk), seq(rv), seq(rg), ret_zero, float(RET_CHUNK))
    o_sb_p = _sb_prompt(q_hm, k_hm, v_hm, sb_bias.astype(F32))
    h_p = _layer_tail([y_ret_p.reshape(t, RET_W), o_sb_p], _mix_concat, h_p, w_out_ab_bf, gain(g_mix_post, 0),
                      gain(g_ffn_pre, 0), w_ff1_bf[0], w_ff2_bf[0], gain(g_ffn_post, 0))

    cos_s, sin_s = _rope_tables(past_len + jnp.arange(ts, dtype=jnp.int32))
    tile_s = lambda a: jnp.tile(a, (bs, 1))
    rq, rk, rv, rg, q_hm, _, _, sb_k_s, sb_v_s = _inproj_ab(h_s, gain(g_mix_pre, 0), w_in_ab_bf,
                                                            tile_s(cos_s), tile_s(sin_s))
    dec = lambda a: _pad_rows(a.reshape(bs, ts, a.shape[-1]), DEC_ROWS)
    y_ret_s, ret_s = _retention(dec(rq), dec(rk), dec(rv), dec(rg), state_ret.astype(F32), float(ts))
    sq_s = jnp.transpose(q_hm.astype(F32), (1, 0, 2)).reshape(bs * ts, SB_W)
    o_sb_s = _sb_decode(dec(sq_s), dec(sb_k_s.reshape(bs * ts, SB_W)), dec(sb_v_s.reshape(bs * ts, SB_W)),
                        cache_sb_k, cache_sb_v, page_table, sb_bias.astype(F32))
    undec = lambda a: a[:, :ts].reshape(bs * ts, a.shape[-1])
    h_s = _layer_tail([undec(y_ret_s), undec(o_sb_s)], _mix_concat, h_s, w_out_ab_bf, gain(g_mix_post, 0),
                      gain(g_ffn_pre, 0), w_ff1_bf[0], w_ff2_bf[0], gain(g_ffn_post, 0))

    q, k_f32, v_f32 = _inproj_c(h_p, gain(g_mix_pre, 1), w_in_c_bf)
    h_p = _layer_tail([_dil_prompt(q, k_f32, v_f32)], _mix_concat, h_p, w_out_c_bf, gain(g_mix_post, 1),
                      gain(g_ffn_pre, 1), w_ff1_bf[1], w_ff2_bf[1], gain(g_ffn_post, 1))
    keep_p = min(DIL_WINDOW_MAX, t)
    swa_k_p = k_f32[t - keep_p:].reshape(1, keep_p, DIL_HEADS, DIL_HEAD_DIM)
    swa_v_p = v_f32[t - keep_p:].reshape(1, keep_p, DIL_HEADS, DIL_HEAD_DIM)

    q, k_f32, v_f32 = _inproj_c(h_s, gain(g_mix_pre, 1), w_in_c_bf)
    per_seq = lambda a: a.reshape(bs, ts, DIL_W)
    o_dil_s, swa_k_s, swa_v_s = _dil_decode(per_seq(q.astype(F32)), per_seq(k_f32), per_seq(v_f32),
                                            cache_swa_k.astype(F32), cache_swa_v.astype(F32))
    h_s = _layer_tail([o_dil_s.reshape(bs * ts, DIL_W)], _mix_concat, h_s, w_out_c_bf, gain(g_mix_post, 1),
                      gain(g_ffn_pre, 1), w_ff1_bf[1], w_ff2_bf[1], gain(g_ffn_post, 1))

    heads_ab = lambda a, b: a.reshape(b, -1, SB_HEADS, d)
    return (h_p.reshape(1, t, dm), h_s.reshape(bs, ts, dm),
            heads_ab(sb_k_p, 1), heads_ab(sb_v_p, 1), ret_p,
            swa_k_p, swa_v_p,
            heads_ab(sb_k_s, bs), heads_ab(sb_v_s, bs), ret_s,
            swa_k_s, swa_v_s)
```

```python
import functools

import numpy as np
import jax
import jax.numpy as jnp
from jax import lax
from jax.experimental import pallas as pl
from jax.experimental.pallas import tpu as pltpu

F32 = jnp.float32
BF16 = jnp.bfloat16

NORM_EPS = 1e-6
ROPE_BASE = 10000.0
RET_HEADS = 4
SB_HEADS = 4
HEAD_DIM_AB = 128
RET_W = RET_HEADS * HEAD_DIM_AB
SB_W = SB_HEADS * HEAD_DIM_AB
RET_CHUNK = 128
DIL_HEADS = 16
DIL_HEAD_DIM = 64
DIL_W = DIL_HEADS * DIL_HEAD_DIM
DIL_BRANCHES = ((128, 1), (512, 4), (2048, 16))
DIL_WINDOW_MAX = 2048
DIL_SPAN = 128
LANES = 128
DEC_ROWS = 8
INPROJ_ROWS = 512
VMEM_LIMIT = 56 * 1024 * 1024

_NT = (((1,), (1,)), ((), ()))
_TN = (((0,), (0,)), ((), ()))


def _cparams(sem):
    return pltpu.CompilerParams(dimension_semantics=sem, vmem_limit_bytes=VMEM_LIMIT)


def _bf(x):
    return x.astype(BF16)


def _dot(a, b):
    return jnp.dot(a, b, preferred_element_type=F32)


def _rms(x, gain):
    return x * lax.rsqrt(jnp.mean(x * x, axis=-1, keepdims=True) + NORM_EPS) * gain


def _softplus(z):
    return jnp.maximum(z, 0.0) + jnp.log1p(jnp.exp(-jnp.abs(z)))


def _inproj_ab_kernel(x_ref, g_ref, w_ref, cos_ref, sin_ref,
                      rq_ref, rk_ref, rv_ref, rg_ref, q_hm, k_hm, v_hm, sbk_ref, sbv_ref):
    a = _rms(x_ref[...], g_ref[...])
    p = _dot(_bf(a), w_ref[...])
    cosf = cos_ref[...]
    sinf = sin_ref[...]
    d = HEAD_DIM_AB
    for h in range(RET_HEADS):
        qh = p[:, h * d:(h + 1) * d]
        kh = p[:, RET_W + h * d:RET_W + (h + 1) * d]
        rq_ref[:, h * d:(h + 1) * d] = qh * cosf + pltpu.roll(qh, d // 2, 1) * sinf
        rk_ref[:, h * d:(h + 1) * d] = (kh * cosf + pltpu.roll(kh, d // 2, 1) * sinf) * (d ** -0.5)
    rv_ref[...] = p[:, 2 * RET_W:3 * RET_W]
    rg_ref[...] = p[:, 3 * RET_W:4 * RET_W]
    base = 4 * RET_W
    for h in range(SB_HEADS):
        q_hm[h] = _bf(p[:, base + h * d:base + (h + 1) * d] * (d ** -0.5 * LOG2E))
        k_hm[h] = _bf(p[:, base + SB_W + h * d:base + SB_W + (h + 1) * d])
        v_hm[h] = _bf(p[:, base + 2 * SB_W + h * d:base + 2 * SB_W + (h + 1) * d])
    tm = p.shape[0]
    for h in range(SB_HEADS):
        sbk_ref[pl.ds(h, tm, stride=SB_HEADS), :] = p[:, base + SB_W + h * d:base + SB_W + (h + 1) * d]
        sbv_ref[pl.ds(h, tm, stride=SB_HEADS), :] = p[:, base + 2 * SB_W + h * d:base + 2 * SB_W + (h + 1) * d]


def _inproj_ab(x, gain, w_bf, cosf, sinf):
    m, dm = x.shape
    n = w_bf.shape[1]
    tm = min(m, INPROJ_ROWS)
    row = lambda w: pl.BlockSpec((tm, w), lambda i: (i, 0))
    hm = pl.BlockSpec((SB_HEADS, tm, HEAD_DIM_AB), lambda i: (0, i, 0))
    f32_rows = jax.ShapeDtypeStruct((m, RET_W), F32)
    hm_shape = jax.ShapeDtypeStruct((SB_HEADS, m, HEAD_DIM_AB), BF16)
    return pl.pallas_call(
        _inproj_ab_kernel,
        grid=(m // tm,),
        in_specs=[row(dm), pl.BlockSpec((1, dm), lambda i: (0, 0)),
                  pl.BlockSpec((dm, n), lambda i: (0, 0)), row(HEAD_DIM_AB), row(HEAD_DIM_AB)],
        out_specs=[row(RET_W)] * 4 + [hm] * 3 + [pl.BlockSpec((tm * SB_HEADS, HEAD_DIM_AB), lambda i: (i, 0))] * 2,
        out_shape=[f32_rows] * 4 + [hm_shape] * 3 + [jax.ShapeDtypeStruct((m * SB_HEADS, HEAD_DIM_AB), F32)] * 2,
        compiler_params=_cparams(("parallel",)),
        name="inproj_ab",
    )(x, gain, w_bf, cosf, sinf)


def _ret_kernel(rq, rk, rv, rg, s0, intra, qdec, kdec, cdec, y_ref, s_out, s_scr, *, rows, n_chunks):
    step = pl.program_id(1)
    d = HEAD_DIM_AB

    @pl.when(step == 0)
    def _():
        s_scr[...] = s0[0]

    heads = range(RET_HEADS)
    cols = [slice(h * d, (h + 1) * d) for h in heads]
    for c in range(n_chunks):
        rs = slice(c * rows, (c + 1) * rows)
        qs = [rq[0, rs, cs] for cs in cols]
        ks = [rk[0, rs, cs] for cs in cols]
        vs = [rv[0, rs, cs] for cs in cols]
        if rows < RET_CHUNK:
            pad = jnp.zeros((RET_CHUNK - rows, d), F32)
            ks = [jnp.concatenate([k, pad], axis=0) for k in ks]
            vs = [jnp.concatenate([v, pad], axis=0) for v in vs]
        vb = [_bf(v) for v in vs]
        atts = [lax.dot_general(_bf(qs[h]), _bf(ks[h]), _NT, preferred_element_type=F32) for h in heads]
        cross = [_dot(_bf(qs[h] * qdec[h, :rows, :]), _bf(s_scr[h])) for h in heads]
        grow = [lax.dot_general(_bf(ks[h] * kdec[h]), vb[h], _TN, preferred_element_type=F32) for h in heads]
        inner = [_dot(_bf(atts[h] * intra[h, :rows, :]), vb[h]) for h in heads]
        for h in heads:
            s_scr[h] = s_scr[h] * cdec[h] + grow[h]
            o = inner[h] + cross[h]
            g = rg[0, rs, cols[h]]
            xc = o - jnp.mean(o, axis=-1, keepdims=True)
            var = jnp.mean(xc * xc, axis=-1, keepdims=True)
            y_ref[0, rs, cols[h]] = xc * lax.rsqrt(var + NORM_EPS) * (g * jax.nn.sigmoid(g))

    @pl.when(step == pl.num_programs(1) - 1)
    def _():
        s_out[0] = s_scr[...]


def _ret_tables(chunk_len):
    lg = jnp.log1p(-jnp.exp2(-5.0 - jnp.arange(RET_HEADS, dtype=F32)))
    i = jnp.arange(RET_CHUNK, dtype=F32)
    live = i < chunk_len
    rel = i[:, None] - i[None, :]
    intra = jnp.where((rel >= 0) & live[:, None] & live[None, :],
                      jnp.exp(lg[:, None, None] * jnp.maximum(rel, 0.0)), 0.0)
    q_dec = jnp.where(live[None, :], jnp.exp(lg[:, None] * (i[None, :] + 1.0)), 0.0)
    k_dec = jnp.where(live[None, :], jnp.exp(lg[:, None] * (chunk_len - 1.0 - i[None, :])), 0.0)
    c_dec = jnp.exp(lg * chunk_len)
    full = lambda t: jnp.broadcast_to(t[:, :, None], (RET_HEADS, RET_CHUNK, LANES))
    return intra, full(q_dec), full(k_dec), jnp.broadcast_to(c_dec[:, None, None], (RET_HEADS, RET_CHUNK, LANES))


def _retention(rq, rk, rv, rg, s0, chunk_len):
    b, t, _ = rq.shape
    if t % RET_CHUNK == 0:
        rows = RET_CHUNK
        n_chunks = min(8, t // rows)
    else:
        rows, n_chunks = t, 1
    tc = rows * n_chunks
    seq = pl.BlockSpec((1, tc, RET_W), lambda bi, s: (bi, s, 0))
    st = pl.BlockSpec((1, RET_HEADS, HEAD_DIM_AB, HEAD_DIM_AB), lambda bi, s: (bi, 0, 0, 0))
    tab = pl.BlockSpec((RET_HEADS, RET_CHUNK, LANES), lambda bi, s: (0, 0, 0))
    return pl.pallas_call(
        functools.partial(_ret_kernel, rows=rows, n_chunks=n_chunks),
        grid=(b, t // tc),
        in_specs=[seq] * 4 + [st] + [tab] * 4,
        out_specs=[seq, st],
        out_shape=[jax.ShapeDtypeStruct((b, t, RET_W), F32), jax.ShapeDtypeStruct(s0.shape, F32)],
        scratch_shapes=[pltpu.VMEM((RET_HEADS, HEAD_DIM_AB, HEAD_DIM_AB), F32)],
        compiler_params=_cparams(("parallel", "arbitrary")),
        name="retention",
    )(rq, rk, rv, rg, s0, *_ret_tables(chunk_len))


SB_SUB = 256
SB_ROWS = 128
LOG2E = 1.4426950408889634
SB_DECAYED = 152.0


def _sb_tile(q_ref, k_ref, v_ref, tri_ref, zbias, u_scr, sp_scr, w_scr, carry_scr, acc_scr, units, diagonal):
    d = HEAD_DIM_AB
    n_chunks = SB_SUB // SB_ROWS
    keys_of = lambda c: pl.ds(c * SB_SUB, SB_SUB)
    rows_of = lambda g: pl.ds(g * SB_SUB, SB_SUB)
    chunk_of = lambda g, r: pl.ds(g * SB_SUB + r * SB_ROWS, SB_ROWS)
    in_slot = lambda r: pl.ds(r * SB_ROWS, SB_ROWS)
    key = lax.broadcasted_iota(jnp.int32, (SB_ROWS, SB_SUB), 1)
    row = lax.broadcasted_iota(jnp.int32, (SB_ROWS, SB_SUB), 0)

    def mask_of(unit, r):
        c, g = unit
        return key < row + r * SB_ROWS if diagonal and g == c else None

    def scores(unit):
        c, g = unit
        return lax.dot_general(q_ref[rows_of(g), :], k_ref[keys_of(c), :], _NT, preferred_element_type=F32)

    def keep_logs(n, s):
        c, g = units[n]
        slot = n % 2
        for r in range(n_chunks):
            zs = s[r * SB_ROWS:(r + 1) * SB_ROWS, :] + zbias
            sp = jnp.maximum(zs, 0.0) + jnp.log2(1.0 + jnp.exp2(-jnp.abs(zs)))
            carry = carry_scr[chunk_of(g, r), :]
            u_scr[slot, in_slot(r), :] = zs - sp - jnp.tile(carry, (1, SB_SUB // LANES))
            mask = mask_of(units[n], r)
            if mask is not None:
                sp = jnp.where(mask, sp, 0.0)
            sp_scr[slot, in_slot(r), :] = _bf(sp)
            carry_scr[chunk_of(g, r), :] = carry + jnp.sum(sp, axis=1, keepdims=True)
        return _dot(sp_scr[slot], tri_ref[...])

    def weights(n, sums):
        c, g = units[n]
        slot = n % 2
        for r in range(n_chunks):
            w = jnp.exp2(u_scr[slot, in_slot(r), :] - sums[r * SB_ROWS:(r + 1) * SB_ROWS, :])
            mask = mask_of(units[n], r)
            if mask is not None:
                w = jnp.where(mask, w, 0.0)
            w_scr[slot, in_slot(r), :] = _bf(w)
        acc_scr[rows_of(g), :] += _dot(w_scr[slot], v_ref[keys_of(c), :])

    s, sums = {}, {}
    for n in range(len(units) + 2):
        if n < len(units):
            s[n] = scores(units[n])
        if 1 <= n <= len(units):
            sums[n - 1] = keep_logs(n - 1, s.pop(n - 1))
        if n >= 2:
            weights(n - 2, sums.pop(n - 2))


def _sb_prompt_kernel(qi_tab, kj_tab, bias_ref, q_ref, k_ref, v_ref, tri_ref, o_ref,
                      acc_scr, carry_scr, u_scr, sp_scr, w_scr, least_scr, *, tq):
    p = pl.program_id(0)
    qi = qi_tab[p]
    kj = kj_tab[p]
    n_heads = q_ref.shape[0]
    n_sub = tq // SB_SUB

    def tile(h, units, diagonal):
        _sb_tile(q_ref.at[h], k_ref.at[h], v_ref.at[h], tri_ref, bias_ref[h] * LOG2E, u_scr, sp_scr, w_scr,
                 carry_scr.at[h], acc_scr.at[h], units, diagonal)

    def for_heads(fn):
        def body(h, carry):
            fn(h)
            return carry
        lax.fori_loop(0, n_heads, body, 0)

    @pl.when(kj == qi)
    def _():
        acc_scr[...] = jnp.zeros_like(acc_scr)
        carry_scr[...] = jnp.zeros_like(carry_scr)

        def diagonal(h):
            tile(h, [(c, g) for c in reversed(range(n_sub)) for g in range(c, n_sub)], True)
            least_scr[h] = 0.0
        for_heads(diagonal)

    @pl.when(kj < qi)
    def _():
        def earlier(h):
            @pl.when(least_scr[h] <= SB_DECAYED)
            def _():
                tile(h, [(c, g) for c in reversed(range(n_sub)) for g in range(n_sub)], False)
                least = carry_scr[h]
                while least.shape[0] > 8:
                    half = least.shape[0] // 2
                    least = jnp.minimum(least[:half], least[half:])
                least_scr[h] = jnp.min(least)
        for_heads(earlier)

    @pl.when(kj == 0)
    def _():
        o_ref[...] = acc_scr[...].astype(o_ref.dtype)


def _tri_ones(n):
    i = np.arange(n)
    return jnp.asarray((i[:, None] > i[None, :]).astype(np.float32), BF16)


def _sb_prompt(q_hm, k_hm, v_hm, bias):
    nh, t, d = q_hm.shape
    tq = min(1024, t)
    nq = t // tq
    qi_tab = np.concatenate([np.full(i + 1, i) for i in range(nq)]).astype(np.int32)
    kj_tab = np.concatenate([np.arange(i, -1, -1) for i in range(nq)]).astype(np.int32)
    grid_spec = pltpu.PrefetchScalarGridSpec(
        num_scalar_prefetch=2,
        grid=(len(qi_tab),),
        in_specs=[
            pl.BlockSpec(memory_space=pltpu.SMEM),
            pl.BlockSpec((nh, tq, d), lambda p, qt, kt: (0, qt[p], 0)),
            pl.BlockSpec((nh, tq, d), lambda p, qt, kt: (0, kt[p], 0)),
            pl.BlockSpec((nh, tq, d), lambda p, qt, kt: (0, kt[p], 0)),
            pl.BlockSpec((SB_SUB, SB_SUB), lambda p, qt, kt: (0, 0)),
        ],
        out_specs=pl.BlockSpec((nh, tq, d), lambda p, qt, kt: (0, qt[p], 0)),
        scratch_shapes=[pltpu.VMEM((nh, tq, d), F32), pltpu.VMEM((nh, tq, LANES), F32),
                        pltpu.VMEM((2, SB_SUB, SB_SUB), F32),
                        pltpu.VMEM((2, SB_SUB, SB_SUB), BF16), pltpu.VMEM((2, SB_SUB, SB_SUB), BF16),
                        pltpu.SMEM((nh,), F32)],
    )
    return pl.pallas_call(
        functools.partial(_sb_prompt_kernel, tq=tq),
        grid_spec=grid_spec,
        out_shape=jax.ShapeDtypeStruct((nh, t, d), BF16),
        compiler_params=_cparams(("arbitrary",)),
        name="sb_prompt",
    )(jnp.asarray(qi_tab), jnp.asarray(kj_tab), bias, q_hm, k_hm, v_hm, _tri_ones(SB_SUB))


SB_PAGES_PER_STEP = 32


def _sb_decode_kernel(pt_ref, q_ref, bias_ref, knew_ref, vnew_ref, tri_ref, *rest, n_pages_step, page):
    k_pages = rest[:n_pages_step]
    v_pages = rest[n_pages_step:2 * n_pages_step]
    o_ref, acc_scr, carry_scr = rest[2 * n_pages_step:]
    g = pl.program_id(1)
    q_rows = q_ref[0]
    zbias = bias_ref[...] * LOG2E
    tri = tri_ref[...]
    d = HEAD_DIM_AB
    heads = range(SB_HEADS)
    n_rows = SB_HEADS * DEC_ROWS

    def scores(k_unit):
        s = lax.dot_general(q_rows, _bf(k_unit), _NT, preferred_element_type=F32)
        return s + jnp.tile(zbias, (1, k_unit.shape[0] // LANES))

    def keep_logs(zs, mask=None):
        sp = jnp.maximum(zs, 0.0) + jnp.log2(1.0 + jnp.exp2(-jnp.abs(zs)))
        log_beta = zs - sp
        if mask is not None:
            sp = jnp.where(mask, sp, 0.0)
        keys = zs.shape[1]
        return log_beta, _dot(_bf(sp), tri[:keys, :keys]), jnp.sum(sp, axis=1, keepdims=True)

    def weights(logs, v_heads, carry, accs, mask=None):
        log_beta, later, total = logs
        w = jnp.exp2(log_beta - later - carry)
        if mask is not None:
            w = jnp.where(mask, w, 0.0)
        accs = [accs[h] + _dot(_bf(w[h * DEC_ROWS:(h + 1) * DEC_ROWS, :]), _bf(v_heads[h])) for h in heads]
        return carry + total, accs

    def store(carry, accs):
        carry_scr[...] = jnp.broadcast_to(carry, carry_scr.shape)
        for h in heads:
            acc_scr[h * DEC_ROWS:(h + 1) * DEC_ROWS, :] = accs[h]

    @pl.when(g == 0)
    def _():
        k_new = jnp.concatenate([knew_ref[0], jnp.zeros((page - DEC_ROWS, SB_W), F32)], axis=0)
        pad = jnp.zeros((page - DEC_ROWS, d), F32)
        v_heads = [jnp.concatenate([vnew_ref[0, :, h * d:(h + 1) * d], pad], axis=0) for h in heads]
        key = lax.broadcasted_iota(jnp.int32, (n_rows, page), 1)
        qry = lax.broadcasted_iota(jnp.int32, (n_rows, page), 0) & (DEC_ROWS - 1)
        mask = key < qry
        store(*weights(keep_logs(scores(k_new), mask), v_heads, jnp.zeros((n_rows, 1), F32),
                       [jnp.zeros((DEC_ROWS, d), F32) for _ in heads], mask))

    unit_pages = tri_ref.shape[0] // page
    n_units = n_pages_step // unit_pages
    pages_of = lambda n: range(n_pages_step - (n + 1) * unit_pages, n_pages_step - n * unit_pages)
    head_rows = lambda ref, h: ref[0, pl.ds(h, page, stride=SB_HEADS), :]

    def unit_keys(n):
        return jnp.concatenate([jnp.concatenate([head_rows(k_pages[u], h) for h in heads], axis=1)
                                for u in pages_of(n)], axis=0)

    def unit_values(n):
        return [jnp.concatenate([head_rows(v_pages[u], h) for u in pages_of(n)], axis=0) for h in heads]

    carry = carry_scr[:, 0:1]
    accs = [acc_scr[h * DEC_ROWS:(h + 1) * DEC_ROWS, :] for h in heads]
    zs, logs = {}, {}
    for m in range(n_units + 2):
        if m < n_units:
            zs[m] = scores(unit_keys(m))
        if 1 <= m <= n_units:
            logs[m - 1] = keep_logs(zs.pop(m - 1))
        if m >= 2:
            carry, accs = weights(logs.pop(m - 2), unit_values(m - 2), carry, accs)
    store(carry, accs)

    @pl.when(g == pl.num_programs(1) - 1)
    def _():
        for h in heads:
            o_ref[0, :, h * d:(h + 1) * d] = acc_scr[h * DEC_ROWS:(h + 1) * DEC_ROWS, :]


def _sb_decode(sq, sk_new, sv_new, cache_k, cache_v, page_table, bias):
    b = sq.shape[0]
    n_pages = page_table.shape[1]
    page = cache_k.shape[1]
    g_pages = min(SB_PAGES_PER_STEP, n_pages)
    n_steps = n_pages // g_pages
    d = HEAD_DIM_AB
    n_rows = SB_HEADS * DEC_ROWS
    q4 = sq.reshape(b, DEC_ROWS, SB_HEADS, d)
    eye = jnp.eye(SB_HEADS, dtype=F32)
    q_rows = _bf(jnp.einsum('bihd,hg->bgihd', q4, eye).reshape(b, n_rows, SB_W))
    bias_rows = jnp.broadcast_to(jnp.repeat(bias.astype(F32), DEC_ROWS)[:, None], (n_rows, LANES))
    unit_keys = page * (2 if g_pages % 2 == 0 else 1)

    def page_spec(u):
        return pl.BlockSpec((1, page * SB_HEADS, d),
                            lambda bi, g, pt: (pt[bi * n_pages + (n_steps - 1 - g) * g_pages + u], 0, 0))

    per_seq = lambda r, w: pl.BlockSpec((1, r, w), lambda bi, g, pt: (bi, 0, 0))
    grid_spec = pltpu.PrefetchScalarGridSpec(
        num_scalar_prefetch=1,
        grid=(b, n_steps),
        in_specs=[per_seq(n_rows, SB_W), pl.BlockSpec((n_rows, LANES), lambda bi, g, pt: (0, 0)),
                  per_seq(DEC_ROWS, SB_W), per_seq(DEC_ROWS, SB_W),
                  pl.BlockSpec((unit_keys, unit_keys), lambda bi, g, pt: (0, 0))]
                 + [page_spec(u) for u in range(g_pages)] * 2,
        out_specs=per_seq(DEC_ROWS, SB_W),
        scratch_shapes=[pltpu.VMEM((n_rows, d), F32), pltpu.VMEM((n_rows, LANES), F32)],
    )
    return pl.pallas_call(
        functools.partial(_sb_decode_kernel, n_pages_step=g_pages, page=page),
        grid_spec=grid_spec,
        out_shape=jax.ShapeDtypeStruct((b, DEC_ROWS, SB_W), F32),
        compiler_params=_cparams(("parallel", "arbitrary")),
        name="sb_decode",
    )(page_table.reshape(-1), q_rows, bias_rows, sk_new, sv_new, _tri_ones(unit_keys),
      *([cache_k.reshape(-1, page * SB_HEADS, d)] * g_pages), *([cache_v.reshape(-1, page * SB_HEADS, d)] * g_pages))


def _inproj_c_kernel(x_ref, g_ref, w_ref, q_ref, k_ref, v_ref):
    a = _rms(x_ref[...], g_ref[...])
    p = _dot(_bf(a), w_ref[...])
    q_ref[...] = p[:, :DIL_W]
    k_ref[...] = p[:, DIL_W:2 * DIL_W]
    v_ref[...] = p[:, 2 * DIL_W:]


def _inproj_c(x, gain, w_bf):
    m, dm = x.shape
    tm = min(m, INPROJ_ROWS)
    row = pl.BlockSpec((tm, DIL_W), lambda i: (i, 0))
    return pl.pallas_call(
        _inproj_c_kernel,
        grid=(m // tm,),
        in_specs=[pl.BlockSpec((tm, dm), lambda i: (i, 0)), pl.BlockSpec((1, dm), lambda i: (0, 0)),
                  pl.BlockSpec((dm, 3 * DIL_W), lambda i: (0, 0))],
        out_specs=[row] * 3,
        out_shape=[jax.ShapeDtypeStruct((m, DIL_W), F32)] * 3,
        compiler_params=_cparams(("parallel",)),
        name="inproj_c",
    )(x, gain, w_bf)


DIL_SUPER = DIL_WINDOW_MAX
DIL_SKEW = 3


def _dil_prompt_kernel(q_ref, kp_ref, kc_ref, vp_ref, vc_ref, o_ref, o_scr, lse_scr):
    sb = pl.program_id(0)
    blk = DIL_SPAN

    def window(prev_ref, cur_ref, dil, r, j):
        if j > 0:
            return cur_ref[pl.ds(r + dil * blk * (j - 1), 2 * blk, stride=dil), :]
        return jnp.concatenate([prev_ref[pl.ds(DIL_SUPER - dil * blk + r, blk, stride=dil), :],
                                cur_ref[pl.ds(r, blk, stride=dil), :]], axis=0)

    rows = lax.broadcasted_iota(jnp.int32, (blk, 2 * blk), 0)
    cols = lax.broadcasted_iota(jnp.int32, (blk, 2 * blk), 1)
    dist = blk + rows - cols
    in_span = (dist >= 0) & (dist <= DIL_SPAN)
    in_span_first = in_span & ((cols >= blk) | (sb > 0))
    lane = lax.broadcasted_iota(jnp.int32, (blk, LANES), 1)
    first = lane < DIL_HEAD_DIM
    scale = DIL_HEAD_DIM ** -0.5
    units = [(g, dil, r, j) for g, (_, dil) in enumerate(DIL_BRANCHES)
             for j in range(DIL_SUPER // (dil * blk)) for r in range(dil)]

    def scores(unit):
        g, dil, r, j = unit
        q = q_ref[pl.ds(r + dil * blk * j, blk, stride=dil), :] * scale
        k2 = _bf(window(kp_ref, kc_ref, dil, r, j))
        zero = jnp.zeros_like(q)
        return [lax.dot_general(_bf(qh), k2, _NT, preferred_element_type=F32)
                for qh in (jnp.where(first, q, zero), jnp.where(first, zero, q))]

    def attend(unit, zs):
        g, dil, r, j = unit
        v2 = _bf(window(vp_ref, vc_ref, dil, r, j))
        valid = in_span_first if j == 0 else in_span
        nums, tops, dens = [], [], []
        for z in zs:
            z = jnp.where(valid, z, -jnp.inf)
            m = jnp.max(z, axis=-1, keepdims=True)
            e = jnp.exp(z - m)
            dens.append(jnp.sum(e, axis=-1, keepdims=True))
            tops.append(m)
            nums.append(_dot(_bf(e), v2))
        den = jnp.where(first, dens[0], dens[1])
        dst = pl.ds(r + dil * blk * j, blk, stride=dil)
        o_scr[g, dst, :] = jnp.where(first, nums[0], nums[1]) / den
        lse_scr[g, dst, :] = jnp.where(first, tops[0], tops[1]) + jnp.log(den)

    pending = {}
    for n in range(len(units) + DIL_SKEW):
        if n < len(units):
            pending[n] = scores(units[n])
        if n >= DIL_SKEW:
            attend(units[n - DIL_SKEW], pending.pop(n - DIL_SKEW))

    n_br = len(DIL_BRANCHES)
    for c in range(DIL_SUPER // blk):
        rs = slice(c * blk, (c + 1) * blk)
        lses = [lse_scr[g, rs, :] for g in range(n_br)]
        top = functools.reduce(jnp.maximum, lses)
        ws = [jnp.exp(l - top) for l in lses]
        num = functools.reduce(lambda a, b: a + b, [w * o_scr[g, rs, :] for g, w in enumerate(ws)])
        o_ref[rs, :] = (num / functools.reduce(lambda a, b: a + b, ws)).astype(o_ref.dtype)


def _dil_prompt(q, k, v):
    t = q.shape[0]
    assert t % DIL_SUPER == 0
    cur = pl.BlockSpec((DIL_SUPER, LANES), lambda s, hp: (s, hp))
    prev = pl.BlockSpec((DIL_SUPER, LANES), lambda s, hp: (jnp.maximum(s - 1, 0), hp))
    n_br = len(DIL_BRANCHES)
    return pl.pallas_call(
        _dil_prompt_kernel,
        grid=(t // DIL_SUPER, DIL_W // LANES),
        in_specs=[cur, prev, cur, prev, cur],
        out_specs=cur,
        out_shape=jax.ShapeDtypeStruct((t, DIL_W), BF16),
        scratch_shapes=[pltpu.VMEM((n_br, DIL_SUPER, LANES), F32), pltpu.VMEM((n_br, DIL_SUPER, LANES), F32)],
        compiler_params=_cparams(("parallel", "parallel")),
        name="dilated_prompt",
    )(q, k, k, v, v)


def _dil_multiplicity(dist):
    cnt = jnp.zeros(dist.shape, F32)
    for window, dil in DIL_BRANCHES:
        hit = (dist >= 0) & (dist <= window) & ((dist & (dil - 1)) == 0)
        cnt = cnt + jnp.where(hit, 1.0, 0.0)
    return cnt


def _dil_decode_kernel(q_ref, kt_ref, vt_ref, knew_ref, vnew_ref, o_ref, kout_ref, vout_ref, *, n_past, n_new):
    scale = DIL_HEAD_DIM ** -0.5
    heads = range(kt_ref.shape[1])
    qi = lax.broadcasted_iota(jnp.int32, (DEC_ROWS, n_past), 0)
    key = lax.broadcasted_iota(jnp.int32, (DEC_ROWS, n_past), 1)
    cnt_past = _dil_multiplicity(n_past + qi - key)
    qi_new = lax.broadcasted_iota(jnp.int32, (DEC_ROWS, LANES), 0)
    lane = lax.broadcasted_iota(jnp.int32, (DEC_ROWS, LANES), 1)
    j_new = lane - (LANES - n_new)
    cnt_new = jnp.where(j_new >= 0, _dil_multiplicity(qi_new - j_new), 0.0)

    qs = [_bf(q_ref[0, h]) for h in heads]
    z_past = [_dot(qs[h], _bf(kt_ref[0, h])) * scale for h in heads]
    z_new = [_dot(qs[h], _bf(knew_ref[0, h])) * scale for h in heads]
    ps = []
    for h in heads:
        m = jnp.maximum(jnp.max(jnp.where(cnt_past > 0, z_past[h], -jnp.inf), axis=1, keepdims=True),
                        jnp.max(jnp.where(cnt_new > 0, z_new[h], -jnp.inf), axis=1, keepdims=True))
        p_past = jnp.where(cnt_past > 0, cnt_past * jnp.exp(z_past[h] - m), 0.0)
        p_new = jnp.where(cnt_new > 0, cnt_new * jnp.exp(z_new[h] - m), 0.0)
        den = jnp.sum(p_past, axis=1, keepdims=True) + jnp.sum(p_new, axis=1, keepdims=True)
        ps.append((p_past, p_new, den))
    for h in heads:
        p_past, p_new, den = ps[h]
        num = (lax.dot_general(_bf(p_past), _bf(vt_ref[0, h]), _NT, preferred_element_type=F32)
               + lax.dot_general(_bf(p_new), _bf(vnew_ref[0, h]), _NT, preferred_element_type=F32))
        o_ref[0, h] = num / den

    tail = lax.broadcasted_iota(jnp.int32, (DIL_HEAD_DIM, LANES), 1) >= LANES - n_new
    for src, new, dst in ((kt_ref, knew_ref, kout_ref), (vt_ref, vnew_ref, vout_ref)):
        for h in heads:
            shifted = pltpu.roll(src[0, h], n_past - n_new, 1)
            dst[0, h, :, :n_past - LANES] = shifted[:, :n_past - LANES]
            dst[0, h, :, n_past - LANES:] = jnp.where(tail, new[0, h], shifted[:, n_past - LANES:])


DIL_DEC_HEADS_PER_STEP = 4


def _dil_decode(q, k_new, v_new, cache_k, cache_v):
    b, n_new, _ = q.shape
    n_past = cache_k.shape[1]
    assert n_past == DIL_WINDOW_MAX and n_new <= DEC_ROWS
    hd = (DIL_HEADS, DIL_HEAD_DIM)
    to_t = lambda a: jnp.transpose(a, (0, 2, 3, 1))
    new_t = lambda a: jnp.pad(to_t(a.reshape(b, n_new, *hd)), ((0, 0), (0, 0), (0, 0), (LANES - n_new, 0)))
    q_rows = jnp.pad(jnp.transpose(q.reshape(b, n_new, *hd), (0, 2, 1, 3)),
                     ((0, 0), (0, 0), (0, DEC_ROWS - n_new), (0, 0)))
    hg = DIL_DEC_HEADS_PER_STEP
    spec = lambda r, w: pl.BlockSpec((1, hg, r, w), lambda bi, g: (bi, g, 0, 0))
    cache_shape = jax.ShapeDtypeStruct((b, DIL_HEADS, DIL_HEAD_DIM, n_past), F32)
    o, k_out, v_out = pl.pallas_call(
        functools.partial(_dil_decode_kernel, n_past=n_past, n_new=n_new),
        grid=(b, DIL_HEADS // hg),
        in_specs=[spec(DEC_ROWS, DIL_HEAD_DIM), spec(DIL_HEAD_DIM, n_past), spec(DIL_HEAD_DIM, n_past),
                  spec(DIL_HEAD_DIM, LANES), spec(DIL_HEAD_DIM, LANES)],
        out_specs=[spec(DEC_ROWS, DIL_HEAD_DIM), spec(DIL_HEAD_DIM, n_past), spec(DIL_HEAD_DIM, n_past)],
        out_shape=[jax.ShapeDtypeStruct((b, DIL_HEADS, DEC_ROWS, DIL_HEAD_DIM), F32), cache_shape, cache_shape],
        compiler_params=_cparams(("parallel", "parallel")),
        name="dilated_decode",
    )(q_rows, to_t(cache_k), to_t(cache_v), new_t(k_new), new_t(v_new))
    from_t = lambda a: jnp.transpose(a, (0, 3, 1, 2))
    o = jnp.transpose(o[:, :, :n_new, :], (0, 2, 1, 3)).reshape(b, n_new, DIL_W)
    return o, from_t(k_out), from_t(v_out)


def _mix_concat(*refs):
    parts = []
    for r in refs:
        parts += [_bf(r[h]) for h in range(r.shape[0])] if len(r.shape) == 3 else [_bf(r[...])]
    return jnp.concatenate(parts, axis=-1)


def _tail_kernel(*refs, n_mix, mix_fn):
    mix_refs = refs[:n_mix]
    (h_ref, wout_ref, gpost_ref, gpre_ref, w1_ref, w2_ref, gffn_ref,
     out_ref, h1_scr, a_scr, acc_scr) = refs[n_mix:]
    f = pl.program_id(1)

    @pl.when(f == 0)
    def _():
        m = _dot(mix_fn(*mix_refs), wout_ref[...])
        h1 = h_ref[...] + _rms(m, gpost_ref[...])
        h1_scr[...] = h1
        a_scr[...] = _bf(_rms(h1, gpre_ref[...]))
        acc_scr[...] = jnp.zeros_like(acc_scr)

    hid = jnp.square(jnp.maximum(_dot(a_scr[...], w1_ref[...]), 0.0))
    acc_scr[...] += _dot(_bf(hid), w2_ref[...])

    @pl.when(f == pl.num_programs(1) - 1)
    def _():
        out_ref[...] = h1_scr[...] + _rms(acc_scr[...], gffn_ref[...])


TAIL_ROWS = 1024
TAIL_FF = 512


def _layer_tail(mix, mix_fn, h, w_out, g_post, g_ffn_pre, w1, w2, g_ffn_post):
    m, dm = h.shape
    dff = w1.shape[1]
    tm = min(m, TAIL_ROWS)
    tf = min(dff, TAIL_FF)
    row = lambda w: pl.BlockSpec((tm, w), lambda i, f: (i, 0))
    gain = pl.BlockSpec((1, dm), lambda i, f: (0, 0))
    return pl.pallas_call(
        functools.partial(_tail_kernel, n_mix=len(mix), mix_fn=mix_fn),
        grid=(m // tm, dff // tf),
        in_specs=[row(a.shape[1]) if a.ndim == 2 else pl.BlockSpec((a.shape[0], tm, a.shape[2]), lambda i, f: (0, i, 0))
                  for a in mix]
                 + [row(dm), pl.BlockSpec(w_out.shape, lambda i, f: (0, 0)), gain, gain,
                    pl.BlockSpec((dm, tf), lambda i, f: (0, f)), pl.BlockSpec((tf, dm), lambda i, f: (f, 0)), gain],
        out_specs=row(dm),
        out_shape=jax.ShapeDtypeStruct((m, dm), F32),
        scratch_shapes=[pltpu.VMEM((tm, dm), F32), pltpu.VMEM((tm, dm), BF16), pltpu.VMEM((tm, dm), F32)],
        compiler_params=_cparams(("parallel", "arbitrary")),
        name="layer_tail",
    )(*mix, h, w_out, g_post, g_ffn_pre, w1, w2, g_ffn_post)


def _rope_tables(pos):
    half = HEAD_DIM_AB // 2
    inv_freq = jnp.power(ROPE_BASE, -jnp.linspace(0.0, 1.0, half, dtype=F32))
    ang = pos.astype(F32)[:, None] * inv_freq[None, :]
    cos, sin = jnp.cos(ang), jnp.sin(ang)
    return jnp.concatenate([cos, cos], axis=-1), jnp.concatenate([-sin, sin], axis=-1)


def _pad_rows(a, rows):
    return jnp.pad(a, ((0, 0), (0, rows - a.shape[1]), (0, 0)))


def kernel(x_prompt, x_sample, cache_sb_k, cache_sb_v, state_ret, cache_swa_k, cache_swa_v, page_table,
           w_in_ab, w_out_ab, sb_bias, w_in_c, w_out_c, w_ff1, w_ff2, g_mix_pre, g_mix_post, g_ffn_pre, g_ffn_post):
    bp, t, dm = x_prompt.shape
    bs, ts, _ = x_sample.shape
    assert bp == 1 and ts <= DEC_ROWS and t % RET_CHUNK == 0
    n_pages = page_table.shape[1]
    page = cache_sb_k.shape[1]
    past_len = n_pages * page
    d = HEAD_DIM_AB
    gain = lambda g, layer: g[layer][None, :].astype(F32)
    w_in_ab_bf, w_out_ab_bf, w_in_c_bf, w_out_c_bf = _bf(w_in_ab), _bf(w_out_ab), _bf(w_in_c), _bf(w_out_c)
    w_ff1_bf, w_ff2_bf = _bf(w_ff1), _bf(w_ff2)

    h_p = x_prompt.reshape(t, dm)
    h_s = x_sample.reshape(bs * ts, dm)

    cos_p, sin_p = _rope_tables(jnp.arange(t, dtype=jnp.int32))
    rq, rk, rv, rg, q_hm, k_hm, v_hm, sb_k_p, sb_v_p = _inproj_ab(h_p, gain(g_mix_pre, 0), w_in_ab_bf, cos_p, sin_p)
    seq = lambda a: a.reshape(1, t, RET_W)
    ret_zero = jnp.zeros((1, RET_HEADS, d, d), F32)
    y_ret_p, ret_p = _retention(seq(rq), seq(rk), seq(rv), seq(rg), ret_zero, float(RET_CHUNK))
    o_sb_p = _sb_prompt(q_hm, k_hm, v_hm, sb_bias.astype(F32))
    h_p = _layer_tail([y_ret_p.reshape(t, RET_W), o_sb_p], _mix_concat, h_p, w_out_ab_bf, gain(g_mix_post, 0),
                      gain(g_ffn_pre, 0), w_ff1_bf[0], w_ff2_bf[0], gain(g_ffn_post, 0))

    cos_s, sin_s = _rope_tables(past_len + jnp.arange(ts, dtype=jnp.int32))
    tile_s = lambda a: jnp.tile(a, (bs, 1))
    rq, rk, rv, rg, q_hm, _, _, sb_k_s, sb_v_s = _inproj_ab(h_s, gain(g_mix_pre, 0), w_in_ab_bf,
                                                            tile_s(cos_s), tile_s(sin_s))
    dec = lambda a: _pad_rows(a.reshape(bs, ts, a.shape[-1]), DEC_ROWS)
    y_ret_s, ret_s = _retention(dec(rq), dec(rk), dec(rv), dec(rg), state_ret.astype(F32), float(ts))
    sq_s = jnp.transpose(q_hm.astype(F32), (1, 0, 2)).reshape(bs * ts, SB_W)
    o_sb_s = _sb_decode(dec(sq_s), dec(sb_k_s.reshape(bs * ts, SB_W)), dec(sb_v_s.reshape(bs * ts, SB_W)),
                        cache_sb_k, cache_sb_v, page_table, sb_bias.astype(F32))
    undec = lambda a: a[:, :ts].reshape(bs * ts, a.shape[-1])
    h_s = _layer_tail([undec(y_ret_s), undec(o_sb_s)], _mix_concat, h_s, w_out_ab_bf, gain(g_mix_post, 0),
                      gain(g_ffn_pre, 0), w_ff1_bf[0], w_ff2_bf[0], gain(g_ffn_post, 0))

    q, k_f32, v_f32 = _inproj_c(h_p, gain(g_mix_pre, 1), w_in_c_bf)
    h_p = _layer_tail([_dil_prompt(q, k_f32, v_f32)], _mix_concat, h_p, w_out_c_bf, gain(g_mix_post, 1),
                      gain(g_ffn_pre, 1), w_ff1_bf[1], w_ff2_bf[1], gain(g_ffn_post, 1))
    keep_p = min(DIL_WINDOW_MAX, t)
    swa_k_p = k_f32[t - keep_p:].reshape(1, keep_p, DIL_HEADS, DIL_HEAD_DIM)
    swa_v_p = v_f32[t - keep_p:].reshape(1, keep_p, DIL_HEADS, DIL_HEAD_DIM)

    q, k_f32, v_f32 = _inproj_c(h_s, gain(g_mix_pre, 1), w_in_c_bf)
    per_seq = lambda a: a.reshape(bs, ts, DIL_W)
    o_dil_s, swa_k_s, swa_v_s = _dil_decode(per_seq(q.astype(F32)), per_seq(k_f32), per_seq(v_f32),
                                            cache_swa_k.astype(F32), cache_swa_v.astype(F32))
    h_s = _layer_tail([o_dil_s.reshape(bs * ts, DIL_W)], _mix_concat, h_s, w_out_c_bf, gain(g_mix_post, 1),
                      gain(g_ffn_pre, 1), w_ff1_bf[1], w_ff2_bf[1], gain(g_ffn_post, 1))

    heads_ab = lambda a, b: a.reshape(b, -1, SB_HEADS, d)
    return (h_p.reshape(1, t, dm), h_s.reshape(bs, ts, dm),
            heads_ab(sb_k_p, 1), heads_ab(sb_v_p, 1), ret_p,
            swa_k_p, swa_v_p,
            heads_ab(sb_k_s, bs), heads_ab(sb_v_s, bs), ret_s,
            swa_k_s, swa_v_s)
```
